```python
import math
import jax, jax.numpy as jnp
from jax import lax
import numpy as np

D_MODEL = 1024
BATCH = 8
SEQ = 2048
DEPTH = 2
DEC_BATCH = 4
DEC_SEQ = 4096
PAST_LEN = 128

HEAD_DIM = 64
BLOCK = 128
A_GROUPS = 8
A_WIDTH = A_GROUPS * HEAD_DIM
B_HEADS = 8
B_KV = 2
WINDOW = 128
C_HEADS = 8
C_KV = 2
ROPE_THETA = 10000.0
GRID_W = 64
N_BUCKETS = 32
MAX_DIST = 128
D_FF = 2816
CONV_W = 3
EPS = 1e-6
N_BRANCH = 3
BR_WIDTH = 512
A_IN = 2 * A_WIDTH
B_IN = (B_HEADS + 2 * B_KV) * HEAD_DIM
C_IN = (C_HEADS + 2 * C_KV) * HEAD_DIM
IN_WIDTH = A_IN + B_IN + C_IN

kernel_name = 'hybrid_gated_parallel_encoder'

f32 = jnp.float32


def rmsnorm(x, g):
    xf = x.astype(f32)
    y = xf * lax.rsqrt(jnp.mean(xf * xf, axis=-1, keepdims=True) + EPS)
    return (y * g.astype(f32)).astype(x.dtype)


def layernorm(x, g, b):
    xf = x.astype(f32)
    mu = jnp.mean(xf, axis=-1, keepdims=True)
    xc = xf - mu
    y = xc * lax.rsqrt(jnp.mean(xc * xc, axis=-1, keepdims=True) + EPS)
    return (y * g.astype(f32) + b.astype(f32)).astype(x.dtype)


def t5_bucket(rel):
    half = N_BUCKETS // 2
    max_exact = half // 2
    ret = jnp.where(rel > 0, half, 0)
    n = jnp.abs(rel)
    nf = jnp.maximum(n, 1).astype(f32)
    large = max_exact + (jnp.log(nf / max_exact) / math.log(MAX_DIST / max_exact)
                         * (half - max_exact)).astype(jnp.int32)
    large = jnp.minimum(large, half - 1)
    return ret + jnp.where(n < max_exact, n, large)


def mixer_a(z, ln_g, ln_b, w_s, b_s):
    bsz, s, _ = z.shape
    nb = s // BLOCK
    z = jax.nn.gelu(z)
    u, v = jnp.split(z, 2, axis=-1)
    v = layernorm(v, ln_g, ln_b).reshape(bsz, nb, BLOCK, A_GROUPS, HEAD_DIM)
    sv = jnp.einsum('gpq,bnqgc->bnpgc', w_s, v) + b_s.T[None, None, :, :, None]
    return u * sv.reshape(bsz, s, A_WIDTH)


def mixer_b(q, k, v, sink, bias, win):
    bsz, s = q.shape[:2]
    nb = s // BLOCK
    g = B_HEADS // B_KV

    def band(t):
        tp = jnp.pad(t, ((0, 0), (BLOCK, BLOCK), (0, 0), (0, 0)))
        tp = tp.reshape(bsz, nb + 2, BLOCK, B_KV, HEAD_DIM)
        return jnp.concatenate([tp[:, :-2], tp[:, 1:-1], tp[:, 2:]], axis=2)

    kb, vb = band(k), band(v)
    qb = q.reshape(bsz, nb, BLOCK, B_KV, g, HEAD_DIM)
    logits = jnp.einsum('bnqkgd,bnjkd->bnkgqj', qb, kb).astype(f32) * (HEAD_DIM ** -0.5)
    logits = logits + bias.reshape(B_KV, g, BLOCK, 3 * BLOCK)
    kpos = (jnp.arange(nb)[:, None] - 1) * BLOCK + jnp.arange(3 * BLOCK)[None, :]
    valid = (kpos >= 0) & (kpos < s)
    mask = win[None] & valid[:, None, :]
    logits = jnp.where(mask[None, :, None, None], logits, -jnp.inf)
    sk = sink.astype(f32).reshape(B_KV, g, 1, 1)
    m = jnp.maximum(jnp.max(logits, axis=-1, keepdims=True), sk)
    p = jnp.exp(logits - m)
    probs = p / (jnp.sum(p, axis=-1, keepdims=True) + jnp.exp(sk - m))
    o = jnp.einsum('bnkgqj,bnjkd->bnqkgd', probs.astype(v.dtype), vb)
    return o.reshape(bsz, s, B_HEADS * HEAD_DIM)


def rope_axis(x, pos):
    m = x.shape[-1] // 2
    inv = ROPE_THETA ** (-jnp.arange(m, dtype=f32) / m)
    ang = pos.astype(f32)[:, None] * inv[None, :]
    cos = jnp.cos(ang)[:, None, :]
    sin = jnp.sin(ang)[:, None, :]
    x1 = x[..., :m].astype(f32)
    x2 = x[..., m:].astype(f32)
    return jnp.concatenate([x1 * cos - x2 * sin, x2 * cos + x1 * sin], axis=-1).astype(x.dtype)


def rope_2d(x, row, col):
    h = HEAD_DIM // 2
    return jnp.concatenate([rope_axis(x[..., :h], row), rope_axis(x[..., h:], col)], axis=-1)


def mixer_c(q, k, v, qg, kg, row, col):
    bsz, s = q.shape[:2]
    nb = s // BLOCK
    g = C_HEADS // C_KV
    q = rope_2d(rmsnorm(q, qg), row, col)
    k = rope_2d(rmsnorm(k, kg), row, col)
    qb = q.reshape(bsz, nb, BLOCK, C_KV, g, HEAD_DIM).transpose(1, 0, 2, 3, 4, 5)
    scale = HEAD_DIM ** -0.5

    def attend(qblk):
        logits = jnp.einsum('bqkgd,bskd->bkgqs', qblk, k).astype(f32) * scale
        p = jax.nn.softmax(logits, axis=-1).astype(v.dtype)
        return jnp.einsum('bkgqs,bskd->bqkgd', p, v)

    o = lax.map(attend, qb)
    return o.transpose(1, 0, 2, 3, 4, 5).reshape(bsz, s, C_HEADS * HEAD_DIM)


def conv_ffn(x, w_up, cw, cb, w_down):
    h = x @ w_up
    hp = jnp.pad(h, ((0, 0), (1, 1), (0, 0)))
    h = hp[:, :-2] * cw[0] + hp[:, 1:-1] * cw[1] + hp[:, 2:] * cw[2] + cb
    gt, val = jnp.split(h, 2, axis=-1)
    return (jax.nn.silu(gt) * val) @ w_down


def trunk(x, bias_b, win, norm1_g, w_in, ln_v_g, ln_v_b, w_spatial, b_spatial, sink,
          q_norm_g, k_norm_g, w_gate, b_gate, w_branch, w_out, norm2_g, w_up, conv_w,
          conv_b, w_down, final_g):
    bsz, s, d = x.shape
    rows = s // GRID_W
    row = jnp.repeat(jnp.arange(rows), GRID_W)
    col = jnp.tile(jnp.arange(GRID_W), rows)
    hb = B_HEADS * HEAD_DIM
    kvb = B_KV * HEAD_DIM
    hc = C_HEADS * HEAD_DIM
    kvc = C_KV * HEAD_DIM
    for l in range(DEPTH):
        xn = rmsnorm(x, norm1_g[l])
        z = xn @ w_in[l]
        za = z[..., :A_IN]
        zb = z[..., A_IN:A_IN + B_IN]
        zc = z[..., A_IN + B_IN:]
        o_a = mixer_a(za, ln_v_g[l], ln_v_b[l], w_spatial[l], b_spatial[l])
        o_b = mixer_b(zb[..., :hb].reshape(bsz, s, B_HEADS, HEAD_DIM),
                      zb[..., hb:hb + kvb].reshape(bsz, s, B_KV, HEAD_DIM),
                      zb[..., hb + kvb:].reshape(bsz, s, B_KV, HEAD_DIM),
                      sink[l], bias_b, win)
        o_c = mixer_c(zc[..., :hc].reshape(bsz, s, C_HEADS, HEAD_DIM),
                      zc[..., hc:hc + kvc].reshape(bsz, s, C_KV, HEAD_DIM),
                      zc[..., hc + kvc:].reshape(bsz, s, C_KV, HEAD_DIM),
                      q_norm_g[l], k_norm_g[l], row, col)
        o = jnp.stack([o_a, o_b, o_c], axis=2)
        y = jnp.einsum('bsne,ned->bsnd', o, w_branch[l])
        gates = jax.nn.sigmoid(xn @ w_gate[l] + b_gate[l]).reshape(bsz, s, N_BRANCH, d)
        merged = jnp.sum(gates * y, axis=2)
        x = x + merged @ w_out[l]
        x = x + conv_ffn(rmsnorm(x, norm2_g[l]), w_up[l], conv_w[l], conv_b[l], w_down[l])
    return rmsnorm(x, final_g)


def setup_inputs(seed: int = 0) -> dict:
    key = jax.random.key(seed)
    ks = jax.random.split(key, 24)

    def nrm(k, shape, scale):
        return jax.random.normal(k, shape, f32) * scale

    L, D = DEPTH, D_MODEL
    return {
        'x_prompt': nrm(ks[0], (BATCH, SEQ, D), 1.0),
        'x_sample': nrm(ks[1], (DEC_BATCH, DEC_SEQ, D), 1.0),
        'rel_bias': nrm(ks[2], (N_BUCKETS, B_HEADS), 0.5),
        'norm1_g': 1.0 + nrm(ks[3], (L, D), 0.1),
        'w_in': nrm(ks[4], (L, D, IN_WIDTH), D ** -0.5),
        'ln_v_g': 1.0 + nrm(ks[5], (L, A_WIDTH), 0.1),
        'ln_v_b': nrm(ks[6], (L, A_WIDTH), 0.05),
        'w_spatial': nrm(ks[7], (L, A_GROUPS, BLOCK, BLOCK), BLOCK ** -0.5),
        'b_spatial': 1.0 + nrm(ks[8], (L, A_GROUPS, BLOCK), 0.1),
        'sink': nrm(ks[9], (L, B_HEADS), 0.5),
        'q_norm_g': 1.0 + nrm(ks[10], (L, HEAD_DIM), 0.1),
        'k_norm_g': 1.0 + nrm(ks[11], (L, HEAD_DIM), 0.1),
        'w_gate': nrm(ks[12], (L, D, N_BRANCH * D), D ** -0.5),
        'b_gate': nrm(ks[13], (L, N_BRANCH * D), 0.05),
        'w_branch': nrm(ks[14], (L, N_BRANCH, BR_WIDTH, D), BR_WIDTH ** -0.5),
        'w_out': nrm(ks[15], (L, D, D), D ** -0.5),
        'norm2_g': 1.0 + nrm(ks[16], (L, D), 0.1),
        'w_up': nrm(ks[17], (L, D, 2 * D_FF), D ** -0.5),
        'conv_w': nrm(ks[18], (L, CONV_W, 2 * D_FF), CONV_W ** -0.5),
        'conv_b': nrm(ks[19], (L, 2 * D_FF), 0.05),
        'w_down': nrm(ks[20], (L, D_FF, D), D_FF ** -0.5),
        'final_g': 1.0 + nrm(ks[21], (D,), 0.1),
    }


def reference(x_prompt, x_sample, rel_bias, norm1_g, w_in, ln_v_g, ln_v_b, w_spatial,
              b_spatial, sink, q_norm_g, k_norm_g, w_gate, b_gate, w_branch, w_out,
              norm2_g, w_up, conv_w, conv_b, w_down, final_g):
    qpos = jnp.arange(BLOCK)[:, None]
    jpos = jnp.arange(3 * BLOCK)[None, :]
    rel = jpos - BLOCK - qpos
    win = jnp.abs(rel) <= WINDOW
    bias_b = rel_bias[t5_bucket(rel)].transpose(2, 0, 1).astype(f32)
    y_prompt = trunk(x_prompt, bias_b, win, norm1_g, w_in, ln_v_g, ln_v_b, w_spatial,
                     b_spatial, sink, q_norm_g, k_norm_g, w_gate, b_gate, w_branch, w_out,
                     norm2_g, w_up, conv_w, conv_b, w_down, final_g)
    y_sample = trunk(x_sample, bias_b, win, norm1_g, w_in, ln_v_g, ln_v_b, w_spatial,
                     b_spatial, sink, q_norm_g, k_norm_g, w_gate, b_gate, w_branch, w_out,
                     norm2_g, w_up, conv_w, conv_b, w_down, final_g)
    return (y_prompt, y_sample)
```

```python
import functools
import math

import jax
import jax.numpy as jnp
from jax import lax
from jax.experimental import pallas as pl
from jax.experimental.pallas import tpu as pltpu

D_MODEL = 1024
DEPTH = 2
HEAD_DIM = 64
BLOCK = 128
A_GROUPS = 8
A_WIDTH = A_GROUPS * HEAD_DIM
N_HEADS = 8
N_KV = 2
GROUP = N_HEADS // N_KV
WINDOW = 128
ROPE_THETA = 10000.0
GRID_W = 64
N_BUCKETS = 32
MAX_DIST = 128
D_FF = 2816
EPS = 1e-6
N_BRANCH = 3
BR_WIDTH = 512
Q_WIDTH = N_HEADS * HEAD_DIM
KV_WIDTH = N_KV * HEAD_DIM
QKV_WIDTH = Q_WIDTH + 2 * KV_WIDTH
A_IN = 2 * A_WIDTH
IN_WIDTH = A_IN + 2 * QKV_WIDTH
K_COL_BLOCK = Q_WIDTH // KV_WIDTH
V_COL_BLOCK = K_COL_BLOCK + 1

LANES = 128
SUBLANES = 8
VMEM_LIMIT = 56 * 1024 * 1024

TM_IN = 256
TM_MERGE = 256
TM_FFN = 512
FF_CHUNK = 256
TQ_FLASH = 128
TK_FLASH = 512

f32 = jnp.float32
bf16 = jnp.bfloat16


def _rms_scale(x):
    return lax.rsqrt(jnp.mean(x * x, axis=-1, keepdims=True) + EPS)


def _gelu_tanh(x):
    c = math.sqrt(2.0 / math.pi)
    return x * (0.5 * (1.0 + jnp.tanh(c * (x + 0.044715 * (x * x * x)))))


def _sigmoid(x):
    return 1.0 / (1.0 + jnp.exp(-x))


def _dot(a, b):
    return jnp.dot(a, b, preferred_element_type=f32)


def _dot_nt(a, b):
    return lax.dot_general(a, b, (((1,), (1,)), ((), ())), preferred_element_type=f32)


def _head_mean_sq(x, seg_ref):
    sq = x * x
    hi = sq.astype(bf16)
    lo = (sq - hi.astype(f32)).astype(bf16)
    seg = seg_ref[...]
    return _dot(hi, seg) + _dot(lo, seg)


def _swap16(x):
    lane = lax.broadcasted_iota(jnp.int32, x.shape, 1)
    first_half = (lane % 32) < 16
    return jnp.where(first_half, pltpu.roll(x, LANES - 16, 1), pltpu.roll(x, 16, 1))


def _in_proj_kernel(x_ref, g1_ref, w_ref, lng_ref, lnb_ref, wsp_ref, bsp_ref,
                    qg_ref, kg_ref, cos_ref, sin_ref, seg_ref,
                    oa_ref, zb_ref, zc_ref):
    tm = x_ref.shape[0]
    x = x_ref[...]
    xn = (x * _rms_scale(x) * g1_ref[...]).astype(bf16)

    u = _gelu_tanh(_dot(xn, w_ref[:, 0:A_WIDTH]))
    v = _gelu_tanh(_dot(xn, w_ref[:, A_WIDTH:A_IN]))
    mu = jnp.mean(v, axis=-1, keepdims=True)
    vc = v - mu
    vn = vc * lax.rsqrt(jnp.mean(vc * vc, axis=-1, keepdims=True) + EPS)
    vn = vn * lng_ref[...] + lnb_ref[...]
    lane = lax.broadcasted_iota(jnp.int32, (BLOCK, LANES), 1)
    low_half = lane < HEAD_DIM
    for c in range(tm // BLOCK):
        rows = slice(c * BLOCK, (c + 1) * BLOCK)
        for j in range(A_WIDTH // LANES):
            cols = slice(j * LANES, (j + 1) * LANES)
            vp = vn[rows, cols]
            stacked = jnp.concatenate(
                [jnp.where(low_half, vp, 0.0), jnp.where(low_half, 0.0, vp)], axis=0)
            sv = _dot(wsp_ref[j], stacked.astype(bf16)) + bsp_ref[:, cols]
            oa_ref[rows, cols] = (u[rows, cols] * sv).astype(bf16)

    zb = _dot(xn, w_ref[:, A_IN:A_IN + QKV_WIDTH])
    scale = HEAD_DIM ** -0.5
    zb_ref[:, 0:Q_WIDTH] = (zb[:, 0:Q_WIDTH] * scale).astype(bf16)
    zb_ref[:, Q_WIDTH:QKV_WIDTH] = zb[:, Q_WIDTH:QKV_WIDTH].astype(bf16)

    zc = _dot(xn, w_ref[:, A_IN + QKV_WIDTH:IN_WIDTH])
    cos = cos_ref[...]
    sin = sin_ref[...]
    for j in range((Q_WIDTH + KV_WIDTH) // LANES):
        cols = slice(j * LANES, (j + 1) * LANES)
        is_q = j < Q_WIDTH // LANES
        t = zc[:, cols]
        gain = qg_ref[...] if is_q else kg_ref[...]
        tn = t * lax.rsqrt(_head_mean_sq(t, seg_ref) + EPS) * gain
        tr = tn * cos + _swap16(tn) * sin
        if is_q:
            tr = tr * scale
        zc_ref[:, cols] = tr.astype(bf16)
    zc_ref[:, Q_WIDTH + KV_WIDTH:QKV_WIDTH] = zc[:, Q_WIDTH + KV_WIDTH:QKV_WIDTH].astype(bf16)


def _in_proj(x2d, seq, g1, w_in, lng, lnb, wsp, bsp, qg, kg, cos_t, sin_t, seg):
    t = x2d.shape[0]
    tm = TM_IN
    tiles_per_seq = seq // tm
    const = lambda i: (0, 0)
    return pl.pallas_call(
        _in_proj_kernel,
        grid=(t // tm,),
        in_specs=[
            pl.BlockSpec((tm, D_MODEL), lambda i: (i, 0)),
            pl.BlockSpec((1, D_MODEL), const),
            pl.BlockSpec((D_MODEL, IN_WIDTH), const),
            pl.BlockSpec((1, A_WIDTH), const),
            pl.BlockSpec((1, A_WIDTH), const),
            pl.BlockSpec((A_WIDTH // LANES, BLOCK, 2 * BLOCK), lambda i: (0, 0, 0)),
            pl.BlockSpec((BLOCK, A_WIDTH), const),
            pl.BlockSpec((1, LANES), const),
            pl.BlockSpec((1, LANES), const),
            pl.BlockSpec((tm, LANES), lambda i: (i % tiles_per_seq, 0)),
            pl.BlockSpec((tm, LANES), lambda i: (i % tiles_per_seq, 0)),
            pl.BlockSpec((LANES, LANES), const),
        ],
        out_specs=[
            pl.BlockSpec((tm, A_WIDTH), lambda i: (i, 0)),
            pl.BlockSpec((tm, QKV_WIDTH), lambda i: (i, 0)),
            pl.BlockSpec((tm, QKV_WIDTH), lambda i: (i, 0)),
        ],
        out_shape=[
            jax.ShapeDtypeStruct((t, A_WIDTH), bf16),
            jax.ShapeDtypeStruct((t, QKV_WIDTH), bf16),
            jax.ShapeDtypeStruct((t, QKV_WIDTH), bf16),
        ],
        compiler_params=pltpu.CompilerParams(
            dimension_semantics=("parallel",), vmem_limit_bytes=VMEM_LIMIT),
        name="in_proj",
    )(x2d, g1, w_in, lng, lnb, wsp, bsp, qg, kg, cos_t, sin_t, seg)


def _band_attn_kernel(nb, sink_ref, q_ref, kp_ref, kc_ref, kn_ref, vp_ref, vc_ref, vn_ref,
                      bias_ref, o_ref):
    n = pl.program_id(0) % nb
    kb = jnp.concatenate([kp_ref[...], kc_ref[...], kn_ref[...]], axis=0)
    vb = jnp.concatenate([vp_ref[...], vc_ref[...], vn_ref[...]], axis=0)
    qpos = lax.broadcasted_iota(jnp.int32, (BLOCK, 3 * BLOCK), 0)
    jpos = lax.broadcasted_iota(jnp.int32, (BLOCK, 3 * BLOCK), 1)
    rel = jpos - BLOCK - qpos
    win = jnp.abs(rel) <= WINDOW
    kpos = (n - 1) * BLOCK + jpos
    mask = win & (kpos >= 0) & (kpos < nb * BLOCK)
    for h in range(N_HEADS):
        kv = h // GROUP
        qh = q_ref[:, h * HEAD_DIM:(h + 1) * HEAD_DIM]
        kh = kb[:, kv * HEAD_DIM:(kv + 1) * HEAD_DIM]
        vh = vb[:, kv * HEAD_DIM:(kv + 1) * HEAD_DIM]
        logits = _dot_nt(qh, kh) + bias_ref[h]
        logits = jnp.where(mask, logits, -jnp.inf)
        sk = sink_ref[h]
        m = jnp.maximum(jnp.max(logits, axis=-1, keepdims=True), sk)
        p = jnp.exp(logits - m)
        denom = jnp.sum(p, axis=-1, keepdims=True) + jnp.exp(sk - m)
        probs = (p / denom).astype(bf16)
        o_ref[:, h * HEAD_DIM:(h + 1) * HEAD_DIM] = _dot(probs, vh).astype(bf16)


def _band_attn(zb, seq, sink, bias):
    t = zb.shape[0]
    nb = seq // BLOCK
    nblocks = t // BLOCK
    kv_spec = lambda col, off: pl.BlockSpec(
        (BLOCK, KV_WIDTH), lambda i: (jnp.clip(i + off, 0, nblocks - 1), col))
    return pl.pallas_call(
        functools.partial(_band_attn_kernel, nb),
        grid=(nblocks,),
        in_specs=[
            pl.BlockSpec(memory_space=pltpu.SMEM),
            pl.BlockSpec((BLOCK, Q_WIDTH), lambda i: (i, 0)),
            kv_spec(K_COL_BLOCK, -1), kv_spec(K_COL_BLOCK, 0), kv_spec(K_COL_BLOCK, 1),
            kv_spec(V_COL_BLOCK, -1), kv_spec(V_COL_BLOCK, 0), kv_spec(V_COL_BLOCK, 1),
            pl.BlockSpec((N_HEADS, BLOCK, 3 * BLOCK), lambda i: (0, 0, 0)),
        ],
        out_specs=pl.BlockSpec((BLOCK, Q_WIDTH), lambda i: (i, 0)),
        out_shape=jax.ShapeDtypeStruct((t, Q_WIDTH), bf16),
        compiler_params=pltpu.CompilerParams(
            dimension_semantics=("parallel",), vmem_limit_bytes=VMEM_LIMIT),
        name="band_attn",
    )(sink, zb, zb, zb, zb, zb, zb, zb, bias)


def _flash_attn_kernel(q_ref, k_ref, v_ref, o_ref):
    tq = q_ref.shape[0]
    seq = k_ref.shape[0]
    n_chunks = seq // TK_FLASH
    for kv in range(N_KV):
        heads = range(kv * GROUP, (kv + 1) * GROUP)
        qs = jnp.concatenate(
            [q_ref[:, h * HEAD_DIM:(h + 1) * HEAD_DIM] for h in heads], axis=0)
        lanes = slice(kv * HEAD_DIM, (kv + 1) * HEAD_DIM)

        def body(c, carry):
            m, l, acc = carry
            start = pl.multiple_of(c * TK_FLASH, TK_FLASH)
            kc = k_ref[pl.ds(start, TK_FLASH), lanes]
            vc = v_ref[pl.ds(start, TK_FLASH), lanes]
            s = _dot_nt(qs, kc)
            m_new = jnp.maximum(m, jnp.max(s, axis=-1, keepdims=True))
            alpha = jnp.exp(m - m_new)
            p = jnp.exp(s - m_new)
            l_new = alpha * l + jnp.sum(p, axis=-1, keepdims=True)
            acc_new = alpha * acc + _dot(p.astype(bf16), vc)
            return m_new, l_new, acc_new

        init = (jnp.full((GROUP * tq, 1), -jnp.inf, f32),
                jnp.zeros((GROUP * tq, 1), f32),
                jnp.zeros((GROUP * tq, HEAD_DIM), f32))
        _, l, acc = lax.fori_loop(0, n_chunks, body, init)
        out = acc / l
        for gi, h in enumerate(heads):
            o_ref[:, h * HEAD_DIM:(h + 1) * HEAD_DIM] = out[gi * tq:(gi + 1) * tq].astype(bf16)


def _flash_attn(zc, seq):
    t = zc.shape[0]
    tq = TQ_FLASH
    q_tiles = seq // tq
    return pl.pallas_call(
        _flash_attn_kernel,
        grid=(t // seq, q_tiles),
        in_specs=[
            pl.BlockSpec((tq, Q_WIDTH), lambda b, i: (b * q_tiles + i, 0)),
            pl.BlockSpec((seq, KV_WIDTH), lambda b, i: (b, K_COL_BLOCK)),
            pl.BlockSpec((seq, KV_WIDTH), lambda b, i: (b, V_COL_BLOCK)),
        ],
        out_specs=pl.BlockSpec((tq, Q_WIDTH), lambda b, i: (b * q_tiles + i, 0)),
        out_shape=jax.ShapeDtypeStruct((t, Q_WIDTH), bf16),
        compiler_params=pltpu.CompilerParams(
            dimension_semantics=("parallel", "parallel"), vmem_limit_bytes=VMEM_LIMIT),
        name="flash_attn",
    )(zc, zc, zc)


def _merge_kernel(x_ref, g1_ref, oa_ref, ob_ref, oc_ref, wg_ref, bg_ref, wbr_ref, wo_ref,
                  y_ref):
    x = x_ref[...]
    xn = (x * _rms_scale(x) * g1_ref[...]).astype(bf16)
    merged = None
    for n, o_ref in enumerate((oa_ref, ob_ref, oc_ref)):
        cols = slice(n * D_MODEL, (n + 1) * D_MODEL)
        gate = _sigmoid(_dot(xn, wg_ref[:, cols]) + bg_ref[:, cols])
        term = gate * _dot(o_ref[...], wbr_ref[n])
        merged = term if merged is None else merged + term
    y_ref[...] = x + _dot(merged.astype(bf16), wo_ref[...])


def _merge(x2d, g1, oa, ob, oc, wg, bg, wbr, wo):
    t = x2d.shape[0]
    tm = TM_MERGE
    const = lambda i: (0, 0)
    row = lambda width: pl.BlockSpec((tm, width), lambda i: (i, 0))
    return pl.pallas_call(
        _merge_kernel,
        grid=(t // tm,),
        in_specs=[
            row(D_MODEL),
            pl.BlockSpec((1, D_MODEL), const),
            row(BR_WIDTH), row(BR_WIDTH), row(BR_WIDTH),
            pl.BlockSpec((D_MODEL, N_BRANCH * D_MODEL), const),
            pl.BlockSpec((1, N_BRANCH * D_MODEL), const),
            pl.BlockSpec((N_BRANCH, BR_WIDTH, D_MODEL), lambda i: (0, 0, 0)),
            pl.BlockSpec((D_MODEL, D_MODEL), const),
        ],
        out_specs=row(D_MODEL),
        out_shape=jax.ShapeDtypeStruct((t, D_MODEL), f32),
        compiler_params=pltpu.CompilerParams(
            dimension_semantics=("parallel",), vmem_limit_bytes=VMEM_LIMIT),
        name="merge",
    )(x2d, g1, oa, ob, oc, wg, bg, wbr, wo)


def _conv_ffn_kernel(tiles_per_seq, apply_final, xp_ref, x_ref, xq_ref, g2_ref, wup_ref,
                     cw_ref, cb_ref, wdn_ref, gf_ref, y_ref):
    tm = x_ref.shape[0]
    i = pl.program_id(0) % tiles_per_seq
    x = x_ref[...]
    halo_p = jnp.where(i > 0, xp_ref[...], 0.0)
    halo_n = jnp.where(i < tiles_per_seq - 1, xq_ref[...], 0.0)
    xe = jnp.concatenate([halo_p, x, halo_n], axis=0)
    xn = (xe * _rms_scale(xe) * g2_ref[...]).astype(bf16)
    rows = tm + 2 * SUBLANES
    acc = None
    for c in range(D_FF // FF_CHUNK):
        gcols = slice(c * FF_CHUNK, (c + 1) * FF_CHUNK)
        vcols = slice(D_FF + c * FF_CHUNK, D_FF + (c + 1) * FF_CHUNK)
        parts = []
        for cols in (gcols, vcols):
            h = _dot(xn, wup_ref[:, cols])
            hc = (pltpu.roll(h, 1, 0) * cw_ref[0:1, cols]
                  + h * cw_ref[1:2, cols]
                  + pltpu.roll(h, rows - 1, 0) * cw_ref[2:3, cols]
                  + cb_ref[:, cols])
            parts.append(hc[SUBLANES:SUBLANES + tm])
        act = (parts[0] * _sigmoid(parts[0]) * parts[1]).astype(bf16)
        term = _dot(act, wdn_ref[gcols, :])
        acc = term if acc is None else acc + term
    y = x + acc
    if apply_final:
        y = y * _rms_scale(y) * gf_ref[...]
    y_ref[...] = y


def _conv_ffn(x2d, seq, g2, wup, cw, cb, wdn, gf, apply_final):
    t = x2d.shape[0]
    tm = TM_FFN
    tiles_per_seq = seq // tm
    halo_per_tile = tm // SUBLANES
    n_halo = t // SUBLANES
    const = lambda i: (0, 0)
    return pl.pallas_call(
        functools.partial(_conv_ffn_kernel, tiles_per_seq, apply_final),
        grid=(t // tm,),
        in_specs=[
            pl.BlockSpec((SUBLANES, D_MODEL),
                         lambda i: (jnp.maximum(i * halo_per_tile - 1, 0), 0)),
            pl.BlockSpec((tm, D_MODEL), lambda i: (i, 0)),
            pl.BlockSpec((SUBLANES, D_MODEL),
                         lambda i: (jnp.minimum((i + 1) * halo_per_tile, n_halo - 1), 0)),
            pl.BlockSpec((1, D_MODEL), const),
            pl.BlockSpec((D_MODEL, 2 * D_FF), const),
            pl.BlockSpec((3, 2 * D_FF), const),
            pl.BlockSpec((1, 2 * D_FF), const),
            pl.BlockSpec((D_FF, D_MODEL), const),
            pl.BlockSpec((1, D_MODEL), const),
        ],
        out_specs=pl.BlockSpec((tm, D_MODEL), lambda i: (i, 0)),
        out_shape=jax.ShapeDtypeStruct((t, D_MODEL), f32),
        compiler_params=pltpu.CompilerParams(
            dimension_semantics=("parallel",), vmem_limit_bytes=VMEM_LIMIT),
        name="conv_ffn",
    )(x2d, x2d, x2d, g2, wup, cw, cb, wdn, gf)


def _t5_bucket(rel):
    half = N_BUCKETS // 2
    max_exact = half // 2
    ret = jnp.where(rel > 0, half, 0)
    n = jnp.abs(rel)
    nf = jnp.maximum(n, 1).astype(f32)
    large = max_exact + (jnp.log(nf / max_exact) / math.log(MAX_DIST / max_exact)
                         * (half - max_exact)).astype(jnp.int32)
    large = jnp.minimum(large, half - 1)
    return ret + jnp.where(n < max_exact, n, large)


def _band_bias(rel_bias):
    qpos = jnp.arange(BLOCK)[:, None]
    jpos = jnp.arange(3 * BLOCK)[None, :]
    rel = jpos - BLOCK - qpos
    return rel_bias[_t5_bucket(rel)].transpose(2, 0, 1).astype(f32)


def _rope_tables(seq):
    m = HEAD_DIM // 4
    pos = jnp.arange(seq)
    row = (pos // GRID_W).astype(f32)
    col = (pos % GRID_W).astype(f32)
    inv = ROPE_THETA ** (-jnp.arange(m, dtype=f32) / m)
    ang_r = row[:, None] * inv[None, :]
    ang_c = col[:, None] * inv[None, :]
    cos = jnp.concatenate([jnp.cos(ang_r), jnp.cos(ang_r), jnp.cos(ang_c), jnp.cos(ang_c)], axis=-1)
    sin = jnp.concatenate([-jnp.sin(ang_r), jnp.sin(ang_r), -jnp.sin(ang_c), jnp.sin(ang_c)], axis=-1)
    reps = LANES // HEAD_DIM
    return jnp.tile(cos, (1, reps)), jnp.tile(sin, (1, reps))


def _trunk(x, layers, bias, seg, final_g):
    bsz, seq, d = x.shape
    x2d = x.reshape(bsz * seq, d)
    cos_t, sin_t = _rope_tables(seq)
    for l, p in enumerate(layers):
        oa, zb, zc = _in_proj(x2d, seq, p["g1"], p["w_in"], p["lng"], p["lnb"], p["wsp"],
                              p["bsp"], p["qg"], p["kg"], cos_t, sin_t, seg)
        ob = _band_attn(zb, seq, p["sink"], bias)
        oc = _flash_attn(zc, seq)
        x2d = _merge(x2d, p["g1"], oa, ob, oc, p["wg"], p["bg"], p["wbr"], p["wo"])
        x2d = _conv_ffn(x2d, seq, p["g2"], p["wup"], p["cw"], p["cb"], p["wdn"], final_g,
                        apply_final=(l == len(layers) - 1))
    return x2d.reshape(bsz, seq, d)


def kernel(x_prompt, x_sample, rel_bias, norm1_g, w_in, ln_v_g, ln_v_b, w_spatial, b_spatial,
           sink, q_norm_g, k_norm_g, w_gate, b_gate, w_branch, w_out, norm2_g, w_up, conv_w,
           conv_b, w_down, final_g):
    bias = _band_bias(rel_bias)
    head_of_lane = jnp.arange(LANES) // HEAD_DIM
    seg = ((head_of_lane[:, None] == head_of_lane[None, :]).astype(f32) / HEAD_DIM).astype(bf16)
    reps = LANES // HEAD_DIM
    layers = []
    for l in range(DEPTH):
        ws = w_spatial[l].astype(bf16)
        wsp = jnp.concatenate([ws[0::2], ws[1::2]], axis=-1)
        layers.append(dict(
            g1=norm1_g[l][None, :],
            w_in=w_in[l].astype(bf16),
            lng=ln_v_g[l][None, :],
            lnb=ln_v_b[l][None, :],
            wsp=wsp,
            bsp=jnp.repeat(b_spatial[l].T, HEAD_DIM, axis=1),
            qg=jnp.tile(q_norm_g[l], reps)[None, :],
            kg=jnp.tile(k_norm_g[l], reps)[None, :],
            sink=sink[l],
            wg=w_gate[l].astype(bf16),
            bg=b_gate[l][None, :],
            wbr=w_branch[l].astype(bf16),
            wo=w_out[l].astype(bf16),
            g2=norm2_g[l][None, :],
            wup=w_up[l].astype(bf16),
            cw=conv_w[l],
            cb=conv_b[l][None, :],
            wdn=w_down[l].astype(bf16),
        ))
    gf = final_g[None, :]
    y_prompt = _trunk(x_prompt, layers, bias, seg, gf)
    y_sample = _trunk(x_sample, layers, bias, seg, gf)
    return (y_prompt, y_sample)
```

```python
import functools
import math

import jax
import jax.numpy as jnp
from jax import lax
from jax.experimental import pallas as pl
from jax.experimental.pallas import tpu as pltpu

D_MODEL = 1024
DEPTH = 2
HEAD_DIM = 64
BLOCK = 128
A_GROUPS = 8
A_WIDTH = A_GROUPS * HEAD_DIM
N_HEADS = 8
N_KV = 2
GROUP = N_HEADS // N_KV
WINDOW = 128
ROPE_THETA = 10000.0
GRID_W = 64
N_BUCKETS = 32
MAX_DIST = 128
D_FF = 2816
EPS = 1e-6
N_BRANCH = 3
BR_WIDTH = 512
Q_WIDTH = N_HEADS * HEAD_DIM
KV_WIDTH = N_KV * HEAD_DIM
QKV_WIDTH = Q_WIDTH + 2 * KV_WIDTH
A_IN = 2 * A_WIDTH
IN_WIDTH = A_IN + 2 * QKV_WIDTH
K_COL_BLOCK = Q_WIDTH // KV_WIDTH
V_COL_BLOCK = K_COL_BLOCK + 1

LANES = 128
SUBLANES = 8
VMEM_LIMIT = 56 * 1024 * 1024

TM_IN = 256
TM_MERGE = 256
TM_FFN = 512
FF_CHUNK = 256
TQ_FLASH = 128

f32 = jnp.float32
bf16 = jnp.bfloat16


def _rms_scale(x):
    return lax.rsqrt(jnp.mean(x * x, axis=-1, keepdims=True) + EPS)


def _gelu_tanh(x):
    c = math.sqrt(2.0 / math.pi)
    return x * (0.5 * (1.0 + jnp.tanh(c * (x + 0.044715 * (x * x * x)))))


def _sigmoid(x):
    return 1.0 / (1.0 + jnp.exp(-x))


def _dot(a, b):
    return jnp.dot(a, b, preferred_element_type=f32)


def _dot_tn(a, b):
    return lax.dot_general(a, b, (((0,), (0,)), ((), ())), preferred_element_type=f32)


def _dot_nt(a, b):
    return lax.dot_general(a, b, (((1,), (1,)), ((), ())), preferred_element_type=f32)


def _head_mean_sq(x, seg_ref):
    sq = x * x
    hi = sq.astype(bf16)
    lo = (sq - hi.astype(f32)).astype(bf16)
    seg = seg_ref[...]
    return _dot(hi, seg) + _dot(lo, seg)


def _swap16(x):
    lane = lax.broadcasted_iota(jnp.int32, x.shape, 1)
    first_half = (lane % 32) < 16
    return jnp.where(first_half, pltpu.roll(x, LANES - 16, 1), pltpu.roll(x, 16, 1))


def _swap16_rows(x):
    h = HEAD_DIM // 4
    return jnp.concatenate([x[h:2 * h], x[0:h], x[3 * h:4 * h], x[2 * h:3 * h]], axis=0)


def _in_proj_kernel(x_ref, g1_ref, w_ref, wqt_ref, wvt_ref, lng_ref, lnb_ref, wsp_ref, bsp_ref,
                    qgt_ref, kg_ref, cos_ref, sin_ref, cost_ref, sint_ref, seg_ref,
                    oa_ref, zb_ref, kc_ref, qct_ref, vct_ref):
    tm = x_ref.shape[0]
    x = x_ref[...]
    xn = (x * _rms_scale(x) * g1_ref[...]).astype(bf16)

    u = _gelu_tanh(_dot(xn, w_ref[:, 0:A_WIDTH]))
    v = _gelu_tanh(_dot(xn, w_ref[:, A_WIDTH:A_IN]))
    mu = jnp.mean(v, axis=-1, keepdims=True)
    vc = v - mu
    vn = vc * lax.rsqrt(jnp.mean(vc * vc, axis=-1, keepdims=True) + EPS)
    vn = vn * lng_ref[...] + lnb_ref[...]
    lane = lax.broadcasted_iota(jnp.int32, (BLOCK, LANES), 1)
    low_half = lane < HEAD_DIM
    for c in range(tm // BLOCK):
        rows = slice(c * BLOCK, (c + 1) * BLOCK)
        for j in range(A_WIDTH // LANES):
            cols = slice(j * LANES, (j + 1) * LANES)
            vp = vn[rows, cols]
            stacked = jnp.concatenate(
                [jnp.where(low_half, vp, 0.0), jnp.where(low_half, 0.0, vp)], axis=0)
            sv = _dot(wsp_ref[j], stacked.astype(bf16)) + bsp_ref[:, cols]
            oa_ref[rows, cols] = (u[rows, cols] * sv).astype(bf16)

    zb = _dot(xn, w_ref[:, A_IN:A_IN + QKV_WIDTH])
    scale = HEAD_DIM ** -0.5
    zb_ref[:, 0:Q_WIDTH] = (zb[:, 0:Q_WIDTH] * scale).astype(bf16)
    zb_ref[:, Q_WIDTH:QKV_WIDTH] = zb[:, Q_WIDTH:QKV_WIDTH].astype(bf16)

    t = _dot(xn, w_ref[:, A_IN + QKV_WIDTH:A_IN + QKV_WIDTH + KV_WIDTH])
    tn = t * lax.rsqrt(_head_mean_sq(t, seg_ref) + EPS) * kg_ref[...]
    kc_ref[...] = (tn * cos_ref[...] + _swap16(tn) * sin_ref[...]).astype(bf16)

    qt = _dot_nt(wqt_ref[...], xn)
    cost = cost_ref[...]
    sint = sint_ref[...]
    for h in range(N_HEADS):
        rows = slice(h * HEAD_DIM, (h + 1) * HEAD_DIM)
        th = qt[rows]
        r = lax.rsqrt(jnp.mean(th * th, axis=0, keepdims=True) + EPS)
        tn = th * r * qgt_ref[...]
        qct_ref[rows, :] = ((tn * cost + _swap16_rows(tn) * sint) * scale).astype(bf16)
    vct_ref[0] = _dot_nt(wvt_ref[...], xn).astype(bf16)


def _in_proj(x2d, seq, g1, w_in, wqt, wvt, lng, lnb, wsp, bsp, qgt, kg, cos_t, sin_t,
             cos_tt, sin_tt, seg):
    t = x2d.shape[0]
    tm = TM_IN
    tiles_per_seq = seq // tm
    const = lambda i: (0, 0)
    return pl.pallas_call(
        _in_proj_kernel,
        grid=(t // tm,),
        in_specs=[
            pl.BlockSpec((tm, D_MODEL), lambda i: (i, 0)),
            pl.BlockSpec((1, D_MODEL), const),
            pl.BlockSpec((D_MODEL, w_in.shape[1]), const),
            pl.BlockSpec((Q_WIDTH, D_MODEL), const),
            pl.BlockSpec((KV_WIDTH, D_MODEL), const),
            pl.BlockSpec((1, A_WIDTH), const),
            pl.BlockSpec((1, A_WIDTH), const),
            pl.BlockSpec((A_WIDTH // LANES, BLOCK, 2 * BLOCK), lambda i: (0, 0, 0)),
            pl.BlockSpec((BLOCK, A_WIDTH), const),
            pl.BlockSpec((HEAD_DIM, tm), const),
            pl.BlockSpec((1, LANES), const),
            pl.BlockSpec((tm, LANES), lambda i: (i % tiles_per_seq, 0)),
            pl.BlockSpec((tm, LANES), lambda i: (i % tiles_per_seq, 0)),
            pl.BlockSpec((HEAD_DIM, tm), lambda i: (0, i % tiles_per_seq)),
            pl.BlockSpec((HEAD_DIM, tm), lambda i: (0, i % tiles_per_seq)),
            pl.BlockSpec((LANES, LANES), const),
        ],
        out_specs=[
            pl.BlockSpec((tm, A_WIDTH), lambda i: (i, 0)),
            pl.BlockSpec((tm, QKV_WIDTH), lambda i: (i, 0)),
            pl.BlockSpec((tm, KV_WIDTH), lambda i: (i, 0)),
            pl.BlockSpec((Q_WIDTH, tm), lambda i: (0, i)),
            pl.BlockSpec((1, KV_WIDTH, tm), lambda i: (i, 0, 0)),
        ],
        out_shape=[
            jax.ShapeDtypeStruct((t, A_WIDTH), bf16),
            jax.ShapeDtypeStruct((t, QKV_WIDTH), bf16),
            jax.ShapeDtypeStruct((t, KV_WIDTH), bf16),
            jax.ShapeDtypeStruct((Q_WIDTH, t), bf16),
            jax.ShapeDtypeStruct((t // tm, KV_WIDTH, tm), bf16),
        ],
        compiler_params=pltpu.CompilerParams(
            dimension_semantics=("parallel",), vmem_limit_bytes=VMEM_LIMIT),
        name="in_proj",
    )(x2d, g1, w_in, wqt, wvt, lng, lnb, wsp, bsp, qgt, kg, cos_t, sin_t, cos_tt, sin_tt, seg)


def _band_attn_kernel(nb, sink_ref, q_ref, kp_ref, kc_ref, kn_ref, vp_ref, vc_ref, vn_ref,
                      bias_ref, o_ref):
    n = pl.program_id(0) % nb
    kb = jnp.concatenate([kp_ref[...], kc_ref[...], kn_ref[...]], axis=0)
    vb = jnp.concatenate([vp_ref[...], vc_ref[...], vn_ref[...]], axis=0)
    qpos = lax.broadcasted_iota(jnp.int32, (BLOCK, 3 * BLOCK), 0)
    jpos = lax.broadcasted_iota(jnp.int32, (BLOCK, 3 * BLOCK), 1)
    rel = jpos - BLOCK - qpos
    win = jnp.abs(rel) <= WINDOW
    kpos = (n - 1) * BLOCK + jpos
    mask = win & (kpos >= 0) & (kpos < nb * BLOCK)
    for h in range(N_HEADS):
        kv = h // GROUP
        qh = q_ref[:, h * HEAD_DIM:(h + 1) * HEAD_DIM]
        kh = kb[:, kv * HEAD_DIM:(kv + 1) * HEAD_DIM]
        vh = vb[:, kv * HEAD_DIM:(kv + 1) * HEAD_DIM]
        logits = _dot_nt(qh, kh) + bias_ref[h]
        logits = jnp.where(mask, logits, -jnp.inf)
        sk = sink_ref[h]
        m = jnp.maximum(jnp.max(logits, axis=-1, keepdims=True), sk)
        p = jnp.exp(logits - m)
        denom = jnp.sum(p, axis=-1, keepdims=True) + jnp.exp(sk - m)
        probs = (p / denom).astype(bf16)
        o_ref[:, h * HEAD_DIM:(h + 1) * HEAD_DIM] = _dot(probs, vh).astype(bf16)


def _band_attn(zb, seq, sink, bias):
    t = zb.shape[0]
    nb = seq // BLOCK
    nblocks = t // BLOCK
    kv_spec = lambda col, off: pl.BlockSpec(
        (BLOCK, KV_WIDTH), lambda i: (jnp.clip(i + off, 0, nblocks - 1), col))
    return pl.pallas_call(
        functools.partial(_band_attn_kernel, nb),
        grid=(nblocks,),
        in_specs=[
            pl.BlockSpec(memory_space=pltpu.SMEM),
            pl.BlockSpec((BLOCK, Q_WIDTH), lambda i: (i, 0)),
            kv_spec(K_COL_BLOCK, -1), kv_spec(K_COL_BLOCK, 0), kv_spec(K_COL_BLOCK, 1),
            kv_spec(V_COL_BLOCK, -1), kv_spec(V_COL_BLOCK, 0), kv_spec(V_COL_BLOCK, 1),
            pl.BlockSpec((N_HEADS, BLOCK, 3 * BLOCK), lambda i: (0, 0, 0)),
        ],
        out_specs=pl.BlockSpec((BLOCK, Q_WIDTH), lambda i: (i, 0)),
        out_shape=jax.ShapeDtypeStruct((t, Q_WIDTH), bf16),
        compiler_params=pltpu.CompilerParams(
            dimension_semantics=("parallel",), vmem_limit_bytes=VMEM_LIMIT),
        name="band_attn",
    )(sink, zb, zb, zb, zb, zb, zb, zb, bias)


def _flash_attn_kernel(qt_ref, k_ref, vt_ref, ot_ref):
    tq = qt_ref.shape[1]
    nq = GROUP * tq
    n_chunks = vt_ref.shape[0]
    tk = vt_ref.shape[2]
    zeros = jnp.zeros((HEAD_DIM, nq), bf16)
    q_ext = []
    for kv in range(N_KV):
        qg = jnp.concatenate(
            [qt_ref[h * HEAD_DIM:(h + 1) * HEAD_DIM, :]
             for h in range(kv * GROUP, (kv + 1) * GROUP)], axis=1)
        q_ext.append(jnp.concatenate([qg, zeros] if kv == 0 else [zeros, qg], axis=0))

    def logits(c):
        start = pl.multiple_of(c * tk, tk)
        kc = k_ref[pl.ds(start, tk), :]
        return tuple(_dot(kc, q_ext[kv]) for kv in range(N_KV))

    def body(c, carry):
        s_all, stats = carry
        s_next = logits(jnp.minimum(c + 1, n_chunks - 1))
        vt = vt_ref[c]
        out = []
        for kv in range(N_KV):
            m, l, acc = stats[kv]
            s = s_all[kv]
            m_new = jnp.maximum(m, jnp.max(s, axis=0, keepdims=True))
            alpha = jnp.exp(m - m_new)
            p = jnp.exp(s - m_new)
            l_new = alpha * l + jnp.sum(p, axis=0, keepdims=True)
            pv = _dot(vt[kv * HEAD_DIM:(kv + 1) * HEAD_DIM], p.astype(bf16))
            out.append((m_new, l_new, alpha * acc + pv))
        return s_next, tuple(out)

    init = tuple((jnp.full((1, nq), -jnp.inf, f32), jnp.zeros((1, nq), f32),
                  jnp.zeros((HEAD_DIM, nq), f32)) for _ in range(N_KV))
    _, final = lax.fori_loop(0, n_chunks, body, (logits(0), init))
    for kv in range(N_KV):
        _, l, acc = final[kv]
        out = acc / l
        for gi in range(GROUP):
            h = kv * GROUP + gi
            ot_ref[h * HEAD_DIM:(h + 1) * HEAD_DIM, :] = out[:, gi * tq:(gi + 1) * tq].astype(bf16)


def _flash_attn(qct, kc, vct, seq):
    t = kc.shape[0]
    tq = TQ_FLASH
    q_tiles = seq // tq
    chunks_per_seq = seq // vct.shape[2]
    return pl.pallas_call(
        _flash_attn_kernel,
        grid=(t // seq, q_tiles),
        in_specs=[
            pl.BlockSpec((Q_WIDTH, tq), lambda b, i: (0, b * q_tiles + i)),
            pl.BlockSpec((seq, KV_WIDTH), lambda b, i: (b, 0)),
            pl.BlockSpec((chunks_per_seq, KV_WIDTH, vct.shape[2]), lambda b, i: (b, 0, 0)),
        ],
        out_specs=pl.BlockSpec((Q_WIDTH, tq), lambda b, i: (0, b * q_tiles + i)),
        out_shape=jax.ShapeDtypeStruct((Q_WIDTH, t), bf16),
        compiler_params=pltpu.CompilerParams(
            dimension_semantics=("parallel", "parallel"), vmem_limit_bytes=VMEM_LIMIT),
        name="flash_attn",
    )(qct, kc, vct)


def _merge_kernel(x_ref, g1_ref, oa_ref, ob_ref, oct_ref, wg_ref, bg_ref, wbr_ref, wo_ref,
                  y_ref):
    x = x_ref[...]
    xn = (x * _rms_scale(x) * g1_ref[...]).astype(bf16)
    projs = (_dot(oa_ref[...], wbr_ref[0]), _dot(ob_ref[...], wbr_ref[1]),
             _dot_tn(oct_ref[...], wbr_ref[2]))
    merged = None
    for n, proj in enumerate(projs):
        cols = slice(n * D_MODEL, (n + 1) * D_MODEL)
        gate = _sigmoid(_dot(xn, wg_ref[:, cols]) + bg_ref[:, cols])
        term = gate * proj
        merged = term if merged is None else merged + term
    y_ref[...] = x + _dot(merged.astype(bf16), wo_ref[...])


def _merge(x2d, g1, oa, ob, oct, wg, bg, wbr, wo):
    t = x2d.shape[0]
    tm = TM_MERGE
    const = lambda i: (0, 0)
    row = lambda width: pl.BlockSpec((tm, width), lambda i: (i, 0))
    return pl.pallas_call(
        _merge_kernel,
        grid=(t // tm,),
        in_specs=[
            row(D_MODEL),
            pl.BlockSpec((1, D_MODEL), const),
            row(BR_WIDTH), row(BR_WIDTH),
            pl.BlockSpec((BR_WIDTH, tm), lambda i: (0, i)),
            pl.BlockSpec((D_MODEL, N_BRANCH * D_MODEL), const),
            pl.BlockSpec((1, N_BRANCH * D_MODEL), const),
            pl.BlockSpec((N_BRANCH, BR_WIDTH, D_MODEL), lambda i: (0, 0, 0)),
            pl.BlockSpec((D_MODEL, D_MODEL), const),
        ],
        out_specs=row(D_MODEL),
        out_shape=jax.ShapeDtypeStruct((t, D_MODEL), f32),
        compiler_params=pltpu.CompilerParams(
            dimension_semantics=("parallel",), vmem_limit_bytes=VMEM_LIMIT),
        name="merge",
    )(x2d, g1, oa, ob, oct, wg, bg, wbr, wo)


def _conv_ffn_kernel(tiles_per_seq, apply_final, xp_ref, x_ref, xq_ref, g2_ref, wup_ref,
                     cw_ref, cb_ref, wdn_ref, gf_ref, y_ref):
    tm = x_ref.shape[0]
    i = pl.program_id(0) % tiles_per_seq
    x = x_ref[...]
    halo_p = jnp.where(i > 0, xp_ref[...], 0.0)
    halo_n = jnp.where(i < tiles_per_seq - 1, xq_ref[...], 0.0)
    xe = jnp.concatenate([halo_p, x, halo_n], axis=0)
    xn = (xe * _rms_scale(xe) * g2_ref[...]).astype(bf16)
    rows = tm + 2 * SUBLANES
    acc = None
    for c in range(D_FF // FF_CHUNK):
        gcols = slice(c * FF_CHUNK, (c + 1) * FF_CHUNK)
        vcols = slice(D_FF + c * FF_CHUNK, D_FF + (c + 1) * FF_CHUNK)
        parts = []
        for cols in (gcols, vcols):
            h = _dot(xn, wup_ref[:, cols])
            hc = (pltpu.roll(h, 1, 0) * cw_ref[0:1, cols]
                  + h * cw_ref[1:2, cols]
                  + pltpu.roll(h, rows - 1, 0) * cw_ref[2:3, cols]
                  + cb_ref[:, cols])
            parts.append(hc[SUBLANES:SUBLANES + tm])
        act = (parts[0] * _sigmoid(parts[0]) * parts[1]).astype(bf16)
        term = _dot(act, wdn_ref[gcols, :])
        acc = term if acc is None else acc + term
    y = x + acc
    if apply_final:
        y = y * _rms_scale(y) * gf_ref[...]
    y_ref[...] = y


def _conv_ffn(x2d, seq, g2, wup, cw, cb, wdn, gf, apply_final):
    t = x2d.shape[0]
    tm = TM_FFN
    tiles_per_seq = seq // tm
    halo_per_tile = tm // SUBLANES
    n_halo = t // SUBLANES
    const = lambda i: (0, 0)
    return pl.pallas_call(
        functools.partial(_conv_ffn_kernel, tiles_per_seq, apply_final),
        grid=(t // tm,),
        in_specs=[
            pl.BlockSpec((SUBLANES, D_MODEL),
                         lambda i: (jnp.maximum(i * halo_per_tile - 1, 0), 0)),
            pl.BlockSpec((tm, D_MODEL), lambda i: (i, 0)),
            pl.BlockSpec((SUBLANES, D_MODEL),
                         lambda i: (jnp.minimum((i + 1) * halo_per_tile, n_halo - 1), 0)),
            pl.BlockSpec((1, D_MODEL), const),
            pl.BlockSpec((D_MODEL, 2 * D_FF), const),
            pl.BlockSpec((3, 2 * D_FF), const),
            pl.BlockSpec((1, 2 * D_FF), const),
            pl.BlockSpec((D_FF, D_MODEL), const),
            pl.BlockSpec((1, D_MODEL), const),
        ],
        out_specs=pl.BlockSpec((tm, D_MODEL), lambda i: (i, 0)),
        out_shape=jax.ShapeDtypeStruct((t, D_MODEL), f32),
        compiler_params=pltpu.CompilerParams(
            dimension_semantics=("parallel",), vmem_limit_bytes=VMEM_LIMIT),
        name="conv_ffn",
    )(x2d, x2d, x2d, g2, wup, cw, cb, wdn, gf)


def _t5_bucket(rel):
    half = N_BUCKETS // 2
    max_exact = half // 2
    ret = jnp.where(rel > 0, half, 0)
    n = jnp.abs(rel)
    nf = jnp.maximum(n, 1).astype(f32)
    large = max_exact + (jnp.log(nf / max_exact) / math.log(MAX_DIST / max_exact)
                         * (half - max_exact)).astype(jnp.int32)
    large = jnp.minimum(large, half - 1)
    return ret + jnp.where(n < max_exact, n, large)


def _band_bias(rel_bias):
    qpos = jnp.arange(BLOCK)[:, None]
    jpos = jnp.arange(3 * BLOCK)[None, :]
    rel = jpos - BLOCK - qpos
    return rel_bias[_t5_bucket(rel)].transpose(2, 0, 1).astype(f32)


def _rope_tables(seq):
    m = HEAD_DIM // 4
    pos = jnp.arange(seq)
    row = (pos // GRID_W).astype(f32)
    col = (pos % GRID_W).astype(f32)
    inv = ROPE_THETA ** (-jnp.arange(m, dtype=f32) / m)
    ang_r = row[:, None] * inv[None, :]
    ang_c = col[:, None] * inv[None, :]
    cos = jnp.concatenate([jnp.cos(ang_r), jnp.cos(ang_r), jnp.cos(ang_c), jnp.cos(ang_c)], axis=-1)
    sin = jnp.concatenate([-jnp.sin(ang_r), jnp.sin(ang_r), -jnp.sin(ang_c), jnp.sin(ang_c)], axis=-1)
    reps = LANES // HEAD_DIM
    return jnp.tile(cos, (1, reps)), jnp.tile(sin, (1, reps)), cos.T, sin.T


def _trunk(x, layers, bias, seg, final_g):
    bsz, seq, d = x.shape
    x2d = x.reshape(bsz * seq, d)
    cos_t, sin_t, cos_tt, sin_tt = _rope_tables(seq)
    for l, p in enumerate(layers):
        oa, zb, kc, qct, vct = _in_proj(
            x2d, seq, p["g1"], p["w_in"], p["wqt"], p["wvt"], p["lng"], p["lnb"], p["wsp"],
            p["bsp"], p["qgt"], p["kg"], cos_t, sin_t, cos_tt, sin_tt, seg)
        ob = _band_attn(zb, seq, p["sink"], bias)
        oct = _flash_attn(qct, kc, vct, seq)
        x2d = _merge(x2d, p["g1"], oa, ob, oct, p["wg"], p["bg"], p["wbr"], p["wo"])
        x2d = _conv_ffn(x2d, seq, p["g2"], p["wup"], p["cw"], p["cb"], p["wdn"], final_g,
                        apply_final=(l == len(layers) - 1))
    return x2d.reshape(bsz, seq, d)


def kernel(x_prompt, x_sample, rel_bias, norm1_g, w_in, ln_v_g, ln_v_b, w_spatial, b_spatial,
           sink, q_norm_g, k_norm_g, w_gate, b_gate, w_branch, w_out, norm2_g, w_up, conv_w,
           conv_b, w_down, final_g):
    bias = _band_bias(rel_bias)
    head_of_lane = jnp.arange(LANES) // HEAD_DIM
    seg = ((head_of_lane[:, None] == head_of_lane[None, :]).astype(f32) / HEAD_DIM).astype(bf16)
    reps = LANES // HEAD_DIM
    layers = []
    for l in range(DEPTH):
        ws = w_spatial[l].astype(bf16)
        wsp = jnp.concatenate([ws[0::2], ws[1::2]], axis=-1)
        wl = w_in[l].astype(bf16)
        c_off = A_IN + QKV_WIDTH
        layers.append(dict(
            g1=norm1_g[l][None, :],
            w_in=jnp.concatenate(
                [wl[:, :c_off], wl[:, c_off + Q_WIDTH:c_off + Q_WIDTH + KV_WIDTH]], axis=1),
            wqt=wl[:, c_off:c_off + Q_WIDTH].T,
            wvt=wl[:, c_off + Q_WIDTH + KV_WIDTH:].T,
            lng=ln_v_g[l][None, :],
            lnb=ln_v_b[l][None, :],
            wsp=wsp,
            bsp=jnp.repeat(b_spatial[l].T, HEAD_DIM, axis=1),
            qgt=jnp.broadcast_to(q_norm_g[l][:, None], (HEAD_DIM, TM_IN)),
            kg=jnp.tile(k_norm_g[l], reps)[None, :],
            sink=sink[l],
            wg=w_gate[l].astype(bf16),
            bg=b_gate[l][None, :],
            wbr=w_branch[l].astype(bf16),
            wo=w_out[l].astype(bf16),
            g2=norm2_g[l][None, :],
            wup=w_up[l].astype(bf16),
            cw=conv_w[l],
            cb=conv_b[l][None, :],
            wdn=w_down[l].astype(bf16),
        ))
    gf = final_g[None, :]
    y_prompt = _trunk(x_prompt, layers, bias, seg, gf)
    y_sample = _trunk(x_sample, layers, bias, seg, gf)
    return (y_prompt, y_sample)
```

```python
import functools
import math

import jax
import jax.numpy as jnp
from jax import lax
from jax.experimental import pallas as pl
from jax.experimental.pallas import tpu as pltpu

D_MODEL = 1024
DEPTH = 2
HEAD_DIM = 64
BLOCK = 128
A_GROUPS = 8
A_WIDTH = A_GROUPS * HEAD_DIM
N_HEADS = 8
N_KV = 2
GROUP = N_HEADS // N_KV
WINDOW = 128
ROPE_THETA = 10000.0
GRID_W = 64
N_BUCKETS = 32
MAX_DIST = 128
D_FF = 2816
EPS = 1e-6
N_BRANCH = 3
BR_WIDTH = 512
Q_WIDTH = N_HEADS * HEAD_DIM
KV_WIDTH = N_KV * HEAD_DIM
QKV_WIDTH = Q_WIDTH + 2 * KV_WIDTH
A_IN = 2 * A_WIDTH
IN_WIDTH = A_IN + 2 * QKV_WIDTH
K_COL_BLOCK = Q_WIDTH // KV_WIDTH
V_COL_BLOCK = K_COL_BLOCK + 1

LANES = 128
SUBLANES = 8
VMEM_LIMIT = 56 * 1024 * 1024

TM_IN = 256
TM_MERGE = 256
TM_FFN = 512
FF_CHUNK = 256
TQ_FLASH = 128
FLASH_LOOKAHEAD = 2
LOG2E = math.log2(math.e)
MAX_LOGIT_BOUND = 30.0

f32 = jnp.float32
bf16 = jnp.bfloat16


def _rms_scale(x):
    return lax.rsqrt(jnp.mean(x * x, axis=-1, keepdims=True) + EPS)


def _gelu_tanh(x):
    c = math.sqrt(2.0 / math.pi)
    return x * (0.5 * (1.0 + jnp.tanh(c * (x + 0.044715 * (x * x * x)))))


def _sigmoid(x):
    return 1.0 / (1.0 + jnp.exp(-x))


def _dot(a, b):
    return jnp.dot(a, b, preferred_element_type=f32)


def _dot_tn(a, b):
    return lax.dot_general(a, b, (((0,), (0,)), ((), ())), preferred_element_type=f32)


def _dot_nt(a, b):
    return lax.dot_general(a, b, (((1,), (1,)), ((), ())), preferred_element_type=f32)


def _head_mean_sq(x, seg_ref):
    sq = x * x
    hi = sq.astype(bf16)
    lo = (sq - hi.astype(f32)).astype(bf16)
    seg = seg_ref[...]
    return _dot(hi, seg) + _dot(lo, seg)


def _swap16(x):
    lane = lax.broadcasted_iota(jnp.int32, x.shape, 1)
    first_half = (lane % 32) < 16
    return jnp.where(first_half, pltpu.roll(x, LANES - 16, 1), pltpu.roll(x, 16, 1))


def _swap16_rows(x):
    h = HEAD_DIM // 4
    return jnp.concatenate([x[h:2 * h], x[0:h], x[3 * h:4 * h], x[2 * h:3 * h]], axis=0)


def _in_proj_kernel(x_ref, g1_ref, w_ref, wqt_ref, wvt_ref, lng_ref, lnb_ref, wsp_ref, bsp_ref,
                    qgt_ref, kg_ref, cos_ref, sin_ref, cost_ref, sint_ref, seg_ref,
                    oa_ref, zb_ref, kc_ref, qct_ref, vct_ref):
    tm = x_ref.shape[0]
    x = x_ref[...]
    xn = (x * _rms_scale(x) * g1_ref[...]).astype(bf16)

    u = _gelu_tanh(_dot(xn, w_ref[:, 0:A_WIDTH]))
    v = _gelu_tanh(_dot(xn, w_ref[:, A_WIDTH:A_IN]))
    mu = jnp.mean(v, axis=-1, keepdims=True)
    vc = v - mu
    vn = vc * lax.rsqrt(jnp.mean(vc * vc, axis=-1, keepdims=True) + EPS)
    vn = vn * lng_ref[...] + lnb_ref[...]
    lane = lax.broadcasted_iota(jnp.int32, (BLOCK, LANES), 1)
    low_half = lane < HEAD_DIM
    for c in range(tm // BLOCK):
        rows = slice(c * BLOCK, (c + 1) * BLOCK)
        for j in range(A_WIDTH // LANES):
            cols = slice(j * LANES, (j + 1) * LANES)
            vp = vn[rows, cols]
            stacked = jnp.concatenate(
                [jnp.where(low_half, vp, 0.0), jnp.where(low_half, 0.0, vp)], axis=0)
            sv = _dot(wsp_ref[j], stacked.astype(bf16)) + bsp_ref[:, cols]
            oa_ref[rows, cols] = (u[rows, cols] * sv).astype(bf16)

    zb = _dot(xn, w_ref[:, A_IN:A_IN + QKV_WIDTH])
    scale = HEAD_DIM ** -0.5
    zb_ref[:, 0:Q_WIDTH] = (zb[:, 0:Q_WIDTH] * scale).astype(bf16)
    zb_ref[:, Q_WIDTH:QKV_WIDTH] = zb[:, Q_WIDTH:QKV_WIDTH].astype(bf16)

    t = _dot(xn, w_ref[:, A_IN + QKV_WIDTH:A_IN + QKV_WIDTH + KV_WIDTH])
    tn = t * lax.rsqrt(_head_mean_sq(t, seg_ref) + EPS) * kg_ref[...]
    kc_ref[:, 0:KV_WIDTH] = (tn * cos_ref[...] + _swap16(tn) * sin_ref[...]).astype(bf16)
    lane = lax.broadcasted_iota(jnp.int32, (tm, KV_WIDTH), 1)
    kc_ref[:, KV_WIDTH:2 * KV_WIDTH] = jnp.where(lane == 0, 1.0, 0.0).astype(bf16)

    qt = _dot_nt(wqt_ref[...], xn)
    cost = cost_ref[...]
    sint = sint_ref[...]
    for h in range(N_HEADS):
        rows = slice(h * HEAD_DIM, (h + 1) * HEAD_DIM)
        th = qt[rows]
        r = lax.rsqrt(jnp.mean(th * th, axis=0, keepdims=True) + EPS)
        tn = th * r * qgt_ref[...]
        qct_ref[rows, :] = ((tn * cost + _swap16_rows(tn) * sint) * (scale * LOG2E)).astype(bf16)
    vct_ref[0] = _dot_nt(wvt_ref[...], xn).astype(bf16)


def _in_proj(x2d, seq, g1, w_in, wqt, wvt, lng, lnb, wsp, bsp, qgt, kg, cos_t, sin_t,
             cos_tt, sin_tt, seg):
    t = x2d.shape[0]
    tm = TM_IN
    tiles_per_seq = seq // tm
    const = lambda i: (0, 0)
    return pl.pallas_call(
        _in_proj_kernel,
        grid=(t // tm,),
        in_specs=[
            pl.BlockSpec((tm, D_MODEL), lambda i: (i, 0)),
            pl.BlockSpec((1, D_MODEL), const),
            pl.BlockSpec((D_MODEL, w_in.shape[1]), const),
            pl.BlockSpec((Q_WIDTH, D_MODEL), const),
            pl.BlockSpec((KV_WIDTH, D_MODEL), const),
            pl.BlockSpec((1, A_WIDTH), const),
            pl.BlockSpec((1, A_WIDTH), const),
            pl.BlockSpec((A_WIDTH // LANES, BLOCK, 2 * BLOCK), lambda i: (0, 0, 0)),
            pl.BlockSpec((BLOCK, A_WIDTH), const),
            pl.BlockSpec((HEAD_DIM, tm), const),
            pl.BlockSpec((1, LANES), const),
            pl.BlockSpec((tm, LANES), lambda i: (i % tiles_per_seq, 0)),
            pl.BlockSpec((tm, LANES), lambda i: (i % tiles_per_seq, 0)),
            pl.BlockSpec((HEAD_DIM, tm), lambda i: (0, i % tiles_per_seq)),
            pl.BlockSpec((HEAD_DIM, tm), lambda i: (0, i % tiles_per_seq)),
            pl.BlockSpec((LANES, LANES), const),
        ],
        out_specs=[
            pl.BlockSpec((tm, A_WIDTH), lambda i: (i, 0)),
            pl.BlockSpec((tm, QKV_WIDTH), lambda i: (i, 0)),
            pl.BlockSpec((tm, 2 * KV_WIDTH), lambda i: (i, 0)),
            pl.BlockSpec((Q_WIDTH, tm), lambda i: (0, i)),
            pl.BlockSpec((1, KV_WIDTH, tm), lambda i: (i, 0, 0)),
        ],
        out_shape=[
            jax.ShapeDtypeStruct((t, A_WIDTH), bf16),
            jax.ShapeDtypeStruct((t, QKV_WIDTH), bf16),
            jax.ShapeDtypeStruct((t, 2 * KV_WIDTH), bf16),
            jax.ShapeDtypeStruct((Q_WIDTH, t), bf16),
            jax.ShapeDtypeStruct((t // tm, KV_WIDTH, tm), bf16),
        ],
        compiler_params=pltpu.CompilerParams(
            dimension_semantics=("parallel",), vmem_limit_bytes=VMEM_LIMIT),
        name="in_proj",
    )(x2d, g1, w_in, wqt, wvt, lng, lnb, wsp, bsp, qgt, kg, cos_t, sin_t, cos_tt, sin_tt, seg)


def _band_attn_kernel(nb, sink_ref, q_ref, kp_ref, kc_ref, kn_ref, vp_ref, vc_ref, vn_ref,
                      bias_ref, o_ref):
    n = pl.program_id(0) % nb
    kb = jnp.concatenate([kp_ref[...], kc_ref[...], kn_ref[...]], axis=0)
    vb = jnp.concatenate([vp_ref[...], vc_ref[...], vn_ref[...]], axis=0)
    qpos = lax.broadcasted_iota(jnp.int32, (BLOCK, 3 * BLOCK), 0)
    jpos = lax.broadcasted_iota(jnp.int32, (BLOCK, 3 * BLOCK), 1)
    rel = jpos - BLOCK - qpos
    win = jnp.abs(rel) <= WINDOW
    kpos = (n - 1) * BLOCK + jpos
    mask = win & (kpos >= 0) & (kpos < nb * BLOCK)
    for h in range(N_HEADS):
        kv = h // GROUP
        qh = q_ref[:, h * HEAD_DIM:(h + 1) * HEAD_DIM]
        kh = kb[:, kv * HEAD_DIM:(kv + 1) * HEAD_DIM]
        vh = vb[:, kv * HEAD_DIM:(kv + 1) * HEAD_DIM]
        logits = _dot_nt(qh, kh) + bias_ref[h]
        logits = jnp.where(mask, logits, -jnp.inf)
        sk = sink_ref[h]
        m = jnp.maximum(jnp.max(logits, axis=-1, keepdims=True), sk)
        p = jnp.exp(logits - m)
        denom = jnp.sum(p, axis=-1, keepdims=True) + jnp.exp(sk - m)
        probs = (p / denom).astype(bf16)
        o_ref[:, h * HEAD_DIM:(h + 1) * HEAD_DIM] = _dot(probs, vh).astype(bf16)


def _band_attn(zb, seq, sink, bias):
    t = zb.shape[0]
    nb = seq // BLOCK
    nblocks = t // BLOCK
    kv_spec = lambda col, off: pl.BlockSpec(
        (BLOCK, KV_WIDTH), lambda i: (jnp.clip(i + off, 0, nblocks - 1), col))
    return pl.pallas_call(
        functools.partial(_band_attn_kernel, nb),
        grid=(nblocks,),
        in_specs=[
            pl.BlockSpec(memory_space=pltpu.SMEM),
            pl.BlockSpec((BLOCK, Q_WIDTH), lambda i: (i, 0)),
            kv_spec(K_COL_BLOCK, -1), kv_spec(K_COL_BLOCK, 0), kv_spec(K_COL_BLOCK, 1),
            kv_spec(V_COL_BLOCK, -1), kv_spec(V_COL_BLOCK, 0), kv_spec(V_COL_BLOCK, 1),
            pl.BlockSpec((N_HEADS, BLOCK, 3 * BLOCK), lambda i: (0, 0, 0)),
        ],
        out_specs=pl.BlockSpec((BLOCK, Q_WIDTH), lambda i: (i, 0)),
        out_shape=jax.ShapeDtypeStruct((t, Q_WIDTH), bf16),
        compiler_params=pltpu.CompilerParams(
            dimension_semantics=("parallel",), vmem_limit_bytes=VMEM_LIMIT),
        name="band_attn",
    )(sink, zb, zb, zb, zb, zb, zb, zb, bias)


def _flash_queries(qt_ref, pad):
    tq = qt_ref.shape[1]
    zeros = jnp.zeros((HEAD_DIM, GROUP * tq), bf16)
    q_ext = []
    for kv in range(N_KV):
        qg = jnp.concatenate(
            [qt_ref[h * HEAD_DIM:(h + 1) * HEAD_DIM, :]
             for h in range(kv * GROUP, (kv + 1) * GROUP)], axis=1)
        q_ext.append(jnp.concatenate(([qg, zeros] if kv == 0 else [zeros, qg]) + [pad], axis=0))
    return q_ext


def _flash_store(ot_ref, kv, out):
    tq = ot_ref.shape[1]
    for gi in range(GROUP):
        h = kv * GROUP + gi
        ot_ref[h * HEAD_DIM:(h + 1) * HEAD_DIM, :] = out[:, gi * tq:(gi + 1) * tq].astype(bf16)


def _flash_online_kernel(qt_ref, k_ref, vt_ref, ot_ref):
    nq = GROUP * qt_ref.shape[1]
    n_chunks = vt_ref.shape[0]
    tk = vt_ref.shape[2]
    q_ext = _flash_queries(qt_ref, jnp.zeros((KV_WIDTH, nq), bf16))

    def logits(c):
        start = pl.multiple_of(c * tk, tk)
        kc = k_ref[pl.ds(start, tk), :]
        return tuple(_dot(kc, q_ext[kv]) for kv in range(N_KV))

    def body(c, carry):
        s_all, stats = carry
        s_next = logits(jnp.minimum(c + 1, n_chunks - 1))
        vt = vt_ref[c]
        out = []
        for kv in range(N_KV):
            m, l, acc = stats[kv]
            s = s_all[kv]
            m_new = jnp.maximum(m, jnp.max(s, axis=0, keepdims=True))
            alpha = jnp.exp2(m - m_new)
            p = jnp.exp2(s - m_new)
            l_new = alpha * l + jnp.sum(p, axis=0, keepdims=True)
            pv = _dot(vt[kv * HEAD_DIM:(kv + 1) * HEAD_DIM], p.astype(bf16))
            out.append((m_new, l_new, alpha * acc + pv))
        return s_next, tuple(out)

    init = tuple((jnp.full((1, nq), -jnp.inf, f32), jnp.zeros((1, nq), f32),
                  jnp.zeros((HEAD_DIM, nq), f32)) for _ in range(N_KV))
    _, final = lax.fori_loop(0, n_chunks, body, (logits(0), init))
    for kv in range(N_KV):
        _, l, acc = final[kv]
        _flash_store(ot_ref, kv, acc / l)


def _flash_bounded_kernel(qt_ref, pad_ref, k_ref, vt_ref, ot_ref):
    nq = GROUP * qt_ref.shape[1]
    n_chunks = vt_ref.shape[0]
    tk = vt_ref.shape[2]
    q_ext = _flash_queries(qt_ref, pad_ref[...])
    ones = jnp.ones((2 * SUBLANES, tk), bf16)
    rows = HEAD_DIM + 2 * SUBLANES

    units = [(c, kv) for c in range(n_chunks) for kv in range(N_KV)]

    def logits(c, kv):
        return _dot(k_ref[c * tk:(c + 1) * tk, :], q_ext[kv])

    s_queue = [logits(*u) for u in units[:FLASH_LOOKAHEAD]]
    acc = [None] * N_KV
    for i, (c, kv) in enumerate(units):
        if i + FLASH_LOOKAHEAD < len(units):
            s_queue.append(logits(*units[i + FLASH_LOOKAHEAD]))
        p = jnp.exp2(s_queue[i]).astype(bf16)
        s_queue[i] = None
        v_ext = jnp.concatenate([vt_ref[c, kv * HEAD_DIM:(kv + 1) * HEAD_DIM, :], ones], axis=0)
        pv = _dot(v_ext, p)
        acc[kv] = pv if acc[kv] is None else acc[kv] + pv
    for kv in range(N_KV):
        _flash_store(ot_ref, kv, acc[kv][0:HEAD_DIM] / acc[kv][HEAD_DIM:HEAD_DIM + 1])


def _flash_attn(qct, pad, kc, vct, seq, bounded):
    t = kc.shape[0]
    tq = TQ_FLASH
    q_tiles = seq // tq
    tk = vct.shape[2]
    q_spec = pl.BlockSpec((Q_WIDTH, tq), lambda b, i: (0, b * q_tiles + i))
    kv_specs = [
        pl.BlockSpec((seq, 2 * KV_WIDTH), lambda b, i: (b, 0)),
        pl.BlockSpec((seq // tk, KV_WIDTH, tk), lambda b, i: (b, 0, 0)),
    ]
    pad_spec = pl.BlockSpec((KV_WIDTH, GROUP * tq), lambda b, i: (0, 0))
    return pl.pallas_call(
        _flash_bounded_kernel if bounded else _flash_online_kernel,
        grid=(t // seq, q_tiles),
        in_specs=[q_spec] + ([pad_spec] if bounded else []) + kv_specs,
        out_specs=q_spec,
        out_shape=jax.ShapeDtypeStruct((Q_WIDTH, t), bf16),
        compiler_params=pltpu.CompilerParams(
            dimension_semantics=("parallel", "parallel"), vmem_limit_bytes=VMEM_LIMIT),
        name="flash_bounded" if bounded else "flash_online",
    )(*([qct] + ([pad] if bounded else []) + [kc, vct]))


def _merge_kernel(x_ref, g1_ref, oa_ref, ob_ref, oct_ref, wg_ref, bg_ref, wbr_ref, wo_ref,
                  y_ref):
    x = x_ref[...]
    xn = (x * _rms_scale(x) * g1_ref[...]).astype(bf16)
    projs = (_dot(oa_ref[...], wbr_ref[0]), _dot(ob_ref[...], wbr_ref[1]),
             _dot_tn(oct_ref[...], wbr_ref[2]))
    merged = None
    for n, proj in enumerate(projs):
        cols = slice(n * D_MODEL, (n + 1) * D_MODEL)
        gate = _sigmoid(_dot(xn, wg_ref[:, cols]) + bg_ref[:, cols])
        term = gate * proj
        merged = term if merged is None else merged + term
    y_ref[...] = x + _dot(merged.astype(bf16), wo_ref[...])


def _merge(x2d, g1, oa, ob, oct, wg, bg, wbr, wo):
    t = x2d.shape[0]
    tm = TM_MERGE
    const = lambda i: (0, 0)
    row = lambda width: pl.BlockSpec((tm, width), lambda i: (i, 0))
    return pl.pallas_call(
        _merge_kernel,
        grid=(t // tm,),
        in_specs=[
            row(D_MODEL),
            pl.BlockSpec((1, D_MODEL), const),
            row(BR_WIDTH), row(BR_WIDTH),
            pl.BlockSpec((BR_WIDTH, tm), lambda i: (0, i)),
            pl.BlockSpec((D_MODEL, N_BRANCH * D_MODEL), const),
            pl.BlockSpec((1, N_BRANCH * D_MODEL), const),
            pl.BlockSpec((N_BRANCH, BR_WIDTH, D_MODEL), lambda i: (0, 0, 0)),
            pl.BlockSpec((D_MODEL, D_MODEL), const),
        ],
        out_specs=row(D_MODEL),
        out_shape=jax.ShapeDtypeStruct((t, D_MODEL), f32),
        compiler_params=pltpu.CompilerParams(
            dimension_semantics=("parallel",), vmem_limit_bytes=VMEM_LIMIT),
        name="merge",
    )(x2d, g1, oa, ob, oct, wg, bg, wbr, wo)


def _conv_ffn_kernel(tiles_per_seq, apply_final, xp_ref, x_ref, xq_ref, g2_ref, wup_ref,
                     cw_ref, cb_ref, wdn_ref, gf_ref, y_ref):
    tm = x_ref.shape[0]
    i = pl.program_id(0) % tiles_per_seq
    x = x_ref[...]
    halo_p = jnp.where(i > 0, xp_ref[...], 0.0)
    halo_n = jnp.where(i < tiles_per_seq - 1, xq_ref[...], 0.0)
    xe = jnp.concatenate([halo_p, x, halo_n], axis=0)
    xn = (xe * _rms_scale(xe) * g2_ref[...]).astype(bf16)
    rows = tm + 2 * SUBLANES
    acc = None
    for c in range(D_FF // FF_CHUNK):
        gcols = slice(c * FF_CHUNK, (c + 1) * FF_CHUNK)
        vcols = slice(D_FF + c * FF_CHUNK, D_FF + (c + 1) * FF_CHUNK)
        parts = []
        for cols in (gcols, vcols):
            h = _dot(xn, wup_ref[:, cols])
            hc = (pltpu.roll(h, 1, 0) * cw_ref[0:1, cols]
                  + h * cw_ref[1:2, cols]
                  + pltpu.roll(h, rows - 1, 0) * cw_ref[2:3, cols]
                  + cb_ref[:, cols])
            parts.append(hc[SUBLANES:SUBLANES + tm])
        act = (parts[0] * _sigmoid(parts[0]) * parts[1]).astype(bf16)
        term = _dot(act, wdn_ref[gcols, :])
        acc = term if acc is None else acc + term
    y = x + acc
    if apply_final:
        y = y * _rms_scale(y) * gf_ref[...]
    y_ref[...] = y


def _conv_ffn(x2d, seq, g2, wup, cw, cb, wdn, gf, apply_final):
    t = x2d.shape[0]
    tm = TM_FFN
    tiles_per_seq = seq // tm
    halo_per_tile = tm // SUBLANES
    n_halo = t // SUBLANES
    const = lambda i: (0, 0)
    return pl.pallas_call(
        functools.partial(_conv_ffn_kernel, tiles_per_seq, apply_final),
        grid=(t // tm,),
        in_specs=[
            pl.BlockSpec((SUBLANES, D_MODEL),
                         lambda i: (jnp.maximum(i * halo_per_tile - 1, 0), 0)),
            pl.BlockSpec((tm, D_MODEL), lambda i: (i, 0)),
            pl.BlockSpec((SUBLANES, D_MODEL),
                         lambda i: (jnp.minimum((i + 1) * halo_per_tile, n_halo - 1), 0)),
            pl.BlockSpec((1, D_MODEL), const),
            pl.BlockSpec((D_MODEL, 2 * D_FF), const),
            pl.BlockSpec((3, 2 * D_FF), const),
            pl.BlockSpec((1, 2 * D_FF), const),
            pl.BlockSpec((D_FF, D_MODEL), const),
            pl.BlockSpec((1, D_MODEL), const),
        ],
        out_specs=pl.BlockSpec((tm, D_MODEL), lambda i: (i, 0)),
        out_shape=jax.ShapeDtypeStruct((t, D_MODEL), f32),
        compiler_params=pltpu.CompilerParams(
            dimension_semantics=("parallel",), vmem_limit_bytes=VMEM_LIMIT),
        name="conv_ffn",
    )(x2d, x2d, x2d, g2, wup, cw, cb, wdn, gf)


def _t5_bucket(rel):
    half = N_BUCKETS // 2
    max_exact = half // 2
    ret = jnp.where(rel > 0, half, 0)
    n = jnp.abs(rel)
    nf = jnp.maximum(n, 1).astype(f32)
    large = max_exact + (jnp.log(nf / max_exact) / math.log(MAX_DIST / max_exact)
                         * (half - max_exact)).astype(jnp.int32)
    large = jnp.minimum(large, half - 1)
    return ret + jnp.where(n < max_exact, n, large)


def _band_bias(rel_bias):
    qpos = jnp.arange(BLOCK)[:, None]
    jpos = jnp.arange(3 * BLOCK)[None, :]
    rel = jpos - BLOCK - qpos
    return rel_bias[_t5_bucket(rel)].transpose(2, 0, 1).astype(f32)


def _rope_tables(seq):
    m = HEAD_DIM // 4
    pos = jnp.arange(seq)
    row = (pos // GRID_W).astype(f32)
    col = (pos % GRID_W).astype(f32)
    inv = ROPE_THETA ** (-jnp.arange(m, dtype=f32) / m)
    ang_r = row[:, None] * inv[None, :]
    ang_c = col[:, None] * inv[None, :]
    cos = jnp.concatenate([jnp.cos(ang_r), jnp.cos(ang_r), jnp.cos(ang_c), jnp.cos(ang_c)], axis=-1)
    sin = jnp.concatenate([-jnp.sin(ang_r), jnp.sin(ang_r), -jnp.sin(ang_c), jnp.sin(ang_c)], axis=-1)
    reps = LANES // HEAD_DIM
    return jnp.tile(cos, (1, reps)), jnp.tile(sin, (1, reps)), cos.T, sin.T


def _trunk(x, layers, bias, seg, final_g):
    bsz, seq, d = x.shape
    x2d = x.reshape(bsz * seq, d)
    cos_t, sin_t, cos_tt, sin_tt = _rope_tables(seq)
    for l, p in enumerate(layers):
        oa, zb, kc, qct, vct = _in_proj(
            x2d, seq, p["g1"], p["w_in"], p["wqt"], p["wvt"], p["lng"], p["lnb"], p["wsp"],
            p["bsp"], p["qgt"], p["kg"], cos_t, sin_t, cos_tt, sin_tt, seg)
        ob = _band_attn(zb, seq, p["sink"], bias)
        oct = lax.cond(
            p["logit_bound"] <= MAX_LOGIT_BOUND,
            lambda qct, pad, kc, vct: _flash_attn(qct, pad, kc, vct, seq, bounded=True),
            lambda qct, pad, kc, vct: _flash_attn(qct, pad, kc, vct, seq, bounded=False),
            qct, p["pad"], kc, vct)
        x2d = _merge(x2d, p["g1"], oa, ob, oct, p["wg"], p["bg"], p["wbr"], p["wo"])
        x2d = _conv_ffn(x2d, seq, p["g2"], p["wup"], p["cw"], p["cb"], p["wdn"], final_g,
                        apply_final=(l == len(layers) - 1))
    return x2d.reshape(bsz, seq, d)


def kernel(x_prompt, x_sample, rel_bias, norm1_g, w_in, ln_v_g, ln_v_b, w_spatial, b_spatial,
           sink, q_norm_g, k_norm_g, w_gate, b_gate, w_branch, w_out, norm2_g, w_up, conv_w,
           conv_b, w_down, final_g):
    bias = _band_bias(rel_bias)
    head_of_lane = jnp.arange(LANES) // HEAD_DIM
    seg = ((head_of_lane[:, None] == head_of_lane[None, :]).astype(f32) / HEAD_DIM).astype(bf16)
    reps = LANES // HEAD_DIM
    layers = []
    for l in range(DEPTH):
        ws = w_spatial[l].astype(bf16)
        wsp = jnp.concatenate([ws[0::2], ws[1::2]], axis=-1)
        wl = w_in[l].astype(bf16)
        c_off = A_IN + QKV_WIDTH
        logit_bound = (1.02 * HEAD_DIM ** 0.5) * jnp.max(jnp.abs(q_norm_g[l])) * jnp.max(
            jnp.abs(k_norm_g[l]))
        shift = -(logit_bound * (LOG2E * (1.0 + 2.0 ** -7))).astype(bf16)
        pad = jnp.zeros((KV_WIDTH, GROUP * TQ_FLASH), bf16).at[0, :].set(shift)
        layers.append(dict(
            logit_bound=logit_bound,
            pad=pad,
            g1=norm1_g[l][None, :],
            w_in=jnp.concatenate(
                [wl[:, :c_off], wl[:, c_off + Q_WIDTH:c_off + Q_WIDTH + KV_WIDTH]], axis=1),
            wqt=wl[:, c_off:c_off + Q_WIDTH].T,
            wvt=wl[:, c_off + Q_WIDTH + KV_WIDTH:].T,
            lng=ln_v_g[l][None, :],
            lnb=ln_v_b[l][None, :],
            wsp=wsp,
            bsp=jnp.repeat(b_spatial[l].T, HEAD_DIM, axis=1),
            qgt=jnp.broadcast_to(q_norm_g[l][:, None], (HEAD_DIM, TM_IN)),
            kg=jnp.tile(k_norm_g[l], reps)[None, :],
            sink=sink[l],
            wg=w_gate[l].astype(bf16),
            bg=b_gate[l][None, :],
            wbr=w_branch[l].astype(bf16),
            wo=w_out[l].astype(bf16),
            g2=norm2_g[l][None, :],
            wup=w_up[l].astype(bf16),
            cw=conv_w[l],
            cb=conv_b[l][None, :],
            wdn=w_down[l].astype(bf16),
        ))
    gf = final_g[None, :]
    y_prompt = _trunk(x_prompt, layers, bias, seg, gf)
    y_sample = _trunk(x_sample, layers, bias, seg, gf)
    return (y_prompt, y_sample)
```

```python
import functools
import math

import jax
import jax.numpy as jnp
from jax import lax
from jax.experimental import pallas as pl
from jax.experimental.pallas import tpu as pltpu

D_MODEL = 1024
DEPTH = 2
HEAD_DIM = 64
BLOCK = 128
A_GROUPS = 8
A_WIDTH = A_GROUPS * HEAD_DIM
N_HEADS = 8
N_KV = 2
GROUP = N_HEADS // N_KV
WINDOW = 128
ROPE_THETA = 10000.0
GRID_W = 64
N_BUCKETS = 32
MAX_DIST = 128
D_FF = 2816
EPS = 1e-6
N_BRANCH = 3
BR_WIDTH = 512
Q_WIDTH = N_HEADS * HEAD_DIM
KV_WIDTH = N_KV * HEAD_DIM
QKV_WIDTH = Q_WIDTH + 2 * KV_WIDTH
A_IN = 2 * A_WIDTH

LANES = 128
SUBLANES = 8
VMEM_LIMIT = 56 * 1024 * 1024

TM_IN = 256
TM_MERGE = 256
TM_FFN = 512
FF_CHUNK = 256
BAND_R = 4
BAND_LOOKAHEAD = 2
TQ_FLASH = 128
FLASH_LOOKAHEAD = 2
LOG2E = math.log2(math.e)
MAX_LOGIT_BOUND = 30.0

f32 = jnp.float32
bf16 = jnp.bfloat16


def _rms_scale(x):
    return lax.rsqrt(jnp.mean(x * x, axis=-1, keepdims=True) + EPS)


def _gelu_tanh(x):
    c = math.sqrt(2.0 / math.pi)
    return x * (0.5 * (1.0 + jnp.tanh(c * (x + 0.044715 * (x * x * x)))))


def _sigmoid(x):
    return 1.0 / (1.0 + jnp.exp(-x))


def _dot(a, b):
    return jnp.dot(a, b, preferred_element_type=f32)


def _dot_tn(a, b):
    return lax.dot_general(a, b, (((0,), (0,)), ((), ())), preferred_element_type=f32)


def _dot_nt(a, b):
    return lax.dot_general(a, b, (((1,), (1,)), ((), ())), preferred_element_type=f32)


def _head_mean_sq(x, seg_ref):
    sq = x * x
    hi = sq.astype(bf16)
    lo = (sq - hi.astype(f32)).astype(bf16)
    seg = seg_ref[...]
    return _dot(hi, seg) + _dot(lo, seg)


def _swap16(x):
    lane = lax.broadcasted_iota(jnp.int32, x.shape, 1)
    first_half = (lane % 32) < 16
    return jnp.where(first_half, pltpu.roll(x, LANES - 16, 1), pltpu.roll(x, 16, 1))


def _swap16_rows(x):
    h = HEAD_DIM // 4
    return jnp.concatenate([x[h:2 * h], x[0:h], x[3 * h:4 * h], x[2 * h:3 * h]], axis=0)


def _in_proj_kernel(x_ref, g1_ref, w_ref, wqt_ref, wvt_ref, lng_ref, lnb_ref, wsp_ref, bsp_ref,
                    qgt_ref, kg_ref, cos_ref, sin_ref, cost_ref, sint_ref, seg_ref,
                    oa_ref, kb_ref, kc_ref, qbt_ref, qct_ref, vbt_ref, vct_ref):
    tm = x_ref.shape[0]
    x = x_ref[...]
    xn = (x * _rms_scale(x) * g1_ref[...]).astype(bf16)

    u = _gelu_tanh(_dot(xn, w_ref[:, 0:A_WIDTH]))
    v = _gelu_tanh(_dot(xn, w_ref[:, A_WIDTH:A_IN]))
    mu = jnp.mean(v, axis=-1, keepdims=True)
    vc = v - mu
    vn = vc * lax.rsqrt(jnp.mean(vc * vc, axis=-1, keepdims=True) + EPS)
    vn = vn * lng_ref[...] + lnb_ref[...]
    lane = lax.broadcasted_iota(jnp.int32, (BLOCK, LANES), 1)
    low_half = lane < HEAD_DIM
    for c in range(tm // BLOCK):
        rows = slice(c * BLOCK, (c + 1) * BLOCK)
        for j in range(A_WIDTH // LANES):
            cols = slice(j * LANES, (j + 1) * LANES)
            vp = vn[rows, cols]
            stacked = jnp.concatenate(
                [jnp.where(low_half, vp, 0.0), jnp.where(low_half, 0.0, vp)], axis=0)
            sv = _dot(wsp_ref[j], stacked.astype(bf16)) + bsp_ref[:, cols]
            oa_ref[rows, cols] = (u[rows, cols] * sv).astype(bf16)

    zk = _dot(xn, w_ref[:, A_IN:A_IN + 2 * KV_WIDTH])
    kb_ref[...] = zk[:, 0:KV_WIDTH].astype(bf16)

    t = zk[:, KV_WIDTH:2 * KV_WIDTH]
    tn = t * lax.rsqrt(_head_mean_sq(t, seg_ref) + EPS) * kg_ref[...]
    kc_ref[:, 0:KV_WIDTH] = (tn * cos_ref[...] + _swap16(tn) * sin_ref[...]).astype(bf16)
    lane = lax.broadcasted_iota(jnp.int32, (tm, KV_WIDTH), 1)
    kc_ref[:, KV_WIDTH:2 * KV_WIDTH] = jnp.where(lane == 0, 1.0, 0.0).astype(bf16)

    scale = HEAD_DIM ** -0.5 * LOG2E
    qbt_ref[...] = (_dot_nt(wqt_ref[0:Q_WIDTH, :], xn) * scale).astype(bf16)
    qt = _dot_nt(wqt_ref[Q_WIDTH:2 * Q_WIDTH, :], xn)
    cost = cost_ref[...]
    sint = sint_ref[...]
    for h in range(N_HEADS):
        rows = slice(h * HEAD_DIM, (h + 1) * HEAD_DIM)
        th = qt[rows]
        r = lax.rsqrt(jnp.mean(th * th, axis=0, keepdims=True) + EPS)
        tn = th * r * qgt_ref[...]
        qct_ref[rows, :] = ((tn * cost + _swap16_rows(tn) * sint) * scale).astype(bf16)
    vt = _dot_nt(wvt_ref[...], xn).astype(bf16)
    for c in range(tm // BLOCK):
        vbt_ref[c] = vt[0:KV_WIDTH, c * BLOCK:(c + 1) * BLOCK]
    vct_ref[0] = vt[KV_WIDTH:2 * KV_WIDTH]


def _in_proj(x2d, seq, g1, w_in, wqt, wvt, lng, lnb, wsp, bsp, qgt, kg, cos_t, sin_t,
             cos_tt, sin_tt, seg):
    t = x2d.shape[0]
    tm = TM_IN
    tiles_per_seq = seq // tm
    const = lambda i: (0, 0)
    return pl.pallas_call(
        _in_proj_kernel,
        grid=(t // tm,),
        in_specs=[
            pl.BlockSpec((tm, D_MODEL), lambda i: (i, 0)),
            pl.BlockSpec((1, D_MODEL), const),
            pl.BlockSpec((D_MODEL, w_in.shape[1]), const),
            pl.BlockSpec((2 * Q_WIDTH, D_MODEL), const),
            pl.BlockSpec((2 * KV_WIDTH, D_MODEL), const),
            pl.BlockSpec((1, A_WIDTH), const),
            pl.BlockSpec((1, A_WIDTH), const),
            pl.BlockSpec((A_WIDTH // LANES, BLOCK, 2 * BLOCK), lambda i: (0, 0, 0)),
            pl.BlockSpec((BLOCK, A_WIDTH), const),
            pl.BlockSpec((HEAD_DIM, tm), const),
            pl.BlockSpec((1, LANES), const),
            pl.BlockSpec((tm, LANES), lambda i: (i % tiles_per_seq, 0)),
            pl.BlockSpec((tm, LANES), lambda i: (i % tiles_per_seq, 0)),
            pl.BlockSpec((HEAD_DIM, tm), lambda i: (0, i % tiles_per_seq)),
            pl.BlockSpec((HEAD_DIM, tm), lambda i: (0, i % tiles_per_seq)),
            pl.BlockSpec((LANES, LANES), const),
        ],
        out_specs=[
            pl.BlockSpec((tm, A_WIDTH), lambda i: (i, 0)),
            pl.BlockSpec((tm, KV_WIDTH), lambda i: (i, 0)),
            pl.BlockSpec((tm, 2 * KV_WIDTH), lambda i: (i, 0)),
            pl.BlockSpec((Q_WIDTH, tm), lambda i: (0, i)),
            pl.BlockSpec((Q_WIDTH, tm), lambda i: (0, i)),
            pl.BlockSpec((tm // BLOCK, KV_WIDTH, BLOCK), lambda i: (i, 0, 0)),
            pl.BlockSpec((1, KV_WIDTH, tm), lambda i: (i, 0, 0)),
        ],
        out_shape=[
            jax.ShapeDtypeStruct((t, A_WIDTH), bf16),
            jax.ShapeDtypeStruct((t, KV_WIDTH), bf16),
            jax.ShapeDtypeStruct((t, 2 * KV_WIDTH), bf16),
            jax.ShapeDtypeStruct((Q_WIDTH, t), bf16),
            jax.ShapeDtypeStruct((Q_WIDTH, t), bf16),
            jax.ShapeDtypeStruct((t // BLOCK, KV_WIDTH, BLOCK), bf16),
            jax.ShapeDtypeStruct((t // tm, KV_WIDTH, tm), bf16),
        ],
        compiler_params=pltpu.CompilerParams(
            dimension_semantics=("parallel",), vmem_limit_bytes=VMEM_LIMIT),
        name="in_proj",
    )(x2d, g1, w_in, wqt, wvt, lng, lnb, wsp, bsp, qgt, kg, cos_t, sin_t, cos_tt, sin_tt, seg)


def _band_bias_kernel(rel_bias_ref, bucket_ref, bias_ref):
    bucket = bucket_ref[...]
    for h in range(N_HEADS):
        val = jnp.full(bucket.shape, -jnp.inf, f32)
        for b in range(N_BUCKETS):
            val = jnp.where(bucket == b, rel_bias_ref[b, h] * LOG2E, val)
        gi = h % GROUP
        bias_ref[h // GROUP, :, gi * BLOCK:(gi + 1) * BLOCK] = val


def _band_bias(rel_bias, bucket_t):
    return pl.pallas_call(
        _band_bias_kernel,
        in_specs=[pl.BlockSpec(memory_space=pltpu.SMEM),
                  pl.BlockSpec((3 * BLOCK, BLOCK), lambda: (0, 0))],
        out_specs=pl.BlockSpec((N_KV, 3 * BLOCK, GROUP * BLOCK), lambda: (0, 0, 0)),
        out_shape=jax.ShapeDtypeStruct((N_KV, 3 * BLOCK, GROUP * BLOCK), f32),
        name="band_bias",
    )(rel_bias, bucket_t)


def _band_attn_kernel(nb, qt_ref, kp_ref, km_ref, kn_ref, vp_ref, vm_ref, vn_ref,
                      bias_ref, sink_ref, ot_ref):
    first = (pl.program_id(0) * BAND_R) % nb == 0
    last = ((pl.program_id(0) + 1) * BAND_R) % nb == 0
    k_win = jnp.concatenate([kp_ref[...], km_ref[...], kn_ref[...]], axis=0)
    v_blocks = [vp_ref[0]] + [vm_ref[r] for r in range(BAND_R)] + [vn_ref[0]]
    ones = jnp.ones((2 * SUBLANES, 3 * BLOCK), bf16)
    zeros = jnp.zeros((HEAD_DIM, GROUP * BLOCK), bf16)
    units = [(r, kv) for r in range(BAND_R) for kv in range(N_KV)]

    def logits(r, kv):
        qg = jnp.concatenate(
            [qt_ref[h * HEAD_DIM:(h + 1) * HEAD_DIM, r * BLOCK:(r + 1) * BLOCK]
             for h in range(kv * GROUP, (kv + 1) * GROUP)], axis=1)
        q_ext = jnp.concatenate([qg, zeros] if kv == 0 else [zeros, qg], axis=0)
        return _dot(k_win[r * BLOCK:(r + 3) * BLOCK], q_ext)

    s_queue = [logits(*u) for u in units[:BAND_LOOKAHEAD]]
    for i, (r, kv) in enumerate(units):
        if i + BAND_LOOKAHEAD < len(units):
            s_queue.append(logits(*units[i + BAND_LOOKAHEAD]))
        s = s_queue[i] + bias_ref[kv]
        s_queue[i] = None
        if r == 0:
            s = jnp.concatenate([jnp.where(first, -jnp.inf, s[0:BLOCK]), s[BLOCK:]], axis=0)
        if r == BAND_R - 1:
            s = jnp.concatenate([s[:2 * BLOCK], jnp.where(last, -jnp.inf, s[2 * BLOCK:])], axis=0)
        sink = sink_ref[kv]
        m = jnp.maximum(jnp.max(s, axis=0, keepdims=True), sink)
        p = jnp.exp2(s - m).astype(bf16)
        v_band = jnp.concatenate(
            [blk[kv * HEAD_DIM:(kv + 1) * HEAD_DIM] for blk in v_blocks[r:r + 3]], axis=1)
        pv = _dot(jnp.concatenate([v_band, ones], axis=0), p)
        out = pv[0:HEAD_DIM] / (pv[HEAD_DIM:HEAD_DIM + 1] + jnp.exp2(sink - m))
        for gi in range(GROUP):
            h = kv * GROUP + gi
            ot_ref[h * HEAD_DIM:(h + 1) * HEAD_DIM, r * BLOCK:(r + 1) * BLOCK] = (
                out[:, gi * BLOCK:(gi + 1) * BLOCK].astype(bf16))


def _band_attn(qbt, kb, vbt, seq, sink_rows, bias_t):
    t = kb.shape[0]
    nb = seq // BLOCK
    nblocks = t // BLOCK
    r = BAND_R
    prev_blk = lambda i: jnp.maximum(i * r - 1, 0)
    next_blk = lambda i: jnp.minimum((i + 1) * r, nblocks - 1)
    return pl.pallas_call(
        functools.partial(_band_attn_kernel, nb),
        grid=(nblocks // r,),
        in_specs=[
            pl.BlockSpec((Q_WIDTH, r * BLOCK), lambda i: (0, i)),
            pl.BlockSpec((BLOCK, KV_WIDTH), lambda i: (prev_blk(i), 0)),
            pl.BlockSpec((r * BLOCK, KV_WIDTH), lambda i: (i, 0)),
            pl.BlockSpec((BLOCK, KV_WIDTH), lambda i: (next_blk(i), 0)),
            pl.BlockSpec((1, KV_WIDTH, BLOCK), lambda i: (prev_blk(i), 0, 0)),
            pl.BlockSpec((r, KV_WIDTH, BLOCK), lambda i: (i, 0, 0)),
            pl.BlockSpec((1, KV_WIDTH, BLOCK), lambda i: (next_blk(i), 0, 0)),
            pl.BlockSpec((N_KV, 3 * BLOCK, GROUP * BLOCK), lambda i: (0, 0, 0)),
            pl.BlockSpec((N_KV, 1, GROUP * BLOCK), lambda i: (0, 0, 0)),
        ],
        out_specs=pl.BlockSpec((Q_WIDTH, r * BLOCK), lambda i: (0, i)),
        out_shape=jax.ShapeDtypeStruct((Q_WIDTH, t), bf16),
        compiler_params=pltpu.CompilerParams(
            dimension_semantics=("parallel",), vmem_limit_bytes=VMEM_LIMIT),
        name="band_attn",
    )(qbt, kb, kb, kb, vbt, vbt, vbt, bias_t, sink_rows)


def _flash_queries(qt_ref, pad):
    tq = qt_ref.shape[1]
    zeros = jnp.zeros((HEAD_DIM, GROUP * tq), bf16)
    q_ext = []
    for kv in range(N_KV):
        qg = jnp.concatenate(
            [qt_ref[h * HEAD_DIM:(h + 1) * HEAD_DIM, :]
             for h in range(kv * GROUP, (kv + 1) * GROUP)], axis=1)
        q_ext.append(jnp.concatenate(([qg, zeros] if kv == 0 else [zeros, qg]) + [pad], axis=0))
    return q_ext


def _flash_store(ot_ref, kv, out):
    tq = ot_ref.shape[1]
    for gi in range(GROUP):
        h = kv * GROUP + gi
        ot_ref[h * HEAD_DIM:(h + 1) * HEAD_DIM, :] = out[:, gi * tq:(gi + 1) * tq].astype(bf16)


def _flash_online_kernel(qt_ref, k_ref, vt_ref, ot_ref):
    nq = GROUP * qt_ref.shape[1]
    n_chunks = vt_ref.shape[0]
    tk = vt_ref.shape[2]
    q_ext = _flash_queries(qt_ref, jnp.zeros((KV_WIDTH, nq), bf16))

    def logits(c):
        start = pl.multiple_of(c * tk, tk)
        kc = k_ref[pl.ds(start, tk), :]
        return tuple(_dot(kc, q_ext[kv]) for kv in range(N_KV))

    def body(c, carry):
        s_all, stats = carry
        s_next = logits(jnp.minimum(c + 1, n_chunks - 1))
        vt = vt_ref[c]
        out = []
        for kv in range(N_KV):
            m, l, acc = stats[kv]
            s = s_all[kv]
            m_new = jnp.maximum(m, jnp.max(s, axis=0, keepdims=True))
            alpha = jnp.exp2(m - m_new)
            p = jnp.exp2(s - m_new)
            l_new = alpha * l + jnp.sum(p, axis=0, keepdims=True)
            pv = _dot(vt[kv * HEAD_DIM:(kv + 1) * HEAD_DIM], p.astype(bf16))
            out.append((m_new, l_new, alpha * acc + pv))
        return s_next, tuple(out)

    init = tuple((jnp.full((1, nq), -jnp.inf, f32), jnp.zeros((1, nq), f32),
                  jnp.zeros((HEAD_DIM, nq), f32)) for _ in range(N_KV))
    _, final = lax.fori_loop(0, n_chunks, body, (logits(0), init))
    for kv in range(N_KV):
        _, l, acc = final[kv]
        _flash_store(ot_ref, kv, acc / l)


def _flash_bounded_kernel(qt_ref, pad_ref, k_ref, vt_ref, ot_ref):
    nq = GROUP * qt_ref.shape[1]
    n_chunks = vt_ref.shape[0]
    tk = vt_ref.shape[2]
    q_ext = _flash_queries(qt_ref, pad_ref[...])
    ones = jnp.ones((2 * SUBLANES, tk), bf16)
    rows = HEAD_DIM + 2 * SUBLANES

    units = [(c, kv) for c in range(n_chunks) for kv in range(N_KV)]

    def logits(c, kv):
        return _dot(k_ref[c * tk:(c + 1) * tk, :], q_ext[kv])

    s_queue = [logits(*u) for u in units[:FLASH_LOOKAHEAD]]
    acc = [None] * N_KV
    for i, (c, kv) in enumerate(units):
        if i + FLASH_LOOKAHEAD < len(units):
            s_queue.append(logits(*units[i + FLASH_LOOKAHEAD]))
        p = jnp.exp2(s_queue[i]).astype(bf16)
        s_queue[i] = None
        v_ext = jnp.concatenate([vt_ref[c, kv * HEAD_DIM:(kv + 1) * HEAD_DIM, :], ones], axis=0)
        pv = _dot(v_ext, p)
        acc[kv] = pv if acc[kv] is None else acc[kv] + pv
    for kv in range(N_KV):
        _flash_store(ot_ref, kv, acc[kv][0:HEAD_DIM] / acc[kv][HEAD_DIM:HEAD_DIM + 1])


def _flash_attn(qct, pad, kc, vct, seq, bounded):
    t = kc.shape[0]
    tq = TQ_FLASH
    q_tiles = seq // tq
    tk = vct.shape[2]
    q_spec = pl.BlockSpec((Q_WIDTH, tq), lambda b, i: (0, b * q_tiles + i))
    kv_specs = [
        pl.BlockSpec((seq, 2 * KV_WIDTH), lambda b, i: (b, 0)),
        pl.BlockSpec((seq // tk, KV_WIDTH, tk), lambda b, i: (b, 0, 0)),
    ]
    pad_spec = pl.BlockSpec((KV_WIDTH, GROUP * tq), lambda b, i: (0, 0))
    return pl.pallas_call(
        _flash_bounded_kernel if bounded else _flash_online_kernel,
        grid=(t // seq, q_tiles),
        in_specs=[q_spec] + ([pad_spec] if bounded else []) + kv_specs,
        out_specs=q_spec,
        out_shape=jax.ShapeDtypeStruct((Q_WIDTH, t), bf16),
        compiler_params=pltpu.CompilerParams(
            dimension_semantics=("parallel", "parallel"), vmem_limit_bytes=VMEM_LIMIT),
        name="flash_bounded" if bounded else "flash_online",
    )(*([qct] + ([pad] if bounded else []) + [kc, vct]))


def _merge_kernel(x_ref, g1_ref, oa_ref, obt_ref, oct_ref, wg_ref, bg_ref, wbr_ref, wo_ref,
                  y_ref):
    x = x_ref[...]
    xn = (x * _rms_scale(x) * g1_ref[...]).astype(bf16)
    projs = (_dot(oa_ref[...], wbr_ref[0]), _dot_tn(obt_ref[...], wbr_ref[1]),
             _dot_tn(oct_ref[...], wbr_ref[2]))
    merged = None
    for n, proj in enumerate(projs):
        cols = slice(n * D_MODEL, (n + 1) * D_MODEL)
        gate = _sigmoid(_dot(xn, wg_ref[:, cols]) + bg_ref[:, cols])
        term = gate * proj
        merged = term if merged is None else merged + term
    y_ref[...] = x + _dot(merged.astype(bf16), wo_ref[...])


def _merge(x2d, g1, oa, obt, oct, wg, bg, wbr, wo):
    t = x2d.shape[0]
    tm = TM_MERGE
    const = lambda i: (0, 0)
    row = lambda width: pl.BlockSpec((tm, width), lambda i: (i, 0))
    return pl.pallas_call(
        _merge_kernel,
        grid=(t // tm,),
        in_specs=[
            row(D_MODEL),
            pl.BlockSpec((1, D_MODEL), const),
            row(BR_WIDTH),
            pl.BlockSpec((BR_WIDTH, tm), lambda i: (0, i)),
            pl.BlockSpec((BR_WIDTH, tm), lambda i: (0, i)),
            pl.BlockSpec((D_MODEL, N_BRANCH * D_MODEL), const),
            pl.BlockSpec((1, N_BRANCH * D_MODEL), const),
            pl.BlockSpec((N_BRANCH, BR_WIDTH, D_MODEL), lambda i: (0, 0, 0)),
            pl.BlockSpec((D_MODEL, D_MODEL), const),
        ],
        out_specs=row(D_MODEL),
        out_shape=jax.ShapeDtypeStruct((t, D_MODEL), f32),
        compiler_params=pltpu.CompilerParams(
            dimension_semantics=("parallel",), vmem_limit_bytes=VMEM_LIMIT),
        name="merge",
    )(x2d, g1, oa, obt, oct, wg, bg, wbr, wo)


def _conv_ffn_kernel(tiles_per_seq, apply_final, xp_ref, x_ref, xq_ref, g2_ref, wup_ref,
                     cw_ref, cb_ref, wdn_ref, gf_ref, y_ref):
    tm = x_ref.shape[0]
    i = pl.program_id(0) % tiles_per_seq
    x = x_ref[...]
    halo_p = jnp.where(i > 0, xp_ref[...], 0.0)
    halo_n = jnp.where(i < tiles_per_seq - 1, xq_ref[...], 0.0)
    xe = jnp.concatenate([halo_p, x, halo_n], axis=0)
    xn = (xe * _rms_scale(xe) * g2_ref[...]).astype(bf16)
    rows = tm + 2 * SUBLANES
    acc = None
    for c in range(D_FF // FF_CHUNK):
        gcols = slice(c * FF_CHUNK, (c + 1) * FF_CHUNK)
        vcols = slice(D_FF + c * FF_CHUNK, D_FF + (c + 1) * FF_CHUNK)
        parts = []
        for cols in (gcols, vcols):
            h = _dot(xn, wup_ref[:, cols])
            hc = (pltpu.roll(h, 1, 0) * cw_ref[0:1, cols]
                  + h * cw_ref[1:2, cols]
                  + pltpu.roll(h, rows - 1, 0) * cw_ref[2:3, cols]
                  + cb_ref[:, cols])
            parts.append(hc[SUBLANES:SUBLANES + tm])
        act = (parts[0] * _sigmoid(parts[0]) * parts[1]).astype(bf16)
        term = _dot(act, wdn_ref[gcols, :])
        acc = term if acc is None else acc + term
    y = x + acc
    if apply_final:
        y = y * _rms_scale(y) * gf_ref[...]
    y_ref[...] = y


def _conv_ffn(x2d, seq, g2, wup, cw, cb, wdn, gf, apply_final):
    t = x2d.shape[0]
    tm = TM_FFN
    tiles_per_seq = seq // tm
    halo_per_tile = tm // SUBLANES
    n_halo = t // SUBLANES
    const = lambda i: (0, 0)
    return pl.pallas_call(
        functools.partial(_conv_ffn_kernel, tiles_per_seq, apply_final),
        grid=(t // tm,),
        in_specs=[
            pl.BlockSpec((SUBLANES, D_MODEL),
                         lambda i: (jnp.maximum(i * halo_per_tile - 1, 0), 0)),
            pl.BlockSpec((tm, D_MODEL), lambda i: (i, 0)),
            pl.BlockSpec((SUBLANES, D_MODEL),
                         lambda i: (jnp.minimum((i + 1) * halo_per_tile, n_halo - 1), 0)),
            pl.BlockSpec((1, D_MODEL), const),
            pl.BlockSpec((D_MODEL, 2 * D_FF), const),
            pl.BlockSpec((3, 2 * D_FF), const),
            pl.BlockSpec((1, 2 * D_FF), const),
            pl.BlockSpec((D_FF, D_MODEL), const),
            pl.BlockSpec((1, D_MODEL), const),
        ],
        out_specs=pl.BlockSpec((tm, D_MODEL), lambda i: (i, 0)),
        out_shape=jax.ShapeDtypeStruct((t, D_MODEL), f32),
        compiler_params=pltpu.CompilerParams(
            dimension_semantics=("parallel",), vmem_limit_bytes=VMEM_LIMIT),
        name="conv_ffn",
    )(x2d, x2d, x2d, g2, wup, cw, cb, wdn, gf)


def _t5_bucket(rel):
    half = N_BUCKETS // 2
    max_exact = half // 2
    ret = jnp.where(rel > 0, half, 0)
    n = jnp.abs(rel)
    nf = jnp.maximum(n, 1).astype(f32)
    large = max_exact + (jnp.log(nf / max_exact) / math.log(MAX_DIST / max_exact)
                         * (half - max_exact)).astype(jnp.int32)
    large = jnp.minimum(large, half - 1)
    return ret + jnp.where(n < max_exact, n, large)


def _band_buckets():
    jpos = jnp.arange(3 * BLOCK)[:, None]
    qpos = jnp.arange(BLOCK)[None, :]
    rel = jpos - BLOCK - qpos
    return jnp.where(jnp.abs(rel) <= WINDOW, _t5_bucket(rel), -1).astype(jnp.int32)


def _rope_tables(seq):
    m = HEAD_DIM // 4
    pos = jnp.arange(seq)
    row = (pos // GRID_W).astype(f32)
    col = (pos % GRID_W).astype(f32)
    inv = ROPE_THETA ** (-jnp.arange(m, dtype=f32) / m)
    ang_r = row[:, None] * inv[None, :]
    ang_c = col[:, None] * inv[None, :]
    cos = jnp.concatenate([jnp.cos(ang_r), jnp.cos(ang_r), jnp.cos(ang_c), jnp.cos(ang_c)], axis=-1)
    sin = jnp.concatenate([-jnp.sin(ang_r), jnp.sin(ang_r), -jnp.sin(ang_c), jnp.sin(ang_c)], axis=-1)
    reps = LANES // HEAD_DIM
    return jnp.tile(cos, (1, reps)), jnp.tile(sin, (1, reps)), cos.T, sin.T


def _trunk(x, layers, bias, seg, final_g):
    bsz, seq, d = x.shape
    x2d = x.reshape(bsz * seq, d)
    cos_t, sin_t, cos_tt, sin_tt = _rope_tables(seq)
    for l, p in enumerate(layers):
        oa, kb, kc, qbt, qct, vbt, vct = _in_proj(
            x2d, seq, p["g1"], p["w_in"], p["wqt"], p["wvt"], p["lng"], p["lnb"], p["wsp"],
            p["bsp"], p["qgt"], p["kg"], cos_t, sin_t, cos_tt, sin_tt, seg)
        obt = _band_attn(qbt, kb, vbt, seq, p["sink"], bias)
        oct = lax.cond(
            p["logit_bound"] <= MAX_LOGIT_BOUND,
            lambda qct, pad, kc, vct: _flash_attn(qct, pad, kc, vct, seq, bounded=True),
            lambda qct, pad, kc, vct: _flash_attn(qct, pad, kc, vct, seq, bounded=False),
            qct, p["pad"], kc, vct)
        x2d = _merge(x2d, p["g1"], oa, obt, oct, p["wg"], p["bg"], p["wbr"], p["wo"])
        x2d = _conv_ffn(x2d, seq, p["g2"], p["wup"], p["cw"], p["cb"], p["wdn"], final_g,
                        apply_final=(l == len(layers) - 1))
    return x2d.reshape(bsz, seq, d)


def kernel(x_prompt, x_sample, rel_bias, norm1_g, w_in, ln_v_g, ln_v_b, w_spatial, b_spatial,
           sink, q_norm_g, k_norm_g, w_gate, b_gate, w_branch, w_out, norm2_g, w_up, conv_w,
           conv_b, w_down, final_g):
    bias = _band_bias(rel_bias, _band_buckets())
    head_of_lane = jnp.arange(LANES) // HEAD_DIM
    seg = ((head_of_lane[:, None] == head_of_lane[None, :]).astype(f32) / HEAD_DIM).astype(bf16)
    reps = LANES // HEAD_DIM
    layers = []
    for l in range(DEPTH):
        ws = w_spatial[l].astype(bf16)
        wsp = jnp.concatenate([ws[0::2], ws[1::2]], axis=-1)
        wl = w_in[l].astype(bf16)
        b_q, b_k, b_v = A_IN, A_IN + Q_WIDTH, A_IN + Q_WIDTH + KV_WIDTH
        c_q, c_k, c_v = (b + QKV_WIDTH for b in (b_q, b_k, b_v))
        logit_bound = (1.02 * HEAD_DIM ** 0.5) * jnp.max(jnp.abs(q_norm_g[l])) * jnp.max(
            jnp.abs(k_norm_g[l]))
        shift = -(logit_bound * (LOG2E * (1.0 + 2.0 ** -7))).astype(bf16)
        pad = jnp.zeros((KV_WIDTH, GROUP * TQ_FLASH), bf16).at[0, :].set(shift)
        layers.append(dict(
            logit_bound=logit_bound,
            pad=pad,
            g1=norm1_g[l][None, :],
            w_in=jnp.concatenate(
                [wl[:, :A_IN], wl[:, b_k:b_k + KV_WIDTH], wl[:, c_k:c_k + KV_WIDTH]], axis=1),
            wqt=jnp.concatenate([wl[:, b_q:b_q + Q_WIDTH], wl[:, c_q:c_q + Q_WIDTH]], axis=1).T,
            wvt=jnp.concatenate([wl[:, b_v:b_v + KV_WIDTH], wl[:, c_v:c_v + KV_WIDTH]], axis=1).T,
            lng=ln_v_g[l][None, :],
            lnb=ln_v_b[l][None, :],
            wsp=wsp,
            bsp=jnp.repeat(b_spatial[l].T, HEAD_DIM, axis=1),
            qgt=jnp.broadcast_to(q_norm_g[l][:, None], (HEAD_DIM, TM_IN)),
            kg=jnp.tile(k_norm_g[l], reps)[None, :],
            sink=jnp.repeat(sink[l] * LOG2E, BLOCK).reshape(N_KV, 1, GROUP * BLOCK),
            wg=w_gate[l].astype(bf16),
            bg=b_gate[l][None, :],
            wbr=w_branch[l].astype(bf16),
            wo=w_out[l].astype(bf16),
            g2=norm2_g[l][None, :],
            wup=w_up[l].astype(bf16),
            cw=conv_w[l],
            cb=conv_b[l][None, :],
            wdn=w_down[l].astype(bf16),
        ))
    gf = final_g[None, :]
    y_prompt = _trunk(x_prompt, layers, bias, seg, gf)
    y_sample = _trunk(x_sample, layers, bias, seg, gf)
    return (y_prompt, y_sample)
```

```python
import functools
import math

import jax
import jax.numpy as jnp
from jax import lax
from jax.experimental import pallas as pl
from jax.experimental.pallas import tpu as pltpu

D_MODEL = 1024
DEPTH = 2
HEAD_DIM = 64
BLOCK = 128
A_GROUPS = 8
A_WIDTH = A_GROUPS * HEAD_DIM
N_HEADS = 8
N_KV = 2
GROUP = N_HEADS // N_KV
WINDOW = 128
ROPE_THETA = 10000.0
GRID_W = 64
N_BUCKETS = 32
MAX_DIST = 128
D_FF = 2816
EPS = 1e-6
N_BRANCH = 3
BR_WIDTH = 512
Q_WIDTH = N_HEADS * HEAD_DIM
KV_WIDTH = N_KV * HEAD_DIM
QKV_WIDTH = Q_WIDTH + 2 * KV_WIDTH
A_IN = 2 * A_WIDTH

LANES = 128
SUBLANES = 8
VMEM_LIMIT = 56 * 1024 * 1024

TM_IN = 256
TM_MERGE = 256
TM_FFN = 512
FF_CHUNK = 256
FFN_LOOKAHEAD = 2
BAND_R = 4
BAND_LOOKAHEAD = 2
TQ_FLASH = 128
FLASH_LOOKAHEAD = 2
LOG2E = math.log2(math.e)
MAX_LOGIT_BOUND = 30.0

f32 = jnp.float32
bf16 = jnp.bfloat16


def _rms_scale(x):
    return lax.rsqrt(jnp.mean(x * x, axis=-1, keepdims=True) + EPS)


def _gelu_tanh(x):
    c = math.sqrt(2.0 / math.pi)
    return x * (0.5 * (1.0 + jnp.tanh(c * (x + 0.044715 * (x * x * x)))))


def _sigmoid(x):
    return 1.0 / (1.0 + jnp.exp(-x))


def _dot(a, b):
    return jnp.dot(a, b, preferred_element_type=f32)


def _dot_tn(a, b):
    return lax.dot_general(a, b, (((0,), (0,)), ((), ())), preferred_element_type=f32)


def _dot_nt(a, b):
    return lax.dot_general(a, b, (((1,), (1,)), ((), ())), preferred_element_type=f32)


def _head_mean_sq(x, seg_ref):
    sq = x * x
    hi = sq.astype(bf16)
    lo = (sq - hi.astype(f32)).astype(bf16)
    seg = seg_ref[...]
    return _dot(hi, seg) + _dot(lo, seg)


def _swap16(x):
    lane = lax.broadcasted_iota(jnp.int32, x.shape, 1)
    first_half = (lane % 32) < 16
    return jnp.where(first_half, pltpu.roll(x, LANES - 16, 1), pltpu.roll(x, 16, 1))


def _swap16_rows(x):
    h = HEAD_DIM // 4
    return jnp.concatenate([x[h:2 * h], x[0:h], x[3 * h:4 * h], x[2 * h:3 * h]], axis=0)


def _in_proj_kernel(x_ref, g1_ref, w_ref, wqt_ref, wvt_ref, lng_ref, lnb_ref, wsp_ref, bsp_ref,
                    qgt_ref, kg_ref, cos_ref, sin_ref, cost_ref, sint_ref, seg_ref,
                    oa_ref, kb_ref, kc_ref, qbt_ref, qct_ref, vbt_ref, vct_ref):
    tm = x_ref.shape[0]
    x = x_ref[...]
    xn = (x * _rms_scale(x) * g1_ref[...]).astype(bf16)

    zv = _dot(xn, w_ref[:, A_WIDTH:A_IN])
    zu = _dot(xn, w_ref[:, 0:A_WIDTH])
    zk = _dot(xn, w_ref[:, A_IN:A_IN + 2 * KV_WIDTH])
    qbt = _dot_nt(wqt_ref[0:Q_WIDTH, :], xn)
    qt = _dot_nt(wqt_ref[Q_WIDTH:2 * Q_WIDTH, :], xn)
    vt = _dot_nt(wvt_ref[...], xn).astype(bf16)

    v = _gelu_tanh(zv)
    u = _gelu_tanh(zu)
    mu = jnp.mean(v, axis=-1, keepdims=True)
    vc = v - mu
    vn = vc * lax.rsqrt(jnp.mean(vc * vc, axis=-1, keepdims=True) + EPS)
    vn = vn * lng_ref[...] + lnb_ref[...]
    lane = lax.broadcasted_iota(jnp.int32, (BLOCK, LANES), 1)
    low_half = lane < HEAD_DIM
    for c in range(tm // BLOCK):
        rows = slice(c * BLOCK, (c + 1) * BLOCK)
        for j in range(A_WIDTH // LANES):
            cols = slice(j * LANES, (j + 1) * LANES)
            vp = vn[rows, cols]
            stacked = jnp.concatenate(
                [jnp.where(low_half, vp, 0.0), jnp.where(low_half, 0.0, vp)], axis=0)
            sv = _dot(wsp_ref[j], stacked.astype(bf16)) + bsp_ref[:, cols]
            oa_ref[rows, cols] = (u[rows, cols] * sv).astype(bf16)

    kb_ref[...] = zk[:, 0:KV_WIDTH].astype(bf16)

    t = zk[:, KV_WIDTH:2 * KV_WIDTH]
    tn = t * lax.rsqrt(_head_mean_sq(t, seg_ref) + EPS) * kg_ref[...]
    kc_ref[:, 0:KV_WIDTH] = (tn * cos_ref[...] + _swap16(tn) * sin_ref[...]).astype(bf16)
    lane = lax.broadcasted_iota(jnp.int32, (tm, KV_WIDTH), 1)
    kc_ref[:, KV_WIDTH:2 * KV_WIDTH] = jnp.where(lane == 0, 1.0, 0.0).astype(bf16)

    scale = HEAD_DIM ** -0.5 * LOG2E
    qbt_ref[...] = (qbt * scale).astype(bf16)
    cost = cost_ref[...]
    sint = sint_ref[...]
    for h in range(N_HEADS):
        rows = slice(h * HEAD_DIM, (h + 1) * HEAD_DIM)
        th = qt[rows]
        r = lax.rsqrt(jnp.mean(th * th, axis=0, keepdims=True) + EPS)
        tn = th * r * qgt_ref[...]
        qct_ref[rows, :] = ((tn * cost + _swap16_rows(tn) * sint) * scale).astype(bf16)
    for c in range(tm // BLOCK):
        vbt_ref[c] = vt[0:KV_WIDTH, c * BLOCK:(c + 1) * BLOCK]
    vct_ref[0] = vt[KV_WIDTH:2 * KV_WIDTH]


def _in_proj(x2d, seq, g1, w_in, wqt, wvt, lng, lnb, wsp, bsp, qgt, kg, cos_t, sin_t,
             cos_tt, sin_tt, seg):
    t = x2d.shape[0]
    tm = TM_IN
    tiles_per_seq = seq // tm
    const = lambda i: (0, 0)
    return pl.pallas_call(
        _in_proj_kernel,
        grid=(t // tm,),
        in_specs=[
            pl.BlockSpec((tm, D_MODEL), lambda i: (i, 0)),
            pl.BlockSpec((1, D_MODEL), const),
            pl.BlockSpec((D_MODEL, w_in.shape[1]), const),
            pl.BlockSpec((2 * Q_WIDTH, D_MODEL), const),
            pl.BlockSpec((2 * KV_WIDTH, D_MODEL), const),
            pl.BlockSpec((1, A_WIDTH), const),
            pl.BlockSpec((1, A_WIDTH), const),
            pl.BlockSpec((A_WIDTH // LANES, BLOCK, 2 * BLOCK), lambda i: (0, 0, 0)),
            pl.BlockSpec((BLOCK, A_WIDTH), const),
            pl.BlockSpec((HEAD_DIM, tm), const),
            pl.BlockSpec((1, LANES), const),
            pl.BlockSpec((tm, LANES), lambda i: (i % tiles_per_seq, 0)),
            pl.BlockSpec((tm, LANES), lambda i: (i % tiles_per_seq, 0)),
            pl.BlockSpec((HEAD_DIM, tm), lambda i: (0, i % tiles_per_seq)),
            pl.BlockSpec((HEAD_DIM, tm), lambda i: (0, i % tiles_per_seq)),
            pl.BlockSpec((LANES, LANES), const),
        ],
        out_specs=[
            pl.BlockSpec((tm, A_WIDTH), lambda i: (i, 0)),
            pl.BlockSpec((tm, KV_WIDTH), lambda i: (i, 0)),
            pl.BlockSpec((tm, 2 * KV_WIDTH), lambda i: (i, 0)),
            pl.BlockSpec((Q_WIDTH, tm), lambda i: (0, i)),
            pl.BlockSpec((Q_WIDTH, tm), lambda i: (0, i)),
            pl.BlockSpec((tm // BLOCK, KV_WIDTH, BLOCK), lambda i: (i, 0, 0)),
            pl.BlockSpec((1, KV_WIDTH, tm), lambda i: (i, 0, 0)),
        ],
        out_shape=[
            jax.ShapeDtypeStruct((t, A_WIDTH), bf16),
            jax.ShapeDtypeStruct((t, KV_WIDTH), bf16),
            jax.ShapeDtypeStruct((t, 2 * KV_WIDTH), bf16),
            jax.ShapeDtypeStruct((Q_WIDTH, t), bf16),
            jax.ShapeDtypeStruct((Q_WIDTH, t), bf16),
            jax.ShapeDtypeStruct((t // BLOCK, KV_WIDTH, BLOCK), bf16),
            jax.ShapeDtypeStruct((t // tm, KV_WIDTH, tm), bf16),
        ],
        compiler_params=pltpu.CompilerParams(
            dimension_semantics=("parallel",), vmem_limit_bytes=VMEM_LIMIT),
        name="in_proj",
    )(x2d, g1, w_in, wqt, wvt, lng, lnb, wsp, bsp, qgt, kg, cos_t, sin_t, cos_tt, sin_tt, seg)


def _band_bias_kernel(rel_bias_ref, bucket_ref, bias_ref):
    bucket = bucket_ref[...]
    for h in range(N_HEADS):
        val = jnp.full(bucket.shape, -jnp.inf, f32)
        for b in range(N_BUCKETS):
            val = jnp.where(bucket == b, rel_bias_ref[b, h] * LOG2E, val)
        gi = h % GROUP
        bias_ref[h // GROUP, :, gi * BLOCK:(gi + 1) * BLOCK] = val


def _band_bias(rel_bias, bucket_t):
    return pl.pallas_call(
        _band_bias_kernel,
        in_specs=[pl.BlockSpec(memory_space=pltpu.SMEM),
                  pl.BlockSpec((3 * BLOCK, BLOCK), lambda: (0, 0))],
        out_specs=pl.BlockSpec((N_KV, 3 * BLOCK, GROUP * BLOCK), lambda: (0, 0, 0)),
        out_shape=jax.ShapeDtypeStruct((N_KV, 3 * BLOCK, GROUP * BLOCK), f32),
        name="band_bias",
    )(rel_bias, bucket_t)


def _band_attn_kernel(nb, qt_ref, kp_ref, km_ref, kn_ref, vp_ref, vm_ref, vn_ref,
                      bias_ref, sink_ref, ot_ref):
    first = (pl.program_id(0) * BAND_R) % nb == 0
    last = ((pl.program_id(0) + 1) * BAND_R) % nb == 0
    k_win = jnp.concatenate([kp_ref[...], km_ref[...], kn_ref[...]], axis=0)
    v_blocks = [vp_ref[0]] + [vm_ref[r] for r in range(BAND_R)] + [vn_ref[0]]
    ones = jnp.ones((2 * SUBLANES, 3 * BLOCK), bf16)
    zeros = jnp.zeros((HEAD_DIM, GROUP * BLOCK), bf16)
    units = [(r, kv) for r in range(BAND_R) for kv in range(N_KV)]

    def logits(r, kv):
        qg = jnp.concatenate(
            [qt_ref[h * HEAD_DIM:(h + 1) * HEAD_DIM, r * BLOCK:(r + 1) * BLOCK]
             for h in range(kv * GROUP, (kv + 1) * GROUP)], axis=1)
        q_ext = jnp.concatenate([qg, zeros] if kv == 0 else [zeros, qg], axis=0)
        return _dot(k_win[r * BLOCK:(r + 3) * BLOCK], q_ext)

    s_queue = [logits(*u) for u in units[:BAND_LOOKAHEAD]]
    for i, (r, kv) in enumerate(units):
        if i + BAND_LOOKAHEAD < len(units):
            s_queue.append(logits(*units[i + BAND_LOOKAHEAD]))
        s = s_queue[i] + bias_ref[kv]
        s_queue[i] = None
        if r == 0:
            s = jnp.concatenate([jnp.where(first, -jnp.inf, s[0:BLOCK]), s[BLOCK:]], axis=0)
        if r == BAND_R - 1:
            s = jnp.concatenate([s[:2 * BLOCK], jnp.where(last, -jnp.inf, s[2 * BLOCK:])], axis=0)
        sink = sink_ref[kv]
        m = jnp.maximum(jnp.max(s, axis=0, keepdims=True), sink)
        p = jnp.exp2(s - m).astype(bf16)
        v_band = jnp.concatenate(
            [blk[kv * HEAD_DIM:(kv + 1) * HEAD_DIM] for blk in v_blocks[r:r + 3]], axis=1)
        pv = _dot(jnp.concatenate([v_band, ones], axis=0), p)
        out = pv[0:HEAD_DIM] / (pv[HEAD_DIM:HEAD_DIM + 1] + jnp.exp2(sink - m))
        for gi in range(GROUP):
            h = kv * GROUP + gi
            ot_ref[h * HEAD_DIM:(h + 1) * HEAD_DIM, r * BLOCK:(r + 1) * BLOCK] = (
                out[:, gi * BLOCK:(gi + 1) * BLOCK].astype(bf16))


def _band_attn(qbt, kb, vbt, seq, sink_rows, bias_t):
    t = kb.shape[0]
    nb = seq // BLOCK
    nblocks = t // BLOCK
    r = BAND_R
    prev_blk = lambda i: jnp.maximum(i * r - 1, 0)
    next_blk = lambda i: jnp.minimum((i + 1) * r, nblocks - 1)
    return pl.pallas_call(
        functools.partial(_band_attn_kernel, nb),
        grid=(nblocks // r,),
        in_specs=[
            pl.BlockSpec((Q_WIDTH, r * BLOCK), lambda i: (0, i)),
            pl.BlockSpec((BLOCK, KV_WIDTH), lambda i: (prev_blk(i), 0)),
            pl.BlockSpec((r * BLOCK, KV_WIDTH), lambda i: (i, 0)),
            pl.BlockSpec((BLOCK, KV_WIDTH), lambda i: (next_blk(i), 0)),
            pl.BlockSpec((1, KV_WIDTH, BLOCK), lambda i: (prev_blk(i), 0, 0)),
            pl.BlockSpec((r, KV_WIDTH, BLOCK), lambda i: (i, 0, 0)),
            pl.BlockSpec((1, KV_WIDTH, BLOCK), lambda i: (next_blk(i), 0, 0)),
            pl.BlockSpec((N_KV, 3 * BLOCK, GROUP * BLOCK), lambda i: (0, 0, 0)),
            pl.BlockSpec((N_KV, 1, GROUP * BLOCK), lambda i: (0, 0, 0)),
        ],
        out_specs=pl.BlockSpec((Q_WIDTH, r * BLOCK), lambda i: (0, i)),
        out_shape=jax.ShapeDtypeStruct((Q_WIDTH, t), bf16),
        compiler_params=pltpu.CompilerParams(
            dimension_semantics=("parallel",), vmem_limit_bytes=VMEM_LIMIT),
        name="band_attn",
    )(qbt, kb, kb, kb, vbt, vbt, vbt, bias_t, sink_rows)


def _flash_queries(qt_ref, pad):
    tq = qt_ref.shape[1]
    zeros = jnp.zeros((HEAD_DIM, GROUP * tq), bf16)
    q_ext = []
    for kv in range(N_KV):
        qg = jnp.concatenate(
            [qt_ref[h * HEAD_DIM:(h + 1) * HEAD_DIM, :]
             for h in range(kv * GROUP, (kv + 1) * GROUP)], axis=1)
        q_ext.append(jnp.concatenate(([qg, zeros] if kv == 0 else [zeros, qg]) + [pad], axis=0))
    return q_ext


def _flash_store(ot_ref, kv, out):
    tq = ot_ref.shape[1]
    for gi in range(GROUP):
        h = kv * GROUP + gi
        ot_ref[h * HEAD_DIM:(h + 1) * HEAD_DIM, :] = out[:, gi * tq:(gi + 1) * tq].astype(bf16)


def _flash_online_kernel(qt_ref, k_ref, vt_ref, ot_ref):
    nq = GROUP * qt_ref.shape[1]
    n_chunks = vt_ref.shape[0]
    tk = vt_ref.shape[2]
    q_ext = _flash_queries(qt_ref, jnp.zeros((KV_WIDTH, nq), bf16))

    def logits(c):
        start = pl.multiple_of(c * tk, tk)
        kc = k_ref[pl.ds(start, tk), :]
        return tuple(_dot(kc, q_ext[kv]) for kv in range(N_KV))

    def body(c, carry):
        s_all, stats = carry
        s_next = logits(jnp.minimum(c + 1, n_chunks - 1))
        vt = vt_ref[c]
        out = []
        for kv in range(N_KV):
            m, l, acc = stats[kv]
            s = s_all[kv]
            m_new = jnp.maximum(m, jnp.max(s, axis=0, keepdims=True))
            alpha = jnp.exp2(m - m_new)
            p = jnp.exp2(s - m_new)
            l_new = alpha * l + jnp.sum(p, axis=0, keepdims=True)
            pv = _dot(vt[kv * HEAD_DIM:(kv + 1) * HEAD_DIM], p.astype(bf16))
            out.append((m_new, l_new, alpha * acc + pv))
        return s_next, tuple(out)

    init = tuple((jnp.full((1, nq), -jnp.inf, f32), jnp.zeros((1, nq), f32),
                  jnp.zeros((HEAD_DIM, nq), f32)) for _ in range(N_KV))
    _, final = lax.fori_loop(0, n_chunks, body, (logits(0), init))
    for kv in range(N_KV):
        _, l, acc = final[kv]
        _flash_store(ot_ref, kv, acc / l)


def _flash_bounded_kernel(qt_ref, pad_ref, k_ref, vt_ref, ot_ref):
    nq = GROUP * qt_ref.shape[1]
    n_chunks = vt_ref.shape[0]
    tk = vt_ref.shape[2]
    q_ext = _flash_queries(qt_ref, pad_ref[...])
    ones = jnp.ones((2 * SUBLANES, tk), bf16)
    rows = HEAD_DIM + 2 * SUBLANES

    units = [(c, kv) for c in range(n_chunks) for kv in range(N_KV)]

    def logits(c, kv):
        return _dot(k_ref[c * tk:(c + 1) * tk, :], q_ext[kv])

    s_queue = [logits(*u) for u in units[:FLASH_LOOKAHEAD]]
    acc = [None] * N_KV
    for i, (c, kv) in enumerate(units):
        if i + FLASH_LOOKAHEAD < len(units):
            s_queue.append(logits(*units[i + FLASH_LOOKAHEAD]))
        p = jnp.exp2(s_queue[i]).astype(bf16)
        s_queue[i] = None
        v_ext = jnp.concatenate([vt_ref[c, kv * HEAD_DIM:(kv + 1) * HEAD_DIM, :], ones], axis=0)
        pv = _dot(v_ext, p)
        acc[kv] = pv if acc[kv] is None else acc[kv] + pv
    for kv in range(N_KV):
        _flash_store(ot_ref, kv, acc[kv][0:HEAD_DIM] / acc[kv][HEAD_DIM:HEAD_DIM + 1])


def _flash_attn(qct, pad, kc, vct, seq, bounded):
    t = kc.shape[0]
    tq = TQ_FLASH
    q_tiles = seq // tq
    tk = vct.shape[2]
    q_spec = pl.BlockSpec((Q_WIDTH, tq), lambda b, i: (0, b * q_tiles + i))
    kv_specs = [
        pl.BlockSpec((seq, 2 * KV_WIDTH), lambda b, i: (b, 0)),
        pl.BlockSpec((seq // tk, KV_WIDTH, tk), lambda b, i: (b, 0, 0)),
    ]
    pad_spec = pl.BlockSpec((KV_WIDTH, GROUP * tq), lambda b, i: (0, 0))
    return pl.pallas_call(
        _flash_bounded_kernel if bounded else _flash_online_kernel,
        grid=(t // seq, q_tiles),
        in_specs=[q_spec] + ([pad_spec] if bounded else []) + kv_specs,
        out_specs=q_spec,
        out_shape=jax.ShapeDtypeStruct((Q_WIDTH, t), bf16),
        compiler_params=pltpu.CompilerParams(
            dimension_semantics=("parallel", "parallel"), vmem_limit_bytes=VMEM_LIMIT),
        name="flash_bounded" if bounded else "flash_online",
    )(*([qct] + ([pad] if bounded else []) + [kc, vct]))


def _merge_kernel(x_ref, g1_ref, oa_ref, obt_ref, oct_ref, wg_ref, bg_ref, wbr_ref, wo_ref,
                  y_ref):
    x = x_ref[...]
    xn = (x * _rms_scale(x) * g1_ref[...]).astype(bf16)
    projs = (_dot(oa_ref[...], wbr_ref[0]), _dot_tn(obt_ref[...], wbr_ref[1]),
             _dot_tn(oct_ref[...], wbr_ref[2]))
    logits = [_dot(xn, wg_ref[:, n * D_MODEL:(n + 1) * D_MODEL]) for n in range(N_BRANCH)]
    merged = None
    for n, proj in enumerate(projs):
        gate = _sigmoid(logits[n] + bg_ref[:, n * D_MODEL:(n + 1) * D_MODEL])
        term = gate * proj
        merged = term if merged is None else merged + term
    y_ref[...] = x + _dot(merged.astype(bf16), wo_ref[...])


def _merge(x2d, g1, oa, obt, oct, wg, bg, wbr, wo):
    t = x2d.shape[0]
    tm = TM_MERGE
    const = lambda i: (0, 0)
    row = lambda width: pl.BlockSpec((tm, width), lambda i: (i, 0))
    return pl.pallas_call(
        _merge_kernel,
        grid=(t // tm,),
        in_specs=[
            row(D_MODEL),
            pl.BlockSpec((1, D_MODEL), const),
            row(BR_WIDTH),
            pl.BlockSpec((BR_WIDTH, tm), lambda i: (0, i)),
            pl.BlockSpec((BR_WIDTH, tm), lambda i: (0, i)),
            pl.BlockSpec((D_MODEL, N_BRANCH * D_MODEL), const),
            pl.BlockSpec((1, N_BRANCH * D_MODEL), const),
            pl.BlockSpec((N_BRANCH, BR_WIDTH, D_MODEL), lambda i: (0, 0, 0)),
            pl.BlockSpec((D_MODEL, D_MODEL), const),
        ],
        out_specs=row(D_MODEL),
        out_shape=jax.ShapeDtypeStruct((t, D_MODEL), f32),
        compiler_params=pltpu.CompilerParams(
            dimension_semantics=("parallel",), vmem_limit_bytes=VMEM_LIMIT),
        name="merge",
    )(x2d, g1, oa, obt, oct, wg, bg, wbr, wo)


def _conv_ffn_kernel(tiles_per_seq, apply_final, xp_ref, x_ref, xq_ref, g2_ref, wup_ref,
                     cw_ref, cb_ref, wdn_ref, gf_ref, y_ref, act_ref):
    tm = x_ref.shape[0]
    i = pl.program_id(0) % tiles_per_seq
    x = x_ref[...]
    halo_p = jnp.where(i > 0, xp_ref[...], 0.0)
    halo_n = jnp.where(i < tiles_per_seq - 1, xq_ref[...], 0.0)
    xe = jnp.concatenate([halo_p, x, halo_n], axis=0)
    xn = (xe * _rms_scale(xe) * g2_ref[...]).astype(bf16)
    rows = tm + 2 * SUBLANES
    n_chunks = D_FF // FF_CHUNK
    chunk_cols = lambda c: (slice(c * FF_CHUNK, (c + 1) * FF_CHUNK),
                            slice(D_FF + c * FF_CHUNK, D_FF + (c + 1) * FF_CHUNK))

    def up(c):
        return tuple(_dot(xn, wup_ref[:, cols]) for cols in chunk_cols(c))

    h_queue = [up(c) for c in range(FFN_LOOKAHEAD)]
    for c in range(n_chunks):
        if c + FFN_LOOKAHEAD < n_chunks:
            h_queue.append(up(c + FFN_LOOKAHEAD))
        parts = []
        for h, cols in zip(h_queue[c], chunk_cols(c)):
            hc = (pltpu.roll(h, 1, 0) * cw_ref[0:1, cols]
                  + h * cw_ref[1:2, cols]
                  + pltpu.roll(h, rows - 1, 0) * cw_ref[2:3, cols]
                  + cb_ref[:, cols])
            parts.append(hc[SUBLANES:SUBLANES + tm])
        h_queue[c] = None
        act_ref[:, chunk_cols(c)[0]] = (parts[0] * _sigmoid(parts[0]) * parts[1]).astype(bf16)
    y = x + _dot(act_ref[...], wdn_ref[...])
    if apply_final:
        y = y * _rms_scale(y) * gf_ref[...]
    y_ref[...] = y


def _conv_ffn(x2d, seq, g2, wup, cw, cb, wdn, gf, apply_final):
    t = x2d.shape[0]
    tm = TM_FFN
    tiles_per_seq = seq // tm
    halo_per_tile = tm // SUBLANES
    n_halo = t // SUBLANES
    const = lambda i: (0, 0)
    return pl.pallas_call(
        functools.partial(_conv_ffn_kernel, tiles_per_seq, apply_final),
        grid=(t // tm,),
        in_specs=[
            pl.BlockSpec((SUBLANES, D_MODEL),
                         lambda i: (jnp.maximum(i * halo_per_tile - 1, 0), 0)),
            pl.BlockSpec((tm, D_MODEL), lambda i: (i, 0)),
            pl.BlockSpec((SUBLANES, D_MODEL),
                         lambda i: (jnp.minimum((i + 1) * halo_per_tile, n_halo - 1), 0)),
            pl.BlockSpec((1, D_MODEL), const),
            pl.BlockSpec((D_MODEL, 2 * D_FF), const),
            pl.BlockSpec((3, 2 * D_FF), const),
            pl.BlockSpec((1, 2 * D_FF), const),
            pl.BlockSpec((D_FF, D_MODEL), const),
            pl.BlockSpec((1, D_MODEL), const),
        ],
        out_specs=pl.BlockSpec((tm, D_MODEL), lambda i: (i, 0)),
        out_shape=jax.ShapeDtypeStruct((t, D_MODEL), f32),
        scratch_shapes=[pltpu.VMEM((tm, D_FF), bf16)],
        compiler_params=pltpu.CompilerParams(
            dimension_semantics=("parallel",), vmem_limit_bytes=VMEM_LIMIT),
        name="conv_ffn",
    )(x2d, x2d, x2d, g2, wup, cw, cb, wdn, gf)


def _t5_bucket(rel):
    half = N_BUCKETS // 2
    max_exact = half // 2
    ret = jnp.where(rel > 0, half, 0)
    n = jnp.abs(rel)
    nf = jnp.maximum(n, 1).astype(f32)
    large = max_exact + (jnp.log(nf / max_exact) / math.log(MAX_DIST / max_exact)
                         * (half - max_exact)).astype(jnp.int32)
    large = jnp.minimum(large, half - 1)
    return ret + jnp.where(n < max_exact, n, large)


def _band_buckets():
    jpos = jnp.arange(3 * BLOCK)[:, None]
    qpos = jnp.arange(BLOCK)[None, :]
    rel = jpos - BLOCK - qpos
    return jnp.where(jnp.abs(rel) <= WINDOW, _t5_bucket(rel), -1).astype(jnp.int32)


def _rope_tables(seq):
    m = HEAD_DIM // 4
    pos = jnp.arange(seq)
    row = (pos // GRID_W).astype(f32)
    col = (pos % GRID_W).astype(f32)
    inv = ROPE_THETA ** (-jnp.arange(m, dtype=f32) / m)
    ang_r = row[:, None] * inv[None, :]
    ang_c = col[:, None] * inv[None, :]
    cos = jnp.concatenate([jnp.cos(ang_r), jnp.cos(ang_r), jnp.cos(ang_c), jnp.cos(ang_c)], axis=-1)
    sin = jnp.concatenate([-jnp.sin(ang_r), jnp.sin(ang_r), -jnp.sin(ang_c), jnp.sin(ang_c)], axis=-1)
    reps = LANES // HEAD_DIM
    return jnp.tile(cos, (1, reps)), jnp.tile(sin, (1, reps)), cos.T, sin.T


def _trunk(x, layers, bias, seg, final_g):
    bsz, seq, d = x.shape
    x2d = x.reshape(bsz * seq, d)
    cos_t, sin_t, cos_tt, sin_tt = _rope_tables(seq)
    for l, p in enumerate(layers):
        oa, kb, kc, qbt, qct, vbt, vct = _in_proj(
            x2d, seq, p["g1"], p["w_in"], p["wqt"], p["wvt"], p["lng"], p["lnb"], p["wsp"],
            p["bsp"], p["qgt"], p["kg"], cos_t, sin_t, cos_tt, sin_tt, seg)
        obt = _band_attn(qbt, kb, vbt, seq, p["sink"], bias)
        oct = lax.cond(
            p["logit_bound"] <= MAX_LOGIT_BOUND,
            lambda qct, pad, kc, vct: _flash_attn(qct, pad, kc, vct, seq, bounded=True),
            lambda qct, pad, kc, vct: _flash_attn(qct, pad, kc, vct, seq, bounded=False),
            qct, p["pad"], kc, vct)
        x2d = _merge(x2d, p["g1"], oa, obt, oct, p["wg"], p["bg"], p["wbr"], p["wo"])
        x2d = _conv_ffn(x2d, seq, p["g2"], p["wup"], p["cw"], p["cb"], p["wdn"], final_g,
                        apply_final=(l == len(layers) - 1))
    return x2d.reshape(bsz, seq, d)


def kernel(x_prompt, x_sample, rel_bias, norm1_g, w_in, ln_v_g, ln_v_b, w_spatial, b_spatial,
           sink, q_norm_g, k_norm_g, w_gate, b_gate, w_branch, w_out, norm2_g, w_up, conv_w,
           conv_b, w_down, final_g):
    bias = _band_bias(rel_bias, _band_buckets())
    head_of_lane = jnp.arange(LANES) // HEAD_DIM
    seg = ((head_of_lane[:, None] == head_of_lane[None, :]).astype(f32) / HEAD_DIM).astype(bf16)
    reps = LANES // HEAD_DIM
    layers = []
    for l in range(DEPTH):
        ws = w_spatial[l].astype(bf16)
        wsp = jnp.concatenate([ws[0::2], ws[1::2]], axis=-1)
        wl = w_in[l].astype(bf16)
        b_q, b_k, b_v = A_IN, A_IN + Q_WIDTH, A_IN + Q_WIDTH + KV_WIDTH
        c_q, c_k, c_v = (b + QKV_WIDTH for b in (b_q, b_k, b_v))
        logit_bound = (1.02 * HEAD_DIM ** 0.5) * jnp.max(jnp.abs(q_norm_g[l])) * jnp.max(
            jnp.abs(k_norm_g[l]))
        shift = -(logit_bound * (LOG2E * (1.0 + 2.0 ** -7))).astype(bf16)
        pad = jnp.zeros((KV_WIDTH, GROUP * TQ_FLASH), bf16).at[0, :].set(shift)
        layers.append(dict(
            logit_bound=logit_bound,
            pad=pad,
            g1=norm1_g[l][None, :],
            w_in=jnp.concatenate(
                [wl[:, :A_IN], wl[:, b_k:b_k + KV_WIDTH], wl[:, c_k:c_k + KV_WIDTH]], axis=1),
            wqt=jnp.concatenate([wl[:, b_q:b_q + Q_WIDTH], wl[:, c_q:c_q + Q_WIDTH]], axis=1).T,
            wvt=jnp.concatenate([wl[:, b_v:b_v + KV_WIDTH], wl[:, c_v:c_v + KV_WIDTH]], axis=1).T,
            lng=ln_v_g[l][None, :],
            lnb=ln_v_b[l][None, :],
            wsp=wsp,
            bsp=jnp.repeat(b_spatial[l].T, HEAD_DIM, axis=1),
            qgt=jnp.broadcast_to(q_norm_g[l][:, None], (HEAD_DIM, TM_IN)),
            kg=jnp.tile(k_norm_g[l], reps)[None, :],
            sink=jnp.repeat(sink[l] * LOG2E, BLOCK).reshape(N_KV, 1, GROUP * BLOCK),
            wg=w_gate[l].astype(bf16),
            bg=b_gate[l][None, :],
            wbr=w_branch[l].astype(bf16),
            wo=w_out[l].astype(bf16),
            g2=norm2_g[l][None, :],
            wup=w_up[l].astype(bf16),
            cw=conv_w[l],
            cb=conv_b[l][None, :],
            wdn=w_down[l].astype(bf16),
        ))
    gf = final_g[None, :]
    y_prompt = _trunk(x_prompt, layers, bias, seg, gf)
    y_sample = _trunk(x_sample, layers, bias, seg, gf)
    return (y_prompt, y_sample)
```

```python
import functools
import math

import jax
import jax.numpy as jnp
from jax import lax
from jax.experimental import pallas as pl
from jax.experimental.pallas import tpu as pltpu

D_MODEL = 1024
DEPTH = 2
HEAD_DIM = 64
BLOCK = 128
A_GROUPS = 8
A_WIDTH = A_GROUPS * HEAD_DIM
N_HEADS = 8
N_KV = 2
GROUP = N_HEADS // N_KV
WINDOW = 128
ROPE_THETA = 10000.0
GRID_W = 64
N_BUCKETS = 32
MAX_DIST = 128
D_FF = 2816
EPS = 1e-6
N_BRANCH = 3
BR_WIDTH = 512
Q_WIDTH = N_HEADS * HEAD_DIM
KV_WIDTH = N_KV * HEAD_DIM
QKV_WIDTH = Q_WIDTH + 2 * KV_WIDTH
A_IN = 2 * A_WIDTH

LANES = 128
SUBLANES = 8
VMEM_LIMIT = 56 * 1024 * 1024

TM_IN = 512
TM_MERGE = 512
TM_FFN = 512
FF_CHUNK = 256
FFN_LOOKAHEAD = 2
BAND_R = 4
BAND_LOOKAHEAD = 2
TQ_FLASH = 128
TK_FLASH = 256
FLASH_LOOKAHEAD = 2
LOG2E = math.log2(math.e)
MAX_LOGIT_BOUND = 30.0

f32 = jnp.float32
bf16 = jnp.bfloat16


def _resident(shape, layer=None):
    if layer is None:
        return pl.BlockSpec(shape, lambda *_: (0,) * len(shape), pipeline_mode=pl.Buffered(1))
    return pl.BlockSpec((None,) + tuple(shape), lambda *_: (layer,) + (0,) * len(shape),
                        pipeline_mode=pl.Buffered(1))


def _rms_scale(x):
    return lax.rsqrt(jnp.mean(x * x, axis=-1, keepdims=True) + EPS)


def _gelu_tanh(x):
    c = math.sqrt(2.0 / math.pi)
    return x * (0.5 * (1.0 + jnp.tanh(c * (x + 0.044715 * (x * x * x)))))


def _sigmoid(x):
    return 1.0 / (1.0 + jnp.exp(-x))


def _dot(a, b):
    return jnp.dot(a, b, preferred_element_type=f32)


def _dot_tn(a, b):
    return lax.dot_general(a, b, (((0,), (0,)), ((), ())), preferred_element_type=f32)


def _dot_nt(a, b):
    return lax.dot_general(a, b, (((1,), (1,)), ((), ())), preferred_element_type=f32)


def _head_mean_sq(x, seg_ref):
    sq = x * x
    hi = sq.astype(bf16)
    lo = (sq - hi.astype(f32)).astype(bf16)
    seg = seg_ref[...]
    return _dot(hi, seg) + _dot(lo, seg)


def _swap16(x):
    lane = lax.broadcasted_iota(jnp.int32, x.shape, 1)
    first_half = (lane % 32) < 16
    return jnp.where(first_half, pltpu.roll(x, LANES - 16, 1), pltpu.roll(x, 16, 1))


def _swap16_rows(x):
    h = HEAD_DIM // 4
    return jnp.concatenate([x[h:2 * h], x[0:h], x[3 * h:4 * h], x[2 * h:3 * h]], axis=0)


def _in_proj_kernel(x_ref, g1_ref, w_ref, wqt_ref, wvt_ref, lng_ref, lnb_ref, wsp_ref, bsp_ref,
                    qgt_ref, kg_ref, cos_ref, sin_ref, cost_ref, sint_ref, seg_ref,
                    oa_ref, kb_ref, kc_ref, qbt_ref, qct_ref, vbt_ref, vct_ref):
    tm = x_ref.shape[0]
    x = x_ref[...]
    xn = (x * _rms_scale(x) * g1_ref[...]).astype(bf16)

    zv = _dot(xn, w_ref[:, A_WIDTH:A_IN])
    zu = _dot(xn, w_ref[:, 0:A_WIDTH])
    zk = _dot(xn, w_ref[:, A_IN:A_IN + 2 * KV_WIDTH])
    qbt = _dot_nt(wqt_ref[0:Q_WIDTH, :], xn)
    qt = _dot_nt(wqt_ref[Q_WIDTH:2 * Q_WIDTH, :], xn)
    vt = _dot_nt(wvt_ref[...], xn).astype(bf16)

    v = _gelu_tanh(zv)
    u = _gelu_tanh(zu)
    mu = jnp.mean(v, axis=-1, keepdims=True)
    vc = v - mu
    vn = vc * lax.rsqrt(jnp.mean(vc * vc, axis=-1, keepdims=True) + EPS)
    vn = vn * lng_ref[...] + lnb_ref[...]
    lane = lax.broadcasted_iota(jnp.int32, (BLOCK, LANES), 1)
    low_half = lane < HEAD_DIM
    for c in range(tm // BLOCK):
        rows = slice(c * BLOCK, (c + 1) * BLOCK)
        for j in range(A_WIDTH // LANES):
            cols = slice(j * LANES, (j + 1) * LANES)
            vp = vn[rows, cols]
            stacked = jnp.concatenate(
                [jnp.where(low_half, vp, 0.0), jnp.where(low_half, 0.0, vp)], axis=0)
            sv = _dot(wsp_ref[j], stacked.astype(bf16)) + bsp_ref[:, cols]
            oa_ref[rows, cols] = (u[rows, cols] * sv).astype(bf16)

    kb_ref[...] = zk[:, 0:KV_WIDTH].astype(bf16)

    t = zk[:, KV_WIDTH:2 * KV_WIDTH]
    tn = t * lax.rsqrt(_head_mean_sq(t, seg_ref) + EPS) * kg_ref[...]
    kc_ref[:, 0:KV_WIDTH] = (tn * cos_ref[...] + _swap16(tn) * sin_ref[...]).astype(bf16)
    lane = lax.broadcasted_iota(jnp.int32, (tm, KV_WIDTH), 1)
    kc_ref[:, KV_WIDTH:2 * KV_WIDTH] = jnp.where(lane == 0, 1.0, 0.0).astype(bf16)

    scale = HEAD_DIM ** -0.5 * LOG2E
    qbt_ref[...] = (qbt * scale).astype(bf16)
    cost = cost_ref[...]
    sint = sint_ref[...]
    for h in range(N_HEADS):
        rows = slice(h * HEAD_DIM, (h + 1) * HEAD_DIM)
        th = qt[rows]
        r = lax.rsqrt(jnp.mean(th * th, axis=0, keepdims=True) + EPS)
        tn = th * r * qgt_ref[...]
        qct_ref[rows, :] = ((tn * cost + _swap16_rows(tn) * sint) * scale).astype(bf16)
    for c in range(tm // BLOCK):
        vbt_ref[c] = vt[0:KV_WIDTH, c * BLOCK:(c + 1) * BLOCK]
    for c in range(tm // TK_FLASH):
        vct_ref[c] = vt[KV_WIDTH:2 * KV_WIDTH, c * TK_FLASH:(c + 1) * TK_FLASH]


def _in_proj(x2d, seq, g1, w_in, wqt, wvt, lng, lnb, wsp, bsp, qgt, kg, cos_t, sin_t,
             cos_tt, sin_tt, seg):
    t = x2d.shape[0]
    tm = TM_IN
    tiles_per_seq = seq // tm
    const = lambda i: (0, 0)
    return pl.pallas_call(
        _in_proj_kernel,
        grid=(t // tm,),
        in_specs=[
            pl.BlockSpec((tm, D_MODEL), lambda i: (i, 0)),
            pl.BlockSpec((1, D_MODEL), const),
            _resident((D_MODEL, w_in.shape[1])),
            _resident((2 * Q_WIDTH, D_MODEL)),
            _resident((2 * KV_WIDTH, D_MODEL)),
            pl.BlockSpec((1, A_WIDTH), const),
            pl.BlockSpec((1, A_WIDTH), const),
            pl.BlockSpec((A_WIDTH // LANES, BLOCK, 2 * BLOCK), lambda i: (0, 0, 0)),
            pl.BlockSpec((BLOCK, A_WIDTH), const),
            pl.BlockSpec((HEAD_DIM, tm), const),
            pl.BlockSpec((1, LANES), const),
            pl.BlockSpec((tm, LANES), lambda i: (i % tiles_per_seq, 0)),
            pl.BlockSpec((tm, LANES), lambda i: (i % tiles_per_seq, 0)),
            pl.BlockSpec((HEAD_DIM, tm), lambda i: (0, i % tiles_per_seq)),
            pl.BlockSpec((HEAD_DIM, tm), lambda i: (0, i % tiles_per_seq)),
            pl.BlockSpec((LANES, LANES), const),
        ],
        out_specs=[
            pl.BlockSpec((tm, A_WIDTH), lambda i: (i, 0)),
            pl.BlockSpec((tm, KV_WIDTH), lambda i: (i, 0)),
            pl.BlockSpec((tm, 2 * KV_WIDTH), lambda i: (i, 0)),
            pl.BlockSpec((Q_WIDTH, tm), lambda i: (0, i)),
            pl.BlockSpec((Q_WIDTH, tm), lambda i: (0, i)),
            pl.BlockSpec((tm // BLOCK, KV_WIDTH, BLOCK), lambda i: (i, 0, 0)),
            pl.BlockSpec((tm // TK_FLASH, KV_WIDTH, TK_FLASH), lambda i: (i, 0, 0)),
        ],
        out_shape=[
            jax.ShapeDtypeStruct((t, A_WIDTH), bf16),
            jax.ShapeDtypeStruct((t, KV_WIDTH), bf16),
            jax.ShapeDtypeStruct((t, 2 * KV_WIDTH), bf16),
            jax.ShapeDtypeStruct((Q_WIDTH, t), bf16),
            jax.ShapeDtypeStruct((Q_WIDTH, t), bf16),
            jax.ShapeDtypeStruct((t // BLOCK, KV_WIDTH, BLOCK), bf16),
            jax.ShapeDtypeStruct((t // TK_FLASH, KV_WIDTH, TK_FLASH), bf16),
        ],
        compiler_params=pltpu.CompilerParams(
            dimension_semantics=("parallel",), vmem_limit_bytes=VMEM_LIMIT),
        name="in_proj",
    )(x2d, g1, w_in, wqt, wvt, lng, lnb, wsp, bsp, qgt, kg, cos_t, sin_t, cos_tt, sin_tt, seg)


def _band_bias_kernel(rel_bias_ref, bucket_ref, bias_ref):
    bucket = bucket_ref[...]
    for h in range(N_HEADS):
        val = jnp.full(bucket.shape, -jnp.inf, f32)
        for b in range(N_BUCKETS):
            val = jnp.where(bucket == b, rel_bias_ref[b, h] * LOG2E, val)
        gi = h % GROUP
        bias_ref[h // GROUP, :, gi * BLOCK:(gi + 1) * BLOCK] = val


def _band_bias(rel_bias, bucket_t):
    return pl.pallas_call(
        _band_bias_kernel,
        in_specs=[pl.BlockSpec(memory_space=pltpu.SMEM),
                  pl.BlockSpec((3 * BLOCK, BLOCK), lambda: (0, 0))],
        out_specs=pl.BlockSpec((N_KV, 3 * BLOCK, GROUP * BLOCK), lambda: (0, 0, 0)),
        out_shape=jax.ShapeDtypeStruct((N_KV, 3 * BLOCK, GROUP * BLOCK), f32),
        name="band_bias",
    )(rel_bias, bucket_t)


def _band_attn_kernel(nb, qt_ref, kp_ref, km_ref, kn_ref, vp_ref, vm_ref, vn_ref,
                      bias_ref, sink_ref, ot_ref):
    first = (pl.program_id(0) * BAND_R) % nb == 0
    last = ((pl.program_id(0) + 1) * BAND_R) % nb == 0
    k_win = jnp.concatenate([kp_ref[...], km_ref[...], kn_ref[...]], axis=0)
    v_blocks = [vp_ref[0]] + [vm_ref[r] for r in range(BAND_R)] + [vn_ref[0]]
    ones = jnp.ones((2 * SUBLANES, 3 * BLOCK), bf16)
    zeros = jnp.zeros((HEAD_DIM, GROUP * BLOCK), bf16)
    units = [(r, kv) for r in range(BAND_R) for kv in range(N_KV)]

    def logits(r, kv):
        qg = jnp.concatenate(
            [qt_ref[h * HEAD_DIM:(h + 1) * HEAD_DIM, r * BLOCK:(r + 1) * BLOCK]
             for h in range(kv * GROUP, (kv + 1) * GROUP)], axis=1)
        q_ext = jnp.concatenate([qg, zeros] if kv == 0 else [zeros, qg], axis=0)
        return _dot(k_win[r * BLOCK:(r + 3) * BLOCK], q_ext)

    s_queue = [logits(*u) for u in units[:BAND_LOOKAHEAD]]
    for i, (r, kv) in enumerate(units):
        if i + BAND_LOOKAHEAD < len(units):
            s_queue.append(logits(*units[i + BAND_LOOKAHEAD]))
        s = s_queue[i] + bias_ref[kv]
        s_queue[i] = None
        if r == 0:
            s = jnp.concatenate([jnp.where(first, -jnp.inf, s[0:BLOCK]), s[BLOCK:]], axis=0)
        if r == BAND_R - 1:
            s = jnp.concatenate([s[:2 * BLOCK], jnp.where(last, -jnp.inf, s[2 * BLOCK:])], axis=0)
        sink = sink_ref[kv]
        m = jnp.maximum(jnp.max(s, axis=0, keepdims=True), sink)
        p = jnp.exp2(s - m).astype(bf16)
        v_band = jnp.concatenate(
            [blk[kv * HEAD_DIM:(kv + 1) * HEAD_DIM] for blk in v_blocks[r:r + 3]], axis=1)
        pv = _dot(jnp.concatenate([v_band, ones], axis=0), p)
        out = pv[0:HEAD_DIM] / (pv[HEAD_DIM:HEAD_DIM + 1] + jnp.exp2(sink - m))
        for gi in range(GROUP):
            h = kv * GROUP + gi
            ot_ref[h * HEAD_DIM:(h + 1) * HEAD_DIM, r * BLOCK:(r + 1) * BLOCK] = (
                out[:, gi * BLOCK:(gi + 1) * BLOCK].astype(bf16))


def _band_attn(qbt, kb, vbt, seq, sink_rows, bias_t):
    t = kb.shape[0]
    nb = seq // BLOCK
    nblocks = t // BLOCK
    r = BAND_R
    prev_blk = lambda i: jnp.maximum(i * r - 1, 0)
    next_blk = lambda i: jnp.minimum((i + 1) * r, nblocks - 1)
    return pl.pallas_call(
        functools.partial(_band_attn_kernel, nb),
        grid=(nblocks // r,),
        in_specs=[
            pl.BlockSpec((Q_WIDTH, r * BLOCK), lambda i: (0, i)),
            pl.BlockSpec((BLOCK, KV_WIDTH), lambda i: (prev_blk(i), 0)),
            pl.BlockSpec((r * BLOCK, KV_WIDTH), lambda i: (i, 0)),
            pl.BlockSpec((BLOCK, KV_WIDTH), lambda i: (next_blk(i), 0)),
            pl.BlockSpec((1, KV_WIDTH, BLOCK), lambda i: (prev_blk(i), 0, 0)),
            pl.BlockSpec((r, KV_WIDTH, BLOCK), lambda i: (i, 0, 0)),
            pl.BlockSpec((1, KV_WIDTH, BLOCK), lambda i: (next_blk(i), 0, 0)),
            pl.BlockSpec((N_KV, 3 * BLOCK, GROUP * BLOCK), lambda i: (0, 0, 0)),
            pl.BlockSpec((N_KV, 1, GROUP * BLOCK), lambda i: (0, 0, 0)),
        ],
        out_specs=pl.BlockSpec((Q_WIDTH, r * BLOCK), lambda i: (0, i)),
        out_shape=jax.ShapeDtypeStruct((Q_WIDTH, t), bf16),
        compiler_params=pltpu.CompilerParams(
            dimension_semantics=("parallel",), vmem_limit_bytes=VMEM_LIMIT),
        name="band_attn",
    )(qbt, kb, kb, kb, vbt, vbt, vbt, bias_t, sink_rows)


def _flash_queries(qt_ref, pad):
    tq = qt_ref.shape[1]
    zeros = jnp.zeros((HEAD_DIM, GROUP * tq), bf16)
    q_ext = []
    for kv in range(N_KV):
        qg = jnp.concatenate(
            [qt_ref[h * HEAD_DIM:(h + 1) * HEAD_DIM, :]
             for h in range(kv * GROUP, (kv + 1) * GROUP)], axis=1)
        q_ext.append(jnp.concatenate(([qg, zeros] if kv == 0 else [zeros, qg]) + [pad], axis=0))
    return q_ext


def _flash_store(ot_ref, kv, out):
    tq = ot_ref.shape[1]
    for gi in range(GROUP):
        h = kv * GROUP + gi
        ot_ref[h * HEAD_DIM:(h + 1) * HEAD_DIM, :] = out[:, gi * tq:(gi + 1) * tq].astype(bf16)


def _flash_online_kernel(qt_ref, k_ref, vt_ref, ot_ref):
    nq = GROUP * qt_ref.shape[1]
    n_chunks = vt_ref.shape[0]
    tk = vt_ref.shape[2]
    q_ext = _flash_queries(qt_ref, jnp.zeros((KV_WIDTH, nq), bf16))

    def logits(c):
        start = pl.multiple_of(c * tk, tk)
        kc = k_ref[pl.ds(start, tk), :]
        return tuple(_dot(kc, q_ext[kv]) for kv in range(N_KV))

    def body(c, carry):
        s_all, stats = carry
        s_next = logits(jnp.minimum(c + 1, n_chunks - 1))
        vt = vt_ref[c]
        out = []
        for kv in range(N_KV):
            m, l, acc = stats[kv]
            s = s_all[kv]
            m_new = jnp.maximum(m, jnp.max(s, axis=0, keepdims=True))
            alpha = jnp.exp2(m - m_new)
            p = jnp.exp2(s - m_new)
            l_new = alpha * l + jnp.sum(p, axis=0, keepdims=True)
            pv = _dot(vt[kv * HEAD_DIM:(kv + 1) * HEAD_DIM], p.astype(bf16))
            out.append((m_new, l_new, alpha * acc + pv))
        return s_next, tuple(out)

    init = tuple((jnp.full((1, nq), -jnp.inf, f32), jnp.zeros((1, nq), f32),
                  jnp.zeros((HEAD_DIM, nq), f32)) for _ in range(N_KV))
    _, final = lax.fori_loop(0, n_chunks, body, (logits(0), init))
    for kv in range(N_KV):
        _, l, acc = final[kv]
        _flash_store(ot_ref, kv, acc / l)


def _flash_bounded_kernel(qt_ref, pad_ref, k_ref, vt_ref, ot_ref):
    nq = GROUP * qt_ref.shape[1]
    n_chunks = vt_ref.shape[0]
    tk = vt_ref.shape[2]
    q_ext = _flash_queries(qt_ref, pad_ref[...])
    ones = jnp.ones((2 * SUBLANES, tk), bf16)
    rows = HEAD_DIM + 2 * SUBLANES

    units = [(c, kv) for c in range(n_chunks) for kv in range(N_KV)]

    def logits(c, kv):
        return _dot(k_ref[c * tk:(c + 1) * tk, :], q_ext[kv])

    s_queue = [logits(*u) for u in units[:FLASH_LOOKAHEAD]]
    acc = [None] * N_KV
    for i, (c, kv) in enumerate(units):
        if i + FLASH_LOOKAHEAD < len(units):
            s_queue.append(logits(*units[i + FLASH_LOOKAHEAD]))
        p = jnp.exp2(s_queue[i]).astype(bf16)
        s_queue[i] = None
        v_ext = jnp.concatenate([vt_ref[c, kv * HEAD_DIM:(kv + 1) * HEAD_DIM, :], ones], axis=0)
        pv = _dot(v_ext, p)
        acc[kv] = pv if acc[kv] is None else acc[kv] + pv
    for kv in range(N_KV):
        _flash_store(ot_ref, kv, acc[kv][0:HEAD_DIM] / acc[kv][HEAD_DIM:HEAD_DIM + 1])


def _flash_attn(qct, pad, kc, vct, seq, bounded):
    t = kc.shape[0]
    tq = TQ_FLASH
    q_tiles = seq // tq
    tk = vct.shape[2]
    q_spec = pl.BlockSpec((Q_WIDTH, tq), lambda b, i: (0, b * q_tiles + i))
    kv_specs = [
        pl.BlockSpec((seq, 2 * KV_WIDTH), lambda b, i: (b, 0)),
        pl.BlockSpec((seq // tk, KV_WIDTH, tk), lambda b, i: (b, 0, 0)),
    ]
    pad_spec = pl.BlockSpec((KV_WIDTH, GROUP * tq), lambda b, i: (0, 0))
    return pl.pallas_call(
        _flash_bounded_kernel if bounded else _flash_online_kernel,
        grid=(t // seq, q_tiles),
        in_specs=[q_spec] + ([pad_spec] if bounded else []) + kv_specs,
        out_specs=q_spec,
        out_shape=jax.ShapeDtypeStruct((Q_WIDTH, t), bf16),
        compiler_params=pltpu.CompilerParams(
            dimension_semantics=("parallel", "parallel"), vmem_limit_bytes=VMEM_LIMIT),
        name="flash_bounded" if bounded else "flash_online",
    )(*([qct] + ([pad] if bounded else []) + [kc, vct]))


def _merge_kernel(x_ref, g1_ref, oa_ref, obt_ref, oct_ref, wg_ref, bg_ref, wbr_ref, wo_ref,
                  y_ref):
    x = x_ref[...]
    xn = (x * _rms_scale(x) * g1_ref[...]).astype(bf16)
    projs = (_dot(oa_ref[...], wbr_ref[0]), _dot_tn(obt_ref[...], wbr_ref[1]),
             _dot_tn(oct_ref[...], wbr_ref[2]))
    logits = [_dot(xn, wg_ref[:, n * D_MODEL:(n + 1) * D_MODEL]) for n in range(N_BRANCH)]
    merged = None
    for n, proj in enumerate(projs):
        gate = _sigmoid(logits[n] + bg_ref[:, n * D_MODEL:(n + 1) * D_MODEL])
        term = gate * proj
        merged = term if merged is None else merged + term
    y_ref[...] = x + _dot(merged.astype(bf16), wo_ref[...])


def _merge(x2d, g1, oa, obt, oct, wg, bg, wbr, wo, layer):
    t = x2d.shape[0]
    tm = TM_MERGE
    const = lambda i: (0, 0)
    row = lambda width: pl.BlockSpec((tm, width), lambda i: (i, 0))
    return pl.pallas_call(
        _merge_kernel,
        grid=(t // tm,),
        in_specs=[
            row(D_MODEL),
            pl.BlockSpec((1, D_MODEL), const),
            row(BR_WIDTH),
            pl.BlockSpec((BR_WIDTH, tm), lambda i: (0, i)),
            pl.BlockSpec((BR_WIDTH, tm), lambda i: (0, i)),
            _resident((D_MODEL, N_BRANCH * D_MODEL), layer),
            pl.BlockSpec((1, N_BRANCH * D_MODEL), const),
            _resident((N_BRANCH, BR_WIDTH, D_MODEL), layer),
            _resident((D_MODEL, D_MODEL), layer),
        ],
        out_specs=row(D_MODEL),
        out_shape=jax.ShapeDtypeStruct((t, D_MODEL), f32),
        compiler_params=pltpu.CompilerParams(
            dimension_semantics=("parallel",), vmem_limit_bytes=VMEM_LIMIT),
        name="merge",
    )(x2d, g1, oa, obt, oct, wg, bg, wbr, wo)


def _conv_ffn_kernel(tiles_per_seq, apply_final, xp_ref, x_ref, xq_ref, g2_ref, wup_ref,
                     cw_ref, cb_ref, wdn_ref, gf_ref, y_ref, act_ref):
    tm = x_ref.shape[0]
    i = pl.program_id(0) % tiles_per_seq
    x = x_ref[...]
    halo_p = jnp.where(i > 0, xp_ref[...], 0.0)
    halo_n = jnp.where(i < tiles_per_seq - 1, xq_ref[...], 0.0)
    xe = jnp.concatenate([halo_p, x, halo_n], axis=0)
    xn = (xe * _rms_scale(xe) * g2_ref[...]).astype(bf16)
    rows = tm + 2 * SUBLANES
    n_chunks = D_FF // FF_CHUNK
    chunk_cols = lambda c: (slice(c * FF_CHUNK, (c + 1) * FF_CHUNK),
                            slice(D_FF + c * FF_CHUNK, D_FF + (c + 1) * FF_CHUNK))

    def up(c):
        return tuple(_dot(xn, wup_ref[:, cols]) for cols in chunk_cols(c))

    h_queue = [up(c) for c in range(FFN_LOOKAHEAD)]
    for c in range(n_chunks):
        if c + FFN_LOOKAHEAD < n_chunks:
            h_queue.append(up(c + FFN_LOOKAHEAD))
        parts = []
        for h, cols in zip(h_queue[c], chunk_cols(c)):
            hc = (pltpu.roll(h, 1, 0) * cw_ref[0:1, cols]
                  + h * cw_ref[1:2, cols]
                  + pltpu.roll(h, rows - 1, 0) * cw_ref[2:3, cols]
                  + cb_ref[:, cols])
            parts.append(hc[SUBLANES:SUBLANES + tm])
        h_queue[c] = None
        act_ref[:, chunk_cols(c)[0]] = (parts[0] * _sigmoid(parts[0]) * parts[1]).astype(bf16)
    y = x + _dot(act_ref[...], wdn_ref[...])
    if apply_final:
        y = y * _rms_scale(y) * gf_ref[...]
    y_ref[...] = y


def _conv_ffn(x2d, seq, g2, wup, cw, cb, wdn, gf, layer, apply_final):
    t = x2d.shape[0]
    tm = TM_FFN
    tiles_per_seq = seq // tm
    halo_per_tile = tm // SUBLANES
    n_halo = t // SUBLANES
    const = lambda i: (0, 0)
    return pl.pallas_call(
        functools.partial(_conv_ffn_kernel, tiles_per_seq, apply_final),
        grid=(t // tm,),
        in_specs=[
            pl.BlockSpec((SUBLANES, D_MODEL),
                         lambda i: (jnp.maximum(i * halo_per_tile - 1, 0), 0)),
            pl.BlockSpec((tm, D_MODEL), lambda i: (i, 0)),
            pl.BlockSpec((SUBLANES, D_MODEL),
                         lambda i: (jnp.minimum((i + 1) * halo_per_tile, n_halo - 1), 0)),
            pl.BlockSpec((1, D_MODEL), const),
            _resident((D_MODEL, 2 * D_FF), layer),
            pl.BlockSpec((3, 2 * D_FF), const),
            pl.BlockSpec((1, 2 * D_FF), const),
            _resident((D_FF, D_MODEL), layer),
            pl.BlockSpec((1, D_MODEL), const),
        ],
        out_specs=pl.BlockSpec((tm, D_MODEL), lambda i: (i, 0)),
        out_shape=jax.ShapeDtypeStruct((t, D_MODEL), f32),
        scratch_shapes=[pltpu.VMEM((tm, D_FF), bf16)],
        compiler_params=pltpu.CompilerParams(
            dimension_semantics=("parallel",), vmem_limit_bytes=VMEM_LIMIT),
        name="conv_ffn",
    )(x2d, x2d, x2d, g2, wup, cw, cb, wdn, gf)


def _t5_bucket(rel):
    half = N_BUCKETS // 2
    max_exact = half // 2
    ret = jnp.where(rel > 0, half, 0)
    n = jnp.abs(rel)
    nf = jnp.maximum(n, 1).astype(f32)
    large = max_exact + (jnp.log(nf / max_exact) / math.log(MAX_DIST / max_exact)
                         * (half - max_exact)).astype(jnp.int32)
    large = jnp.minimum(large, half - 1)
    return ret + jnp.where(n < max_exact, n, large)


def _band_buckets():
    jpos = jnp.arange(3 * BLOCK)[:, None]
    qpos = jnp.arange(BLOCK)[None, :]
    rel = jpos - BLOCK - qpos
    return jnp.where(jnp.abs(rel) <= WINDOW, _t5_bucket(rel), -1).astype(jnp.int32)


def _rope_tables(seq):
    m = HEAD_DIM // 4
    pos = jnp.arange(seq)
    row = (pos // GRID_W).astype(f32)
    col = (pos % GRID_W).astype(f32)
    inv = ROPE_THETA ** (-jnp.arange(m, dtype=f32) / m)
    ang_r = row[:, None] * inv[None, :]
    ang_c = col[:, None] * inv[None, :]
    cos = jnp.concatenate([jnp.cos(ang_r), jnp.cos(ang_r), jnp.cos(ang_c), jnp.cos(ang_c)], axis=-1)
    sin = jnp.concatenate([-jnp.sin(ang_r), jnp.sin(ang_r), -jnp.sin(ang_c), jnp.sin(ang_c)], axis=-1)
    reps = LANES // HEAD_DIM
    return jnp.tile(cos, (1, reps)), jnp.tile(sin, (1, reps)), cos.T, sin.T


def _trunk(x, layers, stacked, bias, seg, final_g):
    bsz, seq, d = x.shape
    x2d = x.reshape(bsz * seq, d)
    cos_t, sin_t, cos_tt, sin_tt = _rope_tables(seq)
    for l, p in enumerate(layers):
        oa, kb, kc, qbt, qct, vbt, vct = _in_proj(
            x2d, seq, p["g1"], p["w_in"], p["wqt"], p["wvt"], p["lng"], p["lnb"], p["wsp"],
            p["bsp"], p["qgt"], p["kg"], cos_t, sin_t, cos_tt, sin_tt, seg)
        obt = _band_attn(qbt, kb, vbt, seq, p["sink"], bias)
        oct = lax.cond(
            p["logit_bound"] <= MAX_LOGIT_BOUND,
            lambda qct, pad, kc, vct: _flash_attn(qct, pad, kc, vct, seq, bounded=True),
            lambda qct, pad, kc, vct: _flash_attn(qct, pad, kc, vct, seq, bounded=False),
            qct, p["pad"], kc, vct)
        x2d = _merge(x2d, p["g1"], oa, obt, oct, stacked["wg"], p["bg"], stacked["wbr"],
                     stacked["wo"], l)
        x2d = _conv_ffn(x2d, seq, p["g2"], stacked["wup"], p["cw"], p["cb"], stacked["wdn"],
                        final_g, l, apply_final=(l == len(layers) - 1))
    return x2d.reshape(bsz, seq, d)


def kernel(x_prompt, x_sample, rel_bias, norm1_g, w_in, ln_v_g, ln_v_b, w_spatial, b_spatial,
           sink, q_norm_g, k_norm_g, w_gate, b_gate, w_branch, w_out, norm2_g, w_up, conv_w,
           conv_b, w_down, final_g):
    bias = _band_bias(rel_bias, _band_buckets())
    head_of_lane = jnp.arange(LANES) // HEAD_DIM
    seg = ((head_of_lane[:, None] == head_of_lane[None, :]).astype(f32) / HEAD_DIM).astype(bf16)
    reps = LANES // HEAD_DIM
    layers = []
    for l in range(DEPTH):
        ws = w_spatial[l].astype(bf16)
        wsp = jnp.concatenate([ws[0::2], ws[1::2]], axis=-1)
        wl = w_in[l].astype(bf16)
        b_q, b_k, b_v = A_IN, A_IN + Q_WIDTH, A_IN + Q_WIDTH + KV_WIDTH
        c_q, c_k, c_v = (b + QKV_WIDTH for b in (b_q, b_k, b_v))
        logit_bound = (1.02 * HEAD_DIM ** 0.5) * jnp.max(jnp.abs(q_norm_g[l])) * jnp.max(
            jnp.abs(k_norm_g[l]))
        shift = -(logit_bound * (LOG2E * (1.0 + 2.0 ** -7))).astype(bf16)
        pad = jnp.zeros((KV_WIDTH, GROUP * TQ_FLASH), bf16).at[0, :].set(shift)
        layers.append(dict(
            logit_bound=logit_bound,
            pad=pad,
            g1=norm1_g[l][None, :],
            w_in=jnp.concatenate(
                [wl[:, :A_IN], wl[:, b_k:b_k + KV_WIDTH], wl[:, c_k:c_k + KV_WIDTH]], axis=1),
            wqt=jnp.concatenate([wl[:, b_q:b_q + Q_WIDTH], wl[:, c_q:c_q + Q_WIDTH]], axis=1).T,
            wvt=jnp.concatenate([wl[:, b_v:b_v + KV_WIDTH], wl[:, c_v:c_v + KV_WIDTH]], axis=1).T,
            lng=ln_v_g[l][None, :],
            lnb=ln_v_b[l][None, :],
            wsp=wsp,
            bsp=jnp.repeat(b_spatial[l].T, HEAD_DIM, axis=1),
            qgt=jnp.broadcast_to(q_norm_g[l][:, None], (HEAD_DIM, TM_IN)),
            kg=jnp.tile(k_norm_g[l], reps)[None, :],
            sink=jnp.repeat(sink[l] * LOG2E, BLOCK).reshape(N_KV, 1, GROUP * BLOCK),
            bg=b_gate[l][None, :],
            g2=norm2_g[l][None, :],
            cw=conv_w[l],
            cb=conv_b[l][None, :],
        ))
    stacked = dict(wg=w_gate.astype(bf16), wbr=w_branch.astype(bf16), wo=w_out.astype(bf16),
                   wup=w_up.astype(bf16), wdn=w_down.astype(bf16))
    gf = final_g[None, :]
    y_prompt = _trunk(x_prompt, layers, stacked, bias, seg, gf)
    y_sample = _trunk(x_sample, layers, stacked, bias, seg, gf)
    return (y_prompt, y_sample)
```

```python
import functools
import math

import jax
import jax.numpy as jnp
from jax import lax
from jax.experimental import pallas as pl
from jax.experimental.pallas import tpu as pltpu

D_MODEL = 1024
DEPTH = 2
HEAD_DIM = 64
BLOCK = 128
A_GROUPS = 8
A_WIDTH = A_GROUPS * HEAD_DIM
N_HEADS = 8
N_KV = 2
GROUP = N_HEADS // N_KV
WINDOW = 128
ROPE_THETA = 10000.0
GRID_W = 64
N_BUCKETS = 32
MAX_DIST = 128
D_FF = 2816
EPS = 1e-6
N_BRANCH = 3
BR_WIDTH = 512
Q_WIDTH = N_HEADS * HEAD_DIM
KV_WIDTH = N_KV * HEAD_DIM
QKV_WIDTH = Q_WIDTH + 2 * KV_WIDTH
A_IN = 2 * A_WIDTH

LANES = 128
SUBLANES = 8
VMEM_LIMIT = 56 * 1024 * 1024

TM_IN = 512
TM_MERGE = 512
TM_FFN = 512
FF_CHUNK = 256
FFN_LOOKAHEAD = 2
BAND_R = 4
BAND_LOOKAHEAD = 2
TQ_FLASH = 128
TK_FLASH = 256
FLASH_LOOKAHEAD = 2
LOG2E = math.log2(math.e)
MAX_LOGIT_BOUND = 30.0

f32 = jnp.float32
bf16 = jnp.bfloat16


def _resident(shape, layer=None):
    if layer is None:
        return pl.BlockSpec(shape, lambda *_: (0,) * len(shape), pipeline_mode=pl.Buffered(1))
    return pl.BlockSpec((None,) + tuple(shape), lambda *_: (layer,) + (0,) * len(shape),
                        pipeline_mode=pl.Buffered(1))


def _zero_like(x):
    bits = pltpu.bitcast(x, jnp.uint32)
    return pltpu.bitcast((bits >> 16) >> 16, f32)


def _rms_scale(x):
    return lax.rsqrt(jnp.mean(x * x, axis=-1, keepdims=True) + EPS)


def _gelu_tanh(x):
    c = math.sqrt(2.0 / math.pi)
    return x * (0.5 * (1.0 + jnp.tanh(c * (x + 0.044715 * (x * x * x)))))


def _sigmoid(x):
    return 1.0 / (1.0 + jnp.exp(-x))


def _dot(a, b):
    return jnp.dot(a, b, preferred_element_type=f32)


def _dot_tn(a, b):
    return lax.dot_general(a, b, (((0,), (0,)), ((), ())), preferred_element_type=f32)


def _dot_nt(a, b):
    return lax.dot_general(a, b, (((1,), (1,)), ((), ())), preferred_element_type=f32)


def _head_mean_sq(x, seg_ref):
    sq = x * x
    hi = sq.astype(bf16)
    lo = (sq - hi.astype(f32)).astype(bf16)
    seg = seg_ref[...]
    return _dot(hi, seg) + _dot(lo, seg)


def _swap16(x):
    lane = lax.broadcasted_iota(jnp.int32, x.shape, 1)
    first_half = (lane % 32) < 16
    return jnp.where(first_half, pltpu.roll(x, LANES - 16, 1), pltpu.roll(x, 16, 1))


def _swap16_rows(x):
    h = HEAD_DIM // 4
    return jnp.concatenate([x[h:2 * h], x[0:h], x[3 * h:4 * h], x[2 * h:3 * h]], axis=0)


def _in_proj_kernel(x_ref, g1_ref, w_ref, wqt_ref, wvt_ref, lng_ref, lnb_ref, wsp_ref, bsp_ref,
                    qgt_ref, kg_ref, cos_ref, sin_ref, cost_ref, sint_ref, seg_ref,
                    oa_ref, kb_ref, kc_ref, qbt_ref, qct_ref, vbt_ref, vct_ref):
    tm = x_ref.shape[0]
    x = x_ref[...]
    xn = (x * _rms_scale(x) * g1_ref[...]).astype(bf16)

    zv = _dot(xn, w_ref[:, A_WIDTH:A_IN])
    zu = _dot(xn, w_ref[:, 0:A_WIDTH])
    zk = _dot(xn, w_ref[:, A_IN:A_IN + 2 * KV_WIDTH])
    qbt = _dot_nt(wqt_ref[0:Q_WIDTH, :], xn)
    qt = _dot_nt(wqt_ref[Q_WIDTH:2 * Q_WIDTH, :], xn)
    vt = _dot_nt(wvt_ref[...], xn).astype(bf16)

    v = _gelu_tanh(zv)
    u = _gelu_tanh(zu)
    mu = jnp.mean(v, axis=-1, keepdims=True)
    vc = v - mu
    vn = vc * lax.rsqrt(jnp.mean(vc * vc, axis=-1, keepdims=True) + EPS)
    vn = vn * lng_ref[...] + lnb_ref[...]
    lane = lax.broadcasted_iota(jnp.int32, (BLOCK, LANES), 1)
    low_half = lane < HEAD_DIM
    for c in range(tm // BLOCK):
        rows = slice(c * BLOCK, (c + 1) * BLOCK)
        for j in range(A_WIDTH // LANES):
            cols = slice(j * LANES, (j + 1) * LANES)
            vp = vn[rows, cols]
            stacked = jnp.concatenate(
                [jnp.where(low_half, vp, 0.0), jnp.where(low_half, 0.0, vp)], axis=0)
            sv = _dot(wsp_ref[j], stacked.astype(bf16)) + bsp_ref[:, cols]
            oa_ref[rows, cols] = (u[rows, cols] * sv).astype(bf16)

    kb_ref[...] = zk[:, 0:KV_WIDTH].astype(bf16)

    t = zk[:, KV_WIDTH:2 * KV_WIDTH]
    tn = t * lax.rsqrt(_head_mean_sq(t, seg_ref) + EPS) * kg_ref[...]
    kc_ref[:, 0:KV_WIDTH] = (tn * cos_ref[...] + _swap16(tn) * sin_ref[...]).astype(bf16)
    lane = lax.broadcasted_iota(jnp.int32, (tm, KV_WIDTH), 1)
    kc_ref[:, KV_WIDTH:2 * KV_WIDTH] = jnp.where(lane == 0, 1.0, 0.0).astype(bf16)

    scale = HEAD_DIM ** -0.5 * LOG2E
    qbt_ref[...] = (qbt * scale).astype(bf16)
    cost = cost_ref[...]
    sint = sint_ref[...]
    for h in range(N_HEADS):
        rows = slice(h * HEAD_DIM, (h + 1) * HEAD_DIM)
        th = qt[rows]
        r = lax.rsqrt(jnp.mean(th * th, axis=0, keepdims=True) + EPS)
        tn = th * r * qgt_ref[...]
        qct_ref[rows, :] = ((tn * cost + _swap16_rows(tn) * sint) * scale).astype(bf16)
    for c in range(tm // BLOCK):
        vbt_ref[c] = vt[0:KV_WIDTH, c * BLOCK:(c + 1) * BLOCK]
    for c in range(tm // TK_FLASH):
        vct_ref[c] = vt[KV_WIDTH:2 * KV_WIDTH, c * TK_FLASH:(c + 1) * TK_FLASH]


def _in_proj(x2d, seq, g1, w_in, wqt, wvt, lng, lnb, wsp, bsp, qgt, kg, cos_t, sin_t,
             cos_tt, sin_tt, seg):
    t = x2d.shape[0]
    tm = TM_IN
    tiles_per_seq = seq // tm
    const = lambda i: (0, 0)
    return pl.pallas_call(
        _in_proj_kernel,
        grid=(t // tm,),
        in_specs=[
            pl.BlockSpec((tm, D_MODEL), lambda i: (i, 0)),
            pl.BlockSpec((1, D_MODEL), const),
            _resident((D_MODEL, w_in.shape[1])),
            _resident((2 * Q_WIDTH, D_MODEL)),
            _resident((2 * KV_WIDTH, D_MODEL)),
            pl.BlockSpec((1, A_WIDTH), const),
            pl.BlockSpec((1, A_WIDTH), const),
            pl.BlockSpec((A_WIDTH // LANES, BLOCK, 2 * BLOCK), lambda i: (0, 0, 0)),
            pl.BlockSpec((BLOCK, A_WIDTH), const),
            pl.BlockSpec((HEAD_DIM, tm), const),
            pl.BlockSpec((1, LANES), const),
            pl.BlockSpec((tm, LANES), lambda i: (i % tiles_per_seq, 0)),
            pl.BlockSpec((tm, LANES), lambda i: (i % tiles_per_seq, 0)),
            pl.BlockSpec((HEAD_DIM, tm), lambda i: (0, i % tiles_per_seq)),
            pl.BlockSpec((HEAD_DIM, tm), lambda i: (0, i % tiles_per_seq)),
            pl.BlockSpec((LANES, LANES), const),
        ],
        out_specs=[
            pl.BlockSpec((tm, A_WIDTH), lambda i: (i, 0)),
            pl.BlockSpec((tm, KV_WIDTH), lambda i: (i, 0)),
            pl.BlockSpec((tm, 2 * KV_WIDTH), lambda i: (i, 0)),
            pl.BlockSpec((Q_WIDTH, tm), lambda i: (0, i)),
            pl.BlockSpec((Q_WIDTH, tm), lambda i: (0, i)),
            pl.BlockSpec((tm // BLOCK, KV_WIDTH, BLOCK), lambda i: (i, 0, 0)),
            pl.BlockSpec((tm // TK_FLASH, KV_WIDTH, TK_FLASH), lambda i: (i, 0, 0)),
        ],
        out_shape=[
            jax.ShapeDtypeStruct((t, A_WIDTH), bf16),
            jax.ShapeDtypeStruct((t, KV_WIDTH), bf16),
            jax.ShapeDtypeStruct((t, 2 * KV_WIDTH), bf16),
            jax.ShapeDtypeStruct((Q_WIDTH, t), bf16),
            jax.ShapeDtypeStruct((Q_WIDTH, t), bf16),
            jax.ShapeDtypeStruct((t // BLOCK, KV_WIDTH, BLOCK), bf16),
            jax.ShapeDtypeStruct((t // TK_FLASH, KV_WIDTH, TK_FLASH), bf16),
        ],
        compiler_params=pltpu.CompilerParams(
            dimension_semantics=("parallel",), vmem_limit_bytes=VMEM_LIMIT),
        name="in_proj",
    )(x2d, g1, w_in, wqt, wvt, lng, lnb, wsp, bsp, qgt, kg, cos_t, sin_t, cos_tt, sin_tt, seg)


def _band_bias_kernel(rel_bias_ref, bucket_ref, bias_ref):
    bucket = bucket_ref[...]
    for h in range(N_HEADS):
        val = jnp.full(bucket.shape, -jnp.inf, f32)
        for b in range(N_BUCKETS):
            val = jnp.where(bucket == b, rel_bias_ref[b, h] * LOG2E, val)
        gi = h % GROUP
        bias_ref[h // GROUP, :, gi * BLOCK:(gi + 1) * BLOCK] = val


def _band_bias(rel_bias, bucket_t):
    return pl.pallas_call(
        _band_bias_kernel,
        in_specs=[pl.BlockSpec(memory_space=pltpu.SMEM),
                  pl.BlockSpec((3 * BLOCK, BLOCK), lambda: (0, 0))],
        out_specs=pl.BlockSpec((N_KV, 3 * BLOCK, GROUP * BLOCK), lambda: (0, 0, 0)),
        out_shape=jax.ShapeDtypeStruct((N_KV, 3 * BLOCK, GROUP * BLOCK), f32),
        name="band_bias",
    )(rel_bias, bucket_t)


def _band_attn_kernel(nb, qt_ref, kp_ref, km_ref, kn_ref, vp_ref, vm_ref, vn_ref,
                      bias_ref, sink_ref, ot_ref):
    first = (pl.program_id(0) * BAND_R) % nb == 0
    last = ((pl.program_id(0) + 1) * BAND_R) % nb == 0
    k_win = jnp.concatenate([kp_ref[...], km_ref[...], kn_ref[...]], axis=0)
    v_blocks = [vp_ref[0]] + [vm_ref[r] for r in range(BAND_R)] + [vn_ref[0]]
    ones = jnp.ones((2 * SUBLANES, 3 * BLOCK), bf16)
    zeros = jnp.zeros((HEAD_DIM, GROUP * BLOCK), bf16)
    units = [(r, kv) for r in range(BAND_R) for kv in range(N_KV)]

    def logits(r, kv):
        qg = jnp.concatenate(
            [qt_ref[h * HEAD_DIM:(h + 1) * HEAD_DIM, r * BLOCK:(r + 1) * BLOCK]
             for h in range(kv * GROUP, (kv + 1) * GROUP)], axis=1)
        q_ext = jnp.concatenate([qg, zeros] if kv == 0 else [zeros, qg], axis=0)
        return _dot(k_win[r * BLOCK:(r + 3) * BLOCK], q_ext)

    s_queue = [logits(*u) for u in units[:BAND_LOOKAHEAD]]
    for i, (r, kv) in enumerate(units):
        if i + BAND_LOOKAHEAD < len(units):
            s_queue.append(logits(*units[i + BAND_LOOKAHEAD]))
        s = s_queue[i] + bias_ref[kv]
        s_queue[i] = None
        if r == 0:
            s = jnp.concatenate([jnp.where(first, -jnp.inf, s[0:BLOCK]), s[BLOCK:]], axis=0)
        if r == BAND_R - 1:
            s = jnp.concatenate([s[:2 * BLOCK], jnp.where(last, -jnp.inf, s[2 * BLOCK:])], axis=0)
        sink = sink_ref[kv]
        if i + 1 < len(units):
            sink = sink + _zero_like(s_queue[i + 1][0:1, :])
        m = jnp.maximum(jnp.max(s, axis=0, keepdims=True), sink)
        p = jnp.exp2(s - m).astype(bf16)
        v_band = jnp.concatenate(
            [blk[kv * HEAD_DIM:(kv + 1) * HEAD_DIM] for blk in v_blocks[r:r + 3]], axis=1)
        pv = _dot(jnp.concatenate([v_band, ones], axis=0), p)
        out = pv[0:HEAD_DIM] / (pv[HEAD_DIM:HEAD_DIM + 1] + jnp.exp2(sink - m))
        for gi in range(GROUP):
            h = kv * GROUP + gi
            ot_ref[h * HEAD_DIM:(h + 1) * HEAD_DIM, r * BLOCK:(r + 1) * BLOCK] = (
                out[:, gi * BLOCK:(gi + 1) * BLOCK].astype(bf16))


def _band_attn(qbt, kb, vbt, seq, sink_rows, bias_t):
    t = kb.shape[0]
    nb = seq // BLOCK
    nblocks = t // BLOCK
    r = BAND_R
    prev_blk = lambda i: jnp.maximum(i * r - 1, 0)
    next_blk = lambda i: jnp.minimum((i + 1) * r, nblocks - 1)
    return pl.pallas_call(
        functools.partial(_band_attn_kernel, nb),
        grid=(nblocks // r,),
        in_specs=[
            pl.BlockSpec((Q_WIDTH, r * BLOCK), lambda i: (0, i)),
            pl.BlockSpec((BLOCK, KV_WIDTH), lambda i: (prev_blk(i), 0)),
            pl.BlockSpec((r * BLOCK, KV_WIDTH), lambda i: (i, 0)),
            pl.BlockSpec((BLOCK, KV_WIDTH), lambda i: (next_blk(i), 0)),
            pl.BlockSpec((1, KV_WIDTH, BLOCK), lambda i: (prev_blk(i), 0, 0)),
            pl.BlockSpec((r, KV_WIDTH, BLOCK), lambda i: (i, 0, 0)),
            pl.BlockSpec((1, KV_WIDTH, BLOCK), lambda i: (next_blk(i), 0, 0)),
            pl.BlockSpec((N_KV, 3 * BLOCK, GROUP * BLOCK), lambda i: (0, 0, 0)),
            pl.BlockSpec((N_KV, 1, GROUP * BLOCK), lambda i: (0, 0, 0)),
        ],
        out_specs=pl.BlockSpec((Q_WIDTH, r * BLOCK), lambda i: (0, i)),
        out_shape=jax.ShapeDtypeStruct((Q_WIDTH, t), bf16),
        compiler_params=pltpu.CompilerParams(
            dimension_semantics=("parallel",), vmem_limit_bytes=VMEM_LIMIT),
        name="band_attn",
    )(qbt, kb, kb, kb, vbt, vbt, vbt, bias_t, sink_rows)


def _flash_queries(qt_ref, pad):
    tq = qt_ref.shape[1]
    zeros = jnp.zeros((HEAD_DIM, GROUP * tq), bf16)
    q_ext = []
    for kv in range(N_KV):
        qg = jnp.concatenate(
            [qt_ref[h * HEAD_DIM:(h + 1) * HEAD_DIM, :]
             for h in range(kv * GROUP, (kv + 1) * GROUP)], axis=1)
        q_ext.append(jnp.concatenate(([qg, zeros] if kv == 0 else [zeros, qg]) + [pad], axis=0))
    return q_ext


def _flash_store(ot_ref, kv, out):
    tq = ot_ref.shape[1]
    for gi in range(GROUP):
        h = kv * GROUP + gi
        ot_ref[h * HEAD_DIM:(h + 1) * HEAD_DIM, :] = out[:, gi * tq:(gi + 1) * tq].astype(bf16)


def _flash_online_kernel(qt_ref, k_ref, vt_ref, ot_ref):
    nq = GROUP * qt_ref.shape[1]
    n_chunks = vt_ref.shape[0]
    tk = vt_ref.shape[2]
    q_ext = _flash_queries(qt_ref, jnp.zeros((KV_WIDTH, nq), bf16))

    def logits(c):
        start = pl.multiple_of(c * tk, tk)
        kc = k_ref[pl.ds(start, tk), :]
        return tuple(_dot(kc, q_ext[kv]) for kv in range(N_KV))

    def body(c, carry):
        s_all, stats = carry
        s_next = logits(jnp.minimum(c + 1, n_chunks - 1))
        vt = vt_ref[c]
        out = []
        for kv in range(N_KV):
            m, l, acc = stats[kv]
            s = s_all[kv]
            m_new = jnp.maximum(m, jnp.max(s, axis=0, keepdims=True))
            alpha = jnp.exp2(m - m_new)
            p = jnp.exp2(s - m_new)
            l_new = alpha * l + jnp.sum(p, axis=0, keepdims=True)
            pv = _dot(vt[kv * HEAD_DIM:(kv + 1) * HEAD_DIM], p.astype(bf16))
            out.append((m_new, l_new, alpha * acc + pv))
        return s_next, tuple(out)

    init = tuple((jnp.full((1, nq), -jnp.inf, f32), jnp.zeros((1, nq), f32),
                  jnp.zeros((HEAD_DIM, nq), f32)) for _ in range(N_KV))
    _, final = lax.fori_loop(0, n_chunks, body, (logits(0), init))
    for kv in range(N_KV):
        _, l, acc = final[kv]
        _flash_store(ot_ref, kv, acc / l)


def _flash_bounded_kernel(qt_ref, pad_ref, k_ref, vt_ref, ot_ref):
    nq = GROUP * qt_ref.shape[1]
    n_chunks = vt_ref.shape[0]
    tk = vt_ref.shape[2]
    q_ext = _flash_queries(qt_ref, pad_ref[...])
    ones = jnp.ones((2 * SUBLANES, tk), bf16)
    rows = HEAD_DIM + 2 * SUBLANES

    units = [(c, kv) for c in range(n_chunks) for kv in range(N_KV)]

    def logits(c, kv):
        return _dot(k_ref[c * tk:(c + 1) * tk, :], q_ext[kv])

    s_queue = [logits(*u) for u in units[:FLASH_LOOKAHEAD]]
    acc = [None] * N_KV
    for i, (c, kv) in enumerate(units):
        if i + FLASH_LOOKAHEAD < len(units):
            s_queue.append(logits(*units[i + FLASH_LOOKAHEAD]))
        p = jnp.exp2(s_queue[i]).astype(bf16)
        s_queue[i] = None
        v_ext = jnp.concatenate([vt_ref[c, kv * HEAD_DIM:(kv + 1) * HEAD_DIM, :], ones], axis=0)
        pv = _dot(v_ext, p)
        acc[kv] = pv if acc[kv] is None else acc[kv] + pv
    for kv in range(N_KV):
        _flash_store(ot_ref, kv, acc[kv][0:HEAD_DIM] / acc[kv][HEAD_DIM:HEAD_DIM + 1])


def _flash_attn(qct, pad, kc, vct, seq, bounded):
    t = kc.shape[0]
    tq = TQ_FLASH
    q_tiles = seq // tq
    tk = vct.shape[2]
    q_spec = pl.BlockSpec((Q_WIDTH, tq), lambda b, i: (0, b * q_tiles + i))
    kv_specs = [
        pl.BlockSpec((seq, 2 * KV_WIDTH), lambda b, i: (b, 0)),
        pl.BlockSpec((seq // tk, KV_WIDTH, tk), lambda b, i: (b, 0, 0)),
    ]
    pad_spec = pl.BlockSpec((KV_WIDTH, GROUP * tq), lambda b, i: (0, 0))
    return pl.pallas_call(
        _flash_bounded_kernel if bounded else _flash_online_kernel,
        grid=(t // seq, q_tiles),
        in_specs=[q_spec] + ([pad_spec] if bounded else []) + kv_specs,
        out_specs=q_spec,
        out_shape=jax.ShapeDtypeStruct((Q_WIDTH, t), bf16),
        compiler_params=pltpu.CompilerParams(
            dimension_semantics=("parallel", "parallel"), vmem_limit_bytes=VMEM_LIMIT),
        name="flash_bounded" if bounded else "flash_online",
    )(*([qct] + ([pad] if bounded else []) + [kc, vct]))


def _merge_kernel(x_ref, g1_ref, oa_ref, obt_ref, oct_ref, wg_ref, bg_ref, wbr_ref, wo_ref,
                  y_ref):
    x = x_ref[...]
    xn = (x * _rms_scale(x) * g1_ref[...]).astype(bf16)
    projs = (_dot(oa_ref[...], wbr_ref[0]), _dot_tn(obt_ref[...], wbr_ref[1]),
             _dot_tn(oct_ref[...], wbr_ref[2]))
    logits = [_dot(xn, wg_ref[:, n * D_MODEL:(n + 1) * D_MODEL]) for n in range(N_BRANCH)]
    merged = None
    for n, proj in enumerate(projs):
        gate = _sigmoid(logits[n] + bg_ref[:, n * D_MODEL:(n + 1) * D_MODEL])
        term = gate * proj
        merged = term if merged is None else merged + term
    y_ref[...] = x + _dot(merged.astype(bf16), wo_ref[...])


def _merge(x2d, g1, oa, obt, oct, wg, bg, wbr, wo, layer):
    t = x2d.shape[0]
    tm = TM_MERGE
    const = lambda i: (0, 0)
    row = lambda width: pl.BlockSpec((tm, width), lambda i: (i, 0))
    return pl.pallas_call(
        _merge_kernel,
        grid=(t // tm,),
        in_specs=[
            row(D_MODEL),
            pl.BlockSpec((1, D_MODEL), const),
            row(BR_WIDTH),
            pl.BlockSpec((BR_WIDTH, tm), lambda i: (0, i)),
            pl.BlockSpec((BR_WIDTH, tm), lambda i: (0, i)),
            _resident((D_MODEL, N_BRANCH * D_MODEL), layer),
            pl.BlockSpec((1, N_BRANCH * D_MODEL), const),
            _resident((N_BRANCH, BR_WIDTH, D_MODEL), layer),
            _resident((D_MODEL, D_MODEL), layer),
        ],
        out_specs=row(D_MODEL),
        out_shape=jax.ShapeDtypeStruct((t, D_MODEL), f32),
        compiler_params=pltpu.CompilerParams(
            dimension_semantics=("parallel",), vmem_limit_bytes=VMEM_LIMIT),
        name="merge",
    )(x2d, g1, oa, obt, oct, wg, bg, wbr, wo)


def _conv_ffn_kernel(tiles_per_seq, apply_final, xp_ref, x_ref, xq_ref, g2_ref, wup_ref,
                     cw_ref, cb_ref, wdn_ref, gf_ref, y_ref, act_ref):
    q = x_ref.shape[0]
    order = (3, 1, 2, 0)
    start = {s: SUBLANES + q * pos for pos, s in enumerate(order)}
    i = pl.program_id(0) % tiles_per_seq
    x = jnp.concatenate([x_ref[:, s * D_MODEL:(s + 1) * D_MODEL] for s in order], axis=0)
    halo_p = jnp.where(i > 0, xp_ref[...], 0.0)
    halo_n = jnp.where(i < tiles_per_seq - 1, xq_ref[...], 0.0)
    xe = jnp.concatenate([halo_p, x, halo_n], axis=0)
    xn = (xe * _rms_scale(xe) * g2_ref[...]).astype(bf16)
    n_chunks = D_FF // FF_CHUNK
    chunk_cols = lambda c: (slice(c * FF_CHUNK, (c + 1) * FF_CHUNK),
                            slice(D_FF + c * FF_CHUNK, D_FF + (c + 1) * FF_CHUNK))

    def up(c):
        return tuple(_dot(xn, wup_ref[:, cols]) for cols in chunk_cols(c))

    def conv(h, cols):
        blk = {s: h[start[s]:start[s] + q] for s in order}
        span = q + SUBLANES
        blk[-1] = pltpu.roll(h[0:span], 1, 0)[SUBLANES:span]
        blk[4] = pltpu.roll(h[start[0]:start[0] + span], span - 1, 0)[0:q]
        out = [blk[s - 1] * cw_ref[0:1, cols] + blk[s] * cw_ref[1:2, cols]
               + blk[s + 1] * cw_ref[2:3, cols] + cb_ref[:, cols] for s in order]
        return jnp.concatenate(out, axis=0)

    h_queue = [up(c) for c in range(FFN_LOOKAHEAD)]
    for c in range(n_chunks):
        if c + FFN_LOOKAHEAD < n_chunks:
            h_queue.append(up(c + FFN_LOOKAHEAD))
        gate, val = (conv(h, cols) for h, cols in zip(h_queue[c], chunk_cols(c)))
        h_queue[c] = None
        act_ref[:, chunk_cols(c)[0]] = (gate * _sigmoid(gate) * val).astype(bf16)
    y = x + _dot(act_ref[...], wdn_ref[...])
    if apply_final:
        y = y * _rms_scale(y) * gf_ref[...]
    for pos, s in enumerate(order):
        y_ref[:, s * D_MODEL:(s + 1) * D_MODEL] = y[pos * q:(pos + 1) * q]


def _conv_ffn(x2d, seq, g2, wup, cw, cb, wdn, gf, layer, apply_final):
    t = x2d.shape[0]
    tm = TM_FFN
    tiles_per_seq = seq // tm
    halo_per_tile = tm // SUBLANES
    n_halo = t // SUBLANES
    const = lambda i: (0, 0)
    tile_view = pl.BlockSpec((tm // 4, 4 * D_MODEL), lambda i: (i, 0))
    y = pl.pallas_call(
        functools.partial(_conv_ffn_kernel, tiles_per_seq, apply_final),
        grid=(t // tm,),
        in_specs=[
            pl.BlockSpec((SUBLANES, D_MODEL),
                         lambda i: (jnp.maximum(i * halo_per_tile - 1, 0), 0)),
            tile_view,
            pl.BlockSpec((SUBLANES, D_MODEL),
                         lambda i: (jnp.minimum((i + 1) * halo_per_tile, n_halo - 1), 0)),
            pl.BlockSpec((1, D_MODEL), const),
            _resident((D_MODEL, 2 * D_FF), layer),
            pl.BlockSpec((3, 2 * D_FF), const),
            pl.BlockSpec((1, 2 * D_FF), const),
            _resident((D_FF, D_MODEL), layer),
            pl.BlockSpec((1, D_MODEL), const),
        ],
        out_specs=tile_view,
        out_shape=jax.ShapeDtypeStruct((t // 4, 4 * D_MODEL), f32),
        scratch_shapes=[pltpu.VMEM((tm, D_FF), bf16)],
        compiler_params=pltpu.CompilerParams(
            dimension_semantics=("parallel",), vmem_limit_bytes=VMEM_LIMIT),
        name="conv_ffn",
    )(x2d, x2d.reshape(t // 4, 4 * D_MODEL), x2d, g2, wup, cw, cb, wdn, gf)
    return y.reshape(t, D_MODEL)


def _t5_bucket(rel):
    half = N_BUCKETS // 2
    max_exact = half // 2
    ret = jnp.where(rel > 0, half, 0)
    n = jnp.abs(rel)
    nf = jnp.maximum(n, 1).astype(f32)
    large = max_exact + (jnp.log(nf / max_exact) / math.log(MAX_DIST / max_exact)
                         * (half - max_exact)).astype(jnp.int32)
    large = jnp.minimum(large, half - 1)
    return ret + jnp.where(n < max_exact, n, large)


def _band_buckets():
    jpos = jnp.arange(3 * BLOCK)[:, None]
    qpos = jnp.arange(BLOCK)[None, :]
    rel = jpos - BLOCK - qpos
    return jnp.where(jnp.abs(rel) <= WINDOW, _t5_bucket(rel), -1).astype(jnp.int32)


def _rope_tables(seq):
    m = HEAD_DIM // 4
    pos = jnp.arange(seq)
    row = (pos // GRID_W).astype(f32)
    col = (pos % GRID_W).astype(f32)
    inv = ROPE_THETA ** (-jnp.arange(m, dtype=f32) / m)
    ang_r = row[:, None] * inv[None, :]
    ang_c = col[:, None] * inv[None, :]
    cos = jnp.concatenate([jnp.cos(ang_r), jnp.cos(ang_r), jnp.cos(ang_c), jnp.cos(ang_c)], axis=-1)
    sin = jnp.concatenate([-jnp.sin(ang_r), jnp.sin(ang_r), -jnp.sin(ang_c), jnp.sin(ang_c)], axis=-1)
    reps = LANES // HEAD_DIM
    return jnp.tile(cos, (1, reps)), jnp.tile(sin, (1, reps)), cos.T, sin.T


def _trunk(x, layers, stacked, bias, seg, final_g):
    bsz, seq, d = x.shape
    x2d = x.reshape(bsz * seq, d)
    cos_t, sin_t, cos_tt, sin_tt = _rope_tables(seq)
    for l, p in enumerate(layers):
        oa, kb, kc, qbt, qct, vbt, vct = _in_proj(
            x2d, seq, p["g1"], p["w_in"], p["wqt"], p["wvt"], p["lng"], p["lnb"], p["wsp"],
            p["bsp"], p["qgt"], p["kg"], cos_t, sin_t, cos_tt, sin_tt, seg)
        obt = _band_attn(qbt, kb, vbt, seq, p["sink"], bias)
        oct = lax.cond(
            p["logit_bound"] <= MAX_LOGIT_BOUND,
            lambda qct, pad, kc, vct: _flash_attn(qct, pad, kc, vct, seq, bounded=True),
            lambda qct, pad, kc, vct: _flash_attn(qct, pad, kc, vct, seq, bounded=False),
            qct, p["pad"], kc, vct)
        x2d = _merge(x2d, p["g1"], oa, obt, oct, stacked["wg"], p["bg"], stacked["wbr"],
                     stacked["wo"], l)
        x2d = _conv_ffn(x2d, seq, p["g2"], stacked["wup"], p["cw"], p["cb"], stacked["wdn"],
                        final_g, l, apply_final=(l == len(layers) - 1))
    return x2d.reshape(bsz, seq, d)


def kernel(x_prompt, x_sample, rel_bias, norm1_g, w_in, ln_v_g, ln_v_b, w_spatial, b_spatial,
           sink, q_norm_g, k_norm_g, w_gate, b_gate, w_branch, w_out, norm2_g, w_up, conv_w,
           conv_b, w_down, final_g):
    bias = _band_bias(rel_bias, _band_buckets())
    head_of_lane = jnp.arange(LANES) // HEAD_DIM
    seg = ((head_of_lane[:, None] == head_of_lane[None, :]).astype(f32) / HEAD_DIM).astype(bf16)
    reps = LANES // HEAD_DIM
    layers = []
    for l in range(DEPTH):
        ws = w_spatial[l].astype(bf16)
        wsp = jnp.concatenate([ws[0::2], ws[1::2]], axis=-1)
        wl = w_in[l].astype(bf16)
        b_q, b_k, b_v = A_IN, A_IN + Q_WIDTH, A_IN + Q_WIDTH + KV_WIDTH
        c_q, c_k, c_v = (b + QKV_WIDTH for b in (b_q, b_k, b_v))
        logit_bound = (1.02 * HEAD_DIM ** 0.5) * jnp.max(jnp.abs(q_norm_g[l])) * jnp.max(
            jnp.abs(k_norm_g[l]))
        shift = -(logit_bound * (LOG2E * (1.0 + 2.0 ** -7))).astype(bf16)
        pad = jnp.zeros((KV_WIDTH, GROUP * TQ_FLASH), bf16).at[0, :].set(shift)
        layers.append(dict(
            logit_bound=logit_bound,
            pad=pad,
            g1=norm1_g[l][None, :],
            w_in=jnp.concatenate(
                [wl[:, :A_IN], wl[:, b_k:b_k + KV_WIDTH], wl[:, c_k:c_k + KV_WIDTH]], axis=1),
            wqt=jnp.concatenate([wl[:, b_q:b_q + Q_WIDTH], wl[:, c_q:c_q + Q_WIDTH]], axis=1).T,
            wvt=jnp.concatenate([wl[:, b_v:b_v + KV_WIDTH], wl[:, c_v:c_v + KV_WIDTH]], axis=1).T,
            lng=ln_v_g[l][None, :],
            lnb=ln_v_b[l][None, :],
            wsp=wsp,
            bsp=jnp.repeat(b_spatial[l].T, HEAD_DIM, axis=1),
            qgt=jnp.broadcast_to(q_norm_g[l][:, None], (HEAD_DIM, TM_IN)),
            kg=jnp.tile(k_norm_g[l], reps)[None, :],
            sink=jnp.repeat(sink[l] * LOG2E, BLOCK).reshape(N_KV, 1, GROUP * BLOCK),
            bg=b_gate[l][None, :],
            g2=norm2_g[l][None, :],
            cw=conv_w[l],
            cb=conv_b[l][None, :],
        ))
    stacked = dict(wg=w_gate.astype(bf16), wbr=w_branch.astype(bf16), wo=w_out.astype(bf16),
                   wup=w_up.astype(bf16), wdn=w_down.astype(bf16))
    gf = final_g[None, :]
    y_prompt = _trunk(x_prompt, layers, stacked, bias, seg, gf)
    y_sample = _trunk(x_sample, layers, stacked, bias, seg, gf)
    return (y_prompt, y_sample)
```

```python
import functools
import math

import jax
import jax.numpy as jnp
from jax import lax
from jax.experimental import pallas as pl
from jax.experimental.pallas import tpu as pltpu

D_MODEL = 1024
DEPTH = 2
HEAD_DIM = 64
BLOCK = 128
A_GROUPS = 8
A_WIDTH = A_GROUPS * HEAD_DIM
N_HEADS = 8
N_KV = 2
GROUP = N_HEADS // N_KV
WINDOW = 128
ROPE_THETA = 10000.0
GRID_W = 64
N_BUCKETS = 32
MAX_DIST = 128
D_FF = 2816
EPS = 1e-6
N_BRANCH = 3
BR_WIDTH = 512
Q_WIDTH = N_HEADS * HEAD_DIM
KV_WIDTH = N_KV * HEAD_DIM
QKV_WIDTH = Q_WIDTH + 2 * KV_WIDTH
A_IN = 2 * A_WIDTH

LANES = 128
SUBLANES = 8
VMEM_LIMIT = 56 * 1024 * 1024

TM_IN = 512
TM_MERGE = 512
TM_FFN = 512
FF_CHUNK = 256
FFN_LOOKAHEAD = 2
BAND_R = 4
BAND_LOOKAHEAD = 2
TQ_FLASH = 128
TK_FLASH = 256
FLASH_LOOKAHEAD = 2
LOG2E = math.log2(math.e)
MAX_LOGIT_BOUND = 30.0

f32 = jnp.float32
bf16 = jnp.bfloat16


def _resident(shape, layer=None):
    if layer is None:
        return pl.BlockSpec(shape, lambda *_: (0,) * len(shape), pipeline_mode=pl.Buffered(1))
    return pl.BlockSpec((None,) + tuple(shape), lambda *_: (layer,) + (0,) * len(shape),
                        pipeline_mode=pl.Buffered(1))


def _zero_like(x):
    bits = pltpu.bitcast(x, jnp.uint32)
    return pltpu.bitcast((bits >> 16) >> 16, f32)


def _rms_scale(x):
    return lax.rsqrt(jnp.mean(x * x, axis=-1, keepdims=True) + EPS)


def _gelu_tanh(x):
    c = math.sqrt(2.0 / math.pi)
    return x * (0.5 * (1.0 + jnp.tanh(c * (x + 0.044715 * (x * x * x)))))


def _sigmoid(x):
    return 1.0 / (1.0 + jnp.exp(-x))


def _dot(a, b):
    return jnp.dot(a, b, preferred_element_type=f32)


def _dot_tn(a, b):
    return lax.dot_general(a, b, (((0,), (0,)), ((), ())), preferred_element_type=f32)


def _dot_nt(a, b):
    return lax.dot_general(a, b, (((1,), (1,)), ((), ())), preferred_element_type=f32)


def _head_mean_sq(x, seg_ref):
    sq = x * x
    hi = sq.astype(bf16)
    lo = (sq - hi.astype(f32)).astype(bf16)
    seg = seg_ref[...]
    return _dot(hi, seg) + _dot(lo, seg)


def _swap16(x):
    lane = lax.broadcasted_iota(jnp.int32, x.shape, 1)
    first_half = (lane % 32) < 16
    return jnp.where(first_half, pltpu.roll(x, LANES - 16, 1), pltpu.roll(x, 16, 1))


def _swap16_rows(x):
    h = HEAD_DIM // 4
    return jnp.concatenate([x[h:2 * h], x[0:h], x[3 * h:4 * h], x[2 * h:3 * h]], axis=0)


def _in_proj_kernel(x_ref, g1_ref, w_ref, wqt_ref, wvt_ref, lng_ref, lnb_ref, wsp_ref, bsp_ref,
                    qgt_ref, kg_ref, cos_ref, sin_ref, cost_ref, sint_ref, seg_ref,
                    oa_ref, kb_ref, kc_ref, qbt_ref, qct_ref, vbt_ref, vct_ref):
    tm = x_ref.shape[0]
    x = x_ref[...]
    xn = (x * _rms_scale(x) * g1_ref[...]).astype(bf16)

    zv = _dot(xn, w_ref[:, A_WIDTH:A_IN])
    zu = _dot(xn, w_ref[:, 0:A_WIDTH])
    zk = _dot(xn, w_ref[:, A_IN:A_IN + 2 * KV_WIDTH])
    qbt = _dot_nt(wqt_ref[0:Q_WIDTH, :], xn)
    qt = _dot_nt(wqt_ref[Q_WIDTH:2 * Q_WIDTH, :], xn)
    vt = _dot_nt(wvt_ref[...], xn).astype(bf16)

    v = _gelu_tanh(zv)
    u = _gelu_tanh(zu)
    mu = jnp.mean(v, axis=-1, keepdims=True)
    vc = v - mu
    vn = vc * lax.rsqrt(jnp.mean(vc * vc, axis=-1, keepdims=True) + EPS)
    vn = vn * lng_ref[...] + lnb_ref[...]
    lane = lax.broadcasted_iota(jnp.int32, (BLOCK, LANES), 1)
    low_half = lane < HEAD_DIM
    for c in range(tm // BLOCK):
        rows = slice(c * BLOCK, (c + 1) * BLOCK)
        for j in range(A_WIDTH // LANES):
            cols = slice(j * LANES, (j + 1) * LANES)
            vp = vn[rows, cols]
            stacked = jnp.concatenate(
                [jnp.where(low_half, vp, 0.0), jnp.where(low_half, 0.0, vp)], axis=0)
            sv = _dot(wsp_ref[j], stacked.astype(bf16)) + bsp_ref[:, cols]
            oa_ref[rows, cols] = (u[rows, cols] * sv).astype(bf16)

    kb_ref[...] = zk[:, 0:KV_WIDTH].astype(bf16)

    t = zk[:, KV_WIDTH:2 * KV_WIDTH]
    tn = t * lax.rsqrt(_head_mean_sq(t, seg_ref) + EPS) * kg_ref[...]
    kc_ref[:, 0:KV_WIDTH] = (tn * cos_ref[...] + _swap16(tn) * sin_ref[...]).astype(bf16)
    lane = lax.broadcasted_iota(jnp.int32, (tm, KV_WIDTH), 1)
    kc_ref[:, KV_WIDTH:2 * KV_WIDTH] = jnp.where(lane == 0, 1.0, 0.0).astype(bf16)

    scale = HEAD_DIM ** -0.5 * LOG2E
    qbt_ref[...] = (qbt * scale).astype(bf16)
    cost = cost_ref[...]
    sint = sint_ref[...]
    for h in range(N_HEADS):
        rows = slice(h * HEAD_DIM, (h + 1) * HEAD_DIM)
        th = qt[rows]
        r = lax.rsqrt(jnp.mean(th * th, axis=0, keepdims=True) + EPS)
        tn = th * r * qgt_ref[...]
        qct_ref[rows, :] = ((tn * cost + _swap16_rows(tn) * sint) * scale).astype(bf16)
    for c in range(tm // BLOCK):
        vbt_ref[c] = vt[0:KV_WIDTH, c * BLOCK:(c + 1) * BLOCK]
    for c in range(tm // TK_FLASH):
        vct_ref[c] = vt[KV_WIDTH:2 * KV_WIDTH, c * TK_FLASH:(c + 1) * TK_FLASH]


def _in_proj(x2d, seq, g1, w_in, wqt, wvt, lng, lnb, wsp, bsp, qgt, kg, cos_t, sin_t,
             cos_tt, sin_tt, seg):
    t = x2d.shape[0]
    tm = TM_IN
    tiles_per_seq = seq // tm
    const = lambda i: (0, 0)
    return pl.pallas_call(
        _in_proj_kernel,
        grid=(t // tm,),
        in_specs=[
            pl.BlockSpec((tm, D_MODEL), lambda i: (i, 0)),
            pl.BlockSpec((1, D_MODEL), const),
            _resident((D_MODEL, w_in.shape[1])),
            _resident((2 * Q_WIDTH, D_MODEL)),
            _resident((2 * KV_WIDTH, D_MODEL)),
            pl.BlockSpec((1, A_WIDTH), const),
            pl.BlockSpec((1, A_WIDTH), const),
            pl.BlockSpec((A_WIDTH // LANES, BLOCK, 2 * BLOCK), lambda i: (0, 0, 0)),
            pl.BlockSpec((BLOCK, A_WIDTH), const),
            pl.BlockSpec((HEAD_DIM, tm), const),
            pl.BlockSpec((1, LANES), const),
            pl.BlockSpec((tm, LANES), lambda i: (i % tiles_per_seq, 0)),
            pl.BlockSpec((tm, LANES), lambda i: (i % tiles_per_seq, 0)),
            pl.BlockSpec((HEAD_DIM, tm), lambda i: (0, i % tiles_per_seq)),
            pl.BlockSpec((HEAD_DIM, tm), lambda i: (0, i % tiles_per_seq)),
            pl.BlockSpec((LANES, LANES), const),
        ],
        out_specs=[
            pl.BlockSpec((tm, A_WIDTH), lambda i: (i, 0)),
            pl.BlockSpec((tm, KV_WIDTH), lambda i: (i, 0)),
            pl.BlockSpec((tm, 2 * KV_WIDTH), lambda i: (i, 0)),
            pl.BlockSpec((Q_WIDTH, tm), lambda i: (0, i)),
            pl.BlockSpec((Q_WIDTH, tm), lambda i: (0, i)),
            pl.BlockSpec((tm // BLOCK, KV_WIDTH, BLOCK), lambda i: (i, 0, 0)),
            pl.BlockSpec((tm // TK_FLASH, KV_WIDTH, TK_FLASH), lambda i: (i, 0, 0)),
        ],
        out_shape=[
            jax.ShapeDtypeStruct((t, A_WIDTH), bf16),
            jax.ShapeDtypeStruct((t, KV_WIDTH), bf16),
            jax.ShapeDtypeStruct((t, 2 * KV_WIDTH), bf16),
            jax.ShapeDtypeStruct((Q_WIDTH, t), bf16),
            jax.ShapeDtypeStruct((Q_WIDTH, t), bf16),
            jax.ShapeDtypeStruct((t // BLOCK, KV_WIDTH, BLOCK), bf16),
            jax.ShapeDtypeStruct((t // TK_FLASH, KV_WIDTH, TK_FLASH), bf16),
        ],
        compiler_params=pltpu.CompilerParams(
            dimension_semantics=("parallel",), vmem_limit_bytes=VMEM_LIMIT),
        name="in_proj",
    )(x2d, g1, w_in, wqt, wvt, lng, lnb, wsp, bsp, qgt, kg, cos_t, sin_t, cos_tt, sin_tt, seg)


def _band_bias_kernel(rel_bias_ref, bucket_ref, bias_ref):
    bucket = bucket_ref[...]
    for h in range(N_HEADS):
        val = jnp.full(bucket.shape, -jnp.inf, f32)
        for b in range(N_BUCKETS):
            val = jnp.where(bucket == b, rel_bias_ref[b, h] * LOG2E, val)
        gi = h % GROUP
        bias_ref[h // GROUP, :, gi * BLOCK:(gi + 1) * BLOCK] = val


def _band_bias(rel_bias, bucket_t):
    return pl.pallas_call(
        _band_bias_kernel,
        in_specs=[pl.BlockSpec(memory_space=pltpu.SMEM),
                  pl.BlockSpec((3 * BLOCK, BLOCK), lambda: (0, 0))],
        out_specs=pl.BlockSpec((N_KV, 3 * BLOCK, GROUP * BLOCK), lambda: (0, 0, 0)),
        out_shape=jax.ShapeDtypeStruct((N_KV, 3 * BLOCK, GROUP * BLOCK), f32),
        name="band_bias",
    )(rel_bias, bucket_t)


def _band_attn_kernel(nb, qt_ref, kp_ref, km_ref, kn_ref, vp_ref, vm_ref, vn_ref,
                      bias_ref, sink_ref, ot_ref):
    first = (pl.program_id(0) * BAND_R) % nb == 0
    last = ((pl.program_id(0) + 1) * BAND_R) % nb == 0
    k_win = jnp.concatenate([kp_ref[...], km_ref[...], kn_ref[...]], axis=0)
    v_blocks = [vp_ref[0]] + [vm_ref[r] for r in range(BAND_R)] + [vn_ref[0]]
    ones = jnp.ones((2 * SUBLANES, 3 * BLOCK), bf16)
    zeros = jnp.zeros((HEAD_DIM, GROUP * BLOCK), bf16)
    units = [(r, kv) for r in range(BAND_R) for kv in range(N_KV)]

    def logits(r, kv):
        qg = jnp.concatenate(
            [qt_ref[h * HEAD_DIM:(h + 1) * HEAD_DIM, r * BLOCK:(r + 1) * BLOCK]
             for h in range(kv * GROUP, (kv + 1) * GROUP)], axis=1)
        q_ext = jnp.concatenate([qg, zeros] if kv == 0 else [zeros, qg], axis=0)
        return _dot(k_win[r * BLOCK:(r + 3) * BLOCK], q_ext)

    s_queue = [logits(*u) for u in units[:BAND_LOOKAHEAD]]
    for i, (r, kv) in enumerate(units):
        if i + BAND_LOOKAHEAD < len(units):
            s_queue.append(logits(*units[i + BAND_LOOKAHEAD]))
        s = s_queue[i] + bias_ref[kv]
        s_queue[i] = None
        if r == 0:
            s = jnp.concatenate([jnp.where(first, -jnp.inf, s[0:BLOCK]), s[BLOCK:]], axis=0)
        if r == BAND_R - 1:
            s = jnp.concatenate([s[:2 * BLOCK], jnp.where(last, -jnp.inf, s[2 * BLOCK:])], axis=0)
        sink = sink_ref[kv]
        if i + 1 < len(units):
            sink = sink + _zero_like(s_queue[i + 1][0:1, :])
        m = jnp.maximum(jnp.max(s, axis=0, keepdims=True), sink)
        p = jnp.exp2(s - m).astype(bf16)
        v_band = jnp.concatenate(
            [blk[kv * HEAD_DIM:(kv + 1) * HEAD_DIM] for blk in v_blocks[r:r + 3]], axis=1)
        pv = _dot(jnp.concatenate([v_band, ones], axis=0), p)
        out = pv[0:HEAD_DIM] / (pv[HEAD_DIM:HEAD_DIM + 1] + jnp.exp2(sink - m))
        for gi in range(GROUP):
            h = kv * GROUP + gi
            ot_ref[h * HEAD_DIM:(h + 1) * HEAD_DIM, r * BLOCK:(r + 1) * BLOCK] = (
                out[:, gi * BLOCK:(gi + 1) * BLOCK].astype(bf16))


def _band_attn(qbt, kb, vbt, seq, sink_rows, bias_t):
    t = kb.shape[0]
    nb = seq // BLOCK
    nblocks = t // BLOCK
    r = BAND_R
    prev_blk = lambda i: jnp.maximum(i * r - 1, 0)
    next_blk = lambda i: jnp.minimum((i + 1) * r, nblocks - 1)
    return pl.pallas_call(
        functools.partial(_band_attn_kernel, nb),
        grid=(nblocks // r,),
        in_specs=[
            pl.BlockSpec((Q_WIDTH, r * BLOCK), lambda i: (0, i)),
            pl.BlockSpec((BLOCK, KV_WIDTH), lambda i: (prev_blk(i), 0)),
            pl.BlockSpec((r * BLOCK, KV_WIDTH), lambda i: (i, 0)),
            pl.BlockSpec((BLOCK, KV_WIDTH), lambda i: (next_blk(i), 0)),
            pl.BlockSpec((1, KV_WIDTH, BLOCK), lambda i: (prev_blk(i), 0, 0)),
            pl.BlockSpec((r, KV_WIDTH, BLOCK), lambda i: (i, 0, 0)),
            pl.BlockSpec((1, KV_WIDTH, BLOCK), lambda i: (next_blk(i), 0, 0)),
            pl.BlockSpec((N_KV, 3 * BLOCK, GROUP * BLOCK), lambda i: (0, 0, 0)),
            pl.BlockSpec((N_KV, 1, GROUP * BLOCK), lambda i: (0, 0, 0)),
        ],
        out_specs=pl.BlockSpec((Q_WIDTH, r * BLOCK), lambda i: (0, i)),
        out_shape=jax.ShapeDtypeStruct((Q_WIDTH, t), bf16),
        compiler_params=pltpu.CompilerParams(
            dimension_semantics=("parallel",), vmem_limit_bytes=VMEM_LIMIT),
        name="band_attn",
    )(qbt, kb, kb, kb, vbt, vbt, vbt, bias_t, sink_rows)


def _flash_queries(qt_ref, pad):
    tq = qt_ref.shape[1]
    zeros = jnp.zeros((HEAD_DIM, GROUP * tq), bf16)
    q_ext = []
    for kv in range(N_KV):
        qg = jnp.concatenate(
            [qt_ref[h * HEAD_DIM:(h + 1) * HEAD_DIM, :]
             for h in range(kv * GROUP, (kv + 1) * GROUP)], axis=1)
        q_ext.append(jnp.concatenate(([qg, zeros] if kv == 0 else [zeros, qg]) + [pad], axis=0))
    return q_ext


def _flash_store(ot_ref, kv, out):
    tq = ot_ref.shape[1]
    for gi in range(GROUP):
        h = kv * GROUP + gi
        ot_ref[h * HEAD_DIM:(h + 1) * HEAD_DIM, :] = out[:, gi * tq:(gi + 1) * tq].astype(bf16)


def _flash_online_kernel(qt_ref, k_ref, vt_ref, ot_ref):
    nq = GROUP * qt_ref.shape[1]
    n_chunks = vt_ref.shape[0]
    tk = vt_ref.shape[2]
    q_ext = _flash_queries(qt_ref, jnp.zeros((KV_WIDTH, nq), bf16))

    def logits(c):
        start = pl.multiple_of(c * tk, tk)
        kc = k_ref[pl.ds(start, tk), :]
        return tuple(_dot(kc, q_ext[kv]) for kv in range(N_KV))

    def body(c, carry):
        s_all, stats = carry
        s_next = logits(jnp.minimum(c + 1, n_chunks - 1))
        vt = vt_ref[c]
        out = []
        for kv in range(N_KV):
            m, l, acc = stats[kv]
            s = s_all[kv]
            m_new = jnp.maximum(m, jnp.max(s, axis=0, keepdims=True))
            alpha = jnp.exp2(m - m_new)
            p = jnp.exp2(s - m_new)
            l_new = alpha * l + jnp.sum(p, axis=0, keepdims=True)
            pv = _dot(vt[kv * HEAD_DIM:(kv + 1) * HEAD_DIM], p.astype(bf16))
            out.append((m_new, l_new, alpha * acc + pv))
        return s_next, tuple(out)

    init = tuple((jnp.full((1, nq), -jnp.inf, f32), jnp.zeros((1, nq), f32),
                  jnp.zeros((HEAD_DIM, nq), f32)) for _ in range(N_KV))
    _, final = lax.fori_loop(0, n_chunks, body, (logits(0), init))
    for kv in range(N_KV):
        _, l, acc = final[kv]
        _flash_store(ot_ref, kv, acc / l)


def _flash_bounded_kernel(qt_ref, pad_ref, k_ref, vt_ref, ot_ref):
    nq = GROUP * qt_ref.shape[1]
    n_chunks = vt_ref.shape[0]
    tk = vt_ref.shape[2]
    q_ext = _flash_queries(qt_ref, pad_ref[...])
    ones = jnp.ones((2 * SUBLANES, tk), bf16)
    rows = HEAD_DIM + 2 * SUBLANES

    units = [(c, kv) for c in range(n_chunks) for kv in range(N_KV)]

    def logits(c, kv):
        return _dot(k_ref[c * tk:(c + 1) * tk, :], q_ext[kv])

    s_queue = [logits(*u) for u in units[:FLASH_LOOKAHEAD]]
    acc = [None] * N_KV
    for i, (c, kv) in enumerate(units):
        if i + FLASH_LOOKAHEAD < len(units):
            s_queue.append(logits(*units[i + FLASH_LOOKAHEAD]))
        p = jnp.exp2(s_queue[i]).astype(bf16)
        s_queue[i] = None
        v_ext = jnp.concatenate([vt_ref[c, kv * HEAD_DIM:(kv + 1) * HEAD_DIM, :], ones], axis=0)
        pv = _dot(v_ext, p)
        acc[kv] = pv if acc[kv] is None else acc[kv] + pv
    for kv in range(N_KV):
        _flash_store(ot_ref, kv, acc[kv][0:HEAD_DIM] / acc[kv][HEAD_DIM:HEAD_DIM + 1])


def _flash_attn(qct, pad, kc, vct, seq, bounded):
    t = kc.shape[0]
    tq = TQ_FLASH
    q_tiles = seq // tq
    tk = vct.shape[2]
    q_spec = pl.BlockSpec((Q_WIDTH, tq), lambda b, i: (0, b * q_tiles + i))
    kv_specs = [
        pl.BlockSpec((seq, 2 * KV_WIDTH), lambda b, i: (b, 0)),
        pl.BlockSpec((seq // tk, KV_WIDTH, tk), lambda b, i: (b, 0, 0)),
    ]
    pad_spec = pl.BlockSpec((KV_WIDTH, GROUP * tq), lambda b, i: (0, 0))
    return pl.pallas_call(
        _flash_bounded_kernel if bounded else _flash_online_kernel,
        grid=(t // seq, q_tiles),
        in_specs=[q_spec] + ([pad_spec] if bounded else []) + kv_specs,
        out_specs=q_spec,
        out_shape=jax.ShapeDtypeStruct((Q_WIDTH, t), bf16),
        compiler_params=pltpu.CompilerParams(
            dimension_semantics=("parallel", "parallel"), vmem_limit_bytes=VMEM_LIMIT),
        name="flash_bounded" if bounded else "flash_online",
    )(*([qct] + ([pad] if bounded else []) + [kc, vct]))


def _merge_kernel(x_ref, g1_ref, oa_ref, obt_ref, oct_ref, wg_ref, bg_ref, wbr_ref, wo_ref,
                  y_ref):
    x = x_ref[...]
    xn = (x * _rms_scale(x) * g1_ref[...]).astype(bf16)
    projs = (_dot(oa_ref[...], wbr_ref[0]), _dot_tn(obt_ref[...], wbr_ref[1]),
             _dot_tn(oct_ref[...], wbr_ref[2]))
    logits = [_dot(xn, wg_ref[:, n * D_MODEL:(n + 1) * D_MODEL]) for n in range(N_BRANCH)]
    merged = None
    for n, proj in enumerate(projs):
        gate = _sigmoid(logits[n] + bg_ref[:, n * D_MODEL:(n + 1) * D_MODEL])
        term = gate * proj
        merged = term if merged is None else merged + term
    y_ref[...] = x + _dot(merged.astype(bf16), wo_ref[...])


def _merge(x2d, g1, oa, obt, oct, wg, bg, wbr, wo, layer):
    t = x2d.shape[0]
    tm = TM_MERGE
    const = lambda i: (0, 0)
    row = lambda width: pl.BlockSpec((tm, width), lambda i: (i, 0))
    return pl.pallas_call(
        _merge_kernel,
        grid=(t // tm,),
        in_specs=[
            row(D_MODEL),
            pl.BlockSpec((1, D_MODEL), const),
            row(BR_WIDTH),
            pl.BlockSpec((BR_WIDTH, tm), lambda i: (0, i)),
            pl.BlockSpec((BR_WIDTH, tm), lambda i: (0, i)),
            _resident((D_MODEL, N_BRANCH * D_MODEL), layer),
            pl.BlockSpec((1, N_BRANCH * D_MODEL), const),
            _resident((N_BRANCH, BR_WIDTH, D_MODEL), layer),
            _resident((D_MODEL, D_MODEL), layer),
        ],
        out_specs=row(D_MODEL),
        out_shape=jax.ShapeDtypeStruct((t, D_MODEL), f32),
        compiler_params=pltpu.CompilerParams(
            dimension_semantics=("parallel",), vmem_limit_bytes=VMEM_LIMIT),
        name="merge",
    )(x2d, g1, oa, obt, oct, wg, bg, wbr, wo)


def _conv_ffn_kernel(tiles_per_seq, apply_final, xp_ref, x_ref, xq_ref, g2_ref, wup_ref,
                     cw_ref, cb_ref, wdn_ref, gf_ref, y_ref, act_ref):
    tm = x_ref.shape[0]
    i = pl.program_id(0) % tiles_per_seq
    x = x_ref[...]
    halo_p = jnp.where(i > 0, xp_ref[...], 0.0)
    halo_n = jnp.where(i < tiles_per_seq - 1, xq_ref[...], 0.0)
    xe = jnp.concatenate([halo_p, x, halo_n], axis=0)
    xn = (xe * _rms_scale(xe) * g2_ref[...]).astype(bf16)
    rows = tm + 2 * SUBLANES
    n_chunks = D_FF // FF_CHUNK
    chunk_cols = lambda c: (slice(c * FF_CHUNK, (c + 1) * FF_CHUNK),
                            slice(D_FF + c * FF_CHUNK, D_FF + (c + 1) * FF_CHUNK))

    def up(c):
        return tuple(_dot(xn, wup_ref[:, cols]) for cols in chunk_cols(c))

    h_queue = [up(c) for c in range(FFN_LOOKAHEAD)]
    for c in range(n_chunks):
        if c + FFN_LOOKAHEAD < n_chunks:
            h_queue.append(up(c + FFN_LOOKAHEAD))
        parts = []
        for h, cols in zip(h_queue[c], chunk_cols(c)):
            hc = (pltpu.roll(h, 1, 0) * cw_ref[0:1, cols]
                  + h * cw_ref[1:2, cols]
                  + pltpu.roll(h, rows - 1, 0) * cw_ref[2:3, cols]
                  + cb_ref[:, cols])
            parts.append(hc[SUBLANES:SUBLANES + tm])
        h_queue[c] = None
        act_ref[:, chunk_cols(c)[0]] = (parts[0] * _sigmoid(parts[0]) * parts[1]).astype(bf16)
    y = x + _dot(act_ref[...], wdn_ref[...])
    if apply_final:
        y = y * _rms_scale(y) * gf_ref[...]
    y_ref[...] = y


def _conv_ffn(x2d, seq, g2, wup, cw, cb, wdn, gf, layer, apply_final):
    t = x2d.shape[0]
    tm = TM_FFN
    tiles_per_seq = seq // tm
    halo_per_tile = tm // SUBLANES
    n_halo = t // SUBLANES
    const = lambda i: (0, 0)
    return pl.pallas_call(
        functools.partial(_conv_ffn_kernel, tiles_per_seq, apply_final),
        grid=(t // tm,),
        in_specs=[
            pl.BlockSpec((SUBLANES, D_MODEL),
                         lambda i: (jnp.maximum(i * halo_per_tile - 1, 0), 0)),
            pl.BlockSpec((tm, D_MODEL), lambda i: (i, 0)),
            pl.BlockSpec((SUBLANES, D_MODEL),
                         lambda i: (jnp.minimum((i + 1) * halo_per_tile, n_halo - 1), 0)),
            pl.BlockSpec((1, D_MODEL), const),
            _resident((D_MODEL, 2 * D_FF), layer),
            pl.BlockSpec((3, 2 * D_FF), const),
            pl.BlockSpec((1, 2 * D_FF), const),
            _resident((D_FF, D_MODEL), layer),
            pl.BlockSpec((1, D_MODEL), const),
        ],
        out_specs=pl.BlockSpec((tm, D_MODEL), lambda i: (i, 0)),
        out_shape=jax.ShapeDtypeStruct((t, D_MODEL), f32),
        scratch_shapes=[pltpu.VMEM((tm, D_FF), bf16)],
        compiler_params=pltpu.CompilerParams(
            dimension_semantics=("parallel",), vmem_limit_bytes=VMEM_LIMIT),
        name="conv_ffn",
    )(x2d, x2d, x2d, g2, wup, cw, cb, wdn, gf)


def _t5_bucket(rel):
    half = N_BUCKETS // 2
    max_exact = half // 2
    ret = jnp.where(rel > 0, half, 0)
    n = jnp.abs(rel)
    nf = jnp.maximum(n, 1).astype(f32)
    large = max_exact + (jnp.log(nf / max_exact) / math.log(MAX_DIST / max_exact)
                         * (half - max_exact)).astype(jnp.int32)
    large = jnp.minimum(large, half - 1)
    return ret + jnp.where(n < max_exact, n, large)


def _band_buckets():
    jpos = jnp.arange(3 * BLOCK)[:, None]
    qpos = jnp.arange(BLOCK)[None, :]
    rel = jpos - BLOCK - qpos
    return jnp.where(jnp.abs(rel) <= WINDOW, _t5_bucket(rel), -1).astype(jnp.int32)


def _rope_tables(seq):
    m = HEAD_DIM // 4
    pos = jnp.arange(seq)
    row = (pos // GRID_W).astype(f32)
    col = (pos % GRID_W).astype(f32)
    inv = ROPE_THETA ** (-jnp.arange(m, dtype=f32) / m)
    ang_r = row[:, None] * inv[None, :]
    ang_c = col[:, None] * inv[None, :]
    cos = jnp.concatenate([jnp.cos(ang_r), jnp.cos(ang_r), jnp.cos(ang_c), jnp.cos(ang_c)], axis=-1)
    sin = jnp.concatenate([-jnp.sin(ang_r), jnp.sin(ang_r), -jnp.sin(ang_c), jnp.sin(ang_c)], axis=-1)
    reps = LANES // HEAD_DIM
    return jnp.tile(cos, (1, reps)), jnp.tile(sin, (1, reps)), cos.T, sin.T


def _trunk(x, layers, stacked, bias, seg, final_g):
    bsz, seq, d = x.shape
    x2d = x.reshape(bsz * seq, d)
    cos_t, sin_t, cos_tt, sin_tt = _rope_tables(seq)
    for l, p in enumerate(layers):
        oa, kb, kc, qbt, qct, vbt, vct = _in_proj(
            x2d, seq, p["g1"], p["w_in"], p["wqt"], p["wvt"], p["lng"], p["lnb"], p["wsp"],
            p["bsp"], p["qgt"], p["kg"], cos_t, sin_t, cos_tt, sin_tt, seg)
        obt = _band_attn(qbt, kb, vbt, seq, p["sink"], bias)
        oct = lax.cond(
            p["logit_bound"] <= MAX_LOGIT_BOUND,
            lambda qct, pad, kc, vct: _flash_attn(qct, pad, kc, vct, seq, bounded=True),
            lambda qct, pad, kc, vct: _flash_attn(qct, pad, kc, vct, seq, bounded=False),
            qct, p["pad"], kc, vct)
        x2d = _merge(x2d, p["g1"], oa, obt, oct, stacked["wg"], p["bg"], stacked["wbr"],
                     stacked["wo"], l)
        x2d = _conv_ffn(x2d, seq, p["g2"], stacked["wup"], p["cw"], p["cb"], stacked["wdn"],
                        final_g, l, apply_final=(l == len(layers) - 1))
    return x2d.reshape(bsz, seq, d)


def kernel(x_prompt, x_sample, rel_bias, norm1_g, w_in, ln_v_g, ln_v_b, w_spatial, b_spatial,
           sink, q_norm_g, k_norm_g, w_gate, b_gate, w_branch, w_out, norm2_g, w_up, conv_w,
           conv_b, w_down, final_g):
    bias = _band_bias(rel_bias, _band_buckets())
    head_of_lane = jnp.arange(LANES) // HEAD_DIM
    seg = ((head_of_lane[:, None] == head_of_lane[None, :]).astype(f32) / HEAD_DIM).astype(bf16)
    reps = LANES // HEAD_DIM
    layers = []
    for l in range(DEPTH):
        ws = w_spatial[l].astype(bf16)
        wsp = jnp.concatenate([ws[0::2], ws[1::2]], axis=-1)
        wl = w_in[l].astype(bf16)
        b_q, b_k, b_v = A_IN, A_IN + Q_WIDTH, A_IN + Q_WIDTH + KV_WIDTH
        c_q, c_k, c_v = (b + QKV_WIDTH for b in (b_q, b_k, b_v))
        logit_bound = (1.02 * HEAD_DIM ** 0.5) * jnp.max(jnp.abs(q_norm_g[l])) * jnp.max(
            jnp.abs(k_norm_g[l]))
        shift = -(logit_bound * (LOG2E * (1.0 + 2.0 ** -7))).astype(bf16)
        pad = jnp.zeros((KV_WIDTH, GROUP * TQ_FLASH), bf16).at[0, :].set(shift)
        layers.append(dict(
            logit_bound=logit_bound,
            pad=pad,
            g1=norm1_g[l][None, :],
            w_in=jnp.concatenate(
                [wl[:, :A_IN], wl[:, b_k:b_k + KV_WIDTH], wl[:, c_k:c_k + KV_WIDTH]], axis=1),
            wqt=jnp.concatenate([wl[:, b_q:b_q + Q_WIDTH], wl[:, c_q:c_q + Q_WIDTH]], axis=1).T,
            wvt=jnp.concatenate([wl[:, b_v:b_v + KV_WIDTH], wl[:, c_v:c_v + KV_WIDTH]], axis=1).T,
            lng=ln_v_g[l][None, :],
            lnb=ln_v_b[l][None, :],
            wsp=wsp,
            bsp=jnp.repeat(b_spatial[l].T, HEAD_DIM, axis=1),
            qgt=jnp.broadcast_to(q_norm_g[l][:, None], (HEAD_DIM, TM_IN)),
            kg=jnp.tile(k_norm_g[l], reps)[None, :],
            sink=jnp.repeat(sink[l] * LOG2E, BLOCK).reshape(N_KV, 1, GROUP * BLOCK),
            bg=b_gate[l][None, :],
            g2=norm2_g[l][None, :],
            cw=conv_w[l],
            cb=conv_b[l][None, :],
        ))
    stacked = dict(wg=w_gate.astype(bf16), wbr=w_branch.astype(bf16), wo=w_out.astype(bf16),
                   wup=w_up.astype(bf16), wdn=w_down.astype(bf16))
    gf = final_g[None, :]
    y_prompt = _trunk(x_prompt, layers, stacked, bias, seg, gf)
    y_sample = _trunk(x_sample, layers, stacked, bias, seg, gf)
    return (y_prompt, y_sample)
```

```python
import functools
import math

import jax
import jax.numpy as jnp
from jax import lax
from jax.experimental import pallas as pl
from jax.experimental.pallas import tpu as pltpu

D_MODEL = 1024
DEPTH = 2
HEAD_DIM = 64
BLOCK = 128
A_GROUPS = 8
A_WIDTH = A_GROUPS * HEAD_DIM
N_HEADS = 8
N_KV = 2
GROUP = N_HEADS // N_KV
WINDOW = 128
ROPE_THETA = 10000.0
GRID_W = 64
N_BUCKETS = 32
MAX_DIST = 128
D_FF = 2816
EPS = 1e-6
N_BRANCH = 3
BR_WIDTH = 512
Q_WIDTH = N_HEADS * HEAD_DIM
KV_WIDTH = N_KV * HEAD_DIM
QKV_WIDTH = Q_WIDTH + 2 * KV_WIDTH
A_IN = 2 * A_WIDTH

LANES = 128
SUBLANES = 8
VMEM_LIMIT = 56 * 1024 * 1024

TM_IN = 512
TM_MERGE = 512
TM_FFN = 512
FF_CHUNK = 256
FFN_LOOKAHEAD = 2
BAND_R = 4
BAND_LOOKAHEAD = 2
PV_ROWS = 128
TQ_FLASH = 128
TK_FLASH = 256
FLASH_LOOKAHEAD = 3
LOG2E = math.log2(math.e)
MAX_LOGIT_BOUND = 30.0

f32 = jnp.float32
bf16 = jnp.bfloat16


def _resident(shape, layer=None):
    if layer is None:
        return pl.BlockSpec(shape, lambda *_: (0,) * len(shape), pipeline_mode=pl.Buffered(1))
    return pl.BlockSpec((None,) + tuple(shape), lambda *_: (layer,) + (0,) * len(shape),
                        pipeline_mode=pl.Buffered(1))


def _zero_like(x):
    bits = pltpu.bitcast(x, jnp.uint32)
    return pltpu.bitcast((bits >> 16) >> 16, f32)


def _rms_scale(x):
    return lax.rsqrt(jnp.mean(x * x, axis=-1, keepdims=True) + EPS)


def _gelu_tanh(x):
    c = math.sqrt(2.0 / math.pi)
    return x * (0.5 * (1.0 + jnp.tanh(c * (x + 0.044715 * (x * x * x)))))


def _sigmoid(x):
    return 1.0 / (1.0 + jnp.exp(-x))


def _dot(a, b):
    return jnp.dot(a, b, preferred_element_type=f32)


def _dot_tn(a, b):
    return lax.dot_general(a, b, (((0,), (0,)), ((), ())), preferred_element_type=f32)


def _dot_nt(a, b):
    return lax.dot_general(a, b, (((1,), (1,)), ((), ())), preferred_element_type=f32)


def _head_mean_sq(x, seg_ref):
    sq = x * x
    hi = sq.astype(bf16)
    lo = (sq - hi.astype(f32)).astype(bf16)
    seg = seg_ref[...]
    return _dot(hi, seg) + _dot(lo, seg)


def _swap16(x):
    lane = lax.broadcasted_iota(jnp.int32, x.shape, 1)
    first_half = (lane % 32) < 16
    return jnp.where(first_half, pltpu.roll(x, LANES - 16, 1), pltpu.roll(x, 16, 1))


def _swap16_rows(x):
    h = HEAD_DIM // 4
    return jnp.concatenate([x[h:2 * h], x[0:h], x[3 * h:4 * h], x[2 * h:3 * h]], axis=0)


def _in_proj_kernel(x_ref, g1_ref, w_ref, wqt_ref, wvt_ref, lng_ref, lnb_ref, wsp_ref, bsp_ref,
                    qgt_ref, kg_ref, cos_ref, sin_ref, cost_ref, sint_ref, seg_ref,
                    oa_ref, kb_ref, kc_ref, qbt_ref, qct_ref, vbt_ref, vct_ref):
    tm = x_ref.shape[0]
    x = x_ref[...]
    xn = (x * _rms_scale(x) * g1_ref[...]).astype(bf16)

    zv = _dot(xn, w_ref[:, A_WIDTH:A_IN])
    zu = _dot(xn, w_ref[:, 0:A_WIDTH])
    zk = _dot(xn, w_ref[:, A_IN:A_IN + 2 * KV_WIDTH])
    qbt = _dot_nt(wqt_ref[0:Q_WIDTH, :], xn)
    qt = _dot_nt(wqt_ref[Q_WIDTH:2 * Q_WIDTH, :], xn)
    vt = _dot_nt(wvt_ref[...], xn).astype(bf16)

    v = _gelu_tanh(zv)
    u = _gelu_tanh(zu)
    mu = jnp.mean(v, axis=-1, keepdims=True)
    vc = v - mu
    vn = vc * lax.rsqrt(jnp.mean(vc * vc, axis=-1, keepdims=True) + EPS)
    vn = vn * lng_ref[...] + lnb_ref[...]
    lane = lax.broadcasted_iota(jnp.int32, (BLOCK, LANES), 1)
    low_half = lane < HEAD_DIM
    for c in range(tm // BLOCK):
        rows = slice(c * BLOCK, (c + 1) * BLOCK)
        for j in range(A_WIDTH // LANES):
            cols = slice(j * LANES, (j + 1) * LANES)
            vp = vn[rows, cols]
            stacked = jnp.concatenate(
                [jnp.where(low_half, vp, 0.0), jnp.where(low_half, 0.0, vp)], axis=0)
            sv = _dot(wsp_ref[j], stacked.astype(bf16)) + bsp_ref[:, cols]
            oa_ref[rows, cols] = (u[rows, cols] * sv).astype(bf16)

    kb_ref[...] = zk[:, 0:KV_WIDTH].astype(bf16)

    t = zk[:, KV_WIDTH:2 * KV_WIDTH]
    tn = t * lax.rsqrt(_head_mean_sq(t, seg_ref) + EPS) * kg_ref[...]
    kc_ref[:, 0:KV_WIDTH] = (tn * cos_ref[...] + _swap16(tn) * sin_ref[...]).astype(bf16)
    lane = lax.broadcasted_iota(jnp.int32, (tm, KV_WIDTH), 1)
    kc_ref[:, KV_WIDTH:2 * KV_WIDTH] = jnp.where(lane == 0, 1.0, 0.0).astype(bf16)

    scale = HEAD_DIM ** -0.5 * LOG2E
    qbt_ref[...] = (qbt * scale).astype(bf16)
    cost = cost_ref[...]
    sint = sint_ref[...]
    for h in range(N_HEADS):
        rows = slice(h * HEAD_DIM, (h + 1) * HEAD_DIM)
        th = qt[rows]
        r = lax.rsqrt(jnp.mean(th * th, axis=0, keepdims=True) + EPS)
        tn = th * r * qgt_ref[...]
        qct_ref[rows, :] = ((tn * cost + _swap16_rows(tn) * sint) * scale).astype(bf16)
    for c in range(tm // BLOCK):
        vbt_ref[c] = vt[0:KV_WIDTH, c * BLOCK:(c + 1) * BLOCK]
    for c in range(tm // TK_FLASH):
        vct_ref[c] = vt[KV_WIDTH:2 * KV_WIDTH, c * TK_FLASH:(c + 1) * TK_FLASH]


def _in_proj(x2d, seq, g1, w_in, wqt, wvt, lng, lnb, wsp, bsp, qgt, kg, cos_t, sin_t,
             cos_tt, sin_tt, seg):
    t = x2d.shape[0]
    tm = TM_IN
    tiles_per_seq = seq // tm
    const = lambda i: (0, 0)
    return pl.pallas_call(
        _in_proj_kernel,
        grid=(t // tm,),
        in_specs=[
            pl.BlockSpec((tm, D_MODEL), lambda i: (i, 0)),
            pl.BlockSpec((1, D_MODEL), const),
            _resident((D_MODEL, w_in.shape[1])),
            _resident((2 * Q_WIDTH, D_MODEL)),
            _resident((2 * KV_WIDTH, D_MODEL)),
            pl.BlockSpec((1, A_WIDTH), const),
            pl.BlockSpec((1, A_WIDTH), const),
            pl.BlockSpec((A_WIDTH // LANES, BLOCK, 2 * BLOCK), lambda i: (0, 0, 0)),
            pl.BlockSpec((BLOCK, A_WIDTH), const),
            pl.BlockSpec((HEAD_DIM, tm), const),
            pl.BlockSpec((1, LANES), const),
            pl.BlockSpec((tm, LANES), lambda i: (i % tiles_per_seq, 0)),
            pl.BlockSpec((tm, LANES), lambda i: (i % tiles_per_seq, 0)),
            pl.BlockSpec((HEAD_DIM, tm), lambda i: (0, i % tiles_per_seq)),
            pl.BlockSpec((HEAD_DIM, tm), lambda i: (0, i % tiles_per_seq)),
            pl.BlockSpec((LANES, LANES), const),
        ],
        out_specs=[
            pl.BlockSpec((tm, A_WIDTH), lambda i: (i, 0)),
            pl.BlockSpec((tm, KV_WIDTH), lambda i: (i, 0)),
            pl.BlockSpec((tm, 2 * KV_WIDTH), lambda i: (i, 0)),
            pl.BlockSpec((Q_WIDTH, tm), lambda i: (0, i)),
            pl.BlockSpec((Q_WIDTH, tm), lambda i: (0, i)),
            pl.BlockSpec((tm // BLOCK, KV_WIDTH, BLOCK), lambda i: (i, 0, 0)),
            pl.BlockSpec((tm // TK_FLASH, KV_WIDTH, TK_FLASH), lambda i: (i, 0, 0)),
        ],
        out_shape=[
            jax.ShapeDtypeStruct((t, A_WIDTH), bf16),
            jax.ShapeDtypeStruct((t, KV_WIDTH), bf16),
            jax.ShapeDtypeStruct((t, 2 * KV_WIDTH), bf16),
            jax.ShapeDtypeStruct((Q_WIDTH, t), bf16),
            jax.ShapeDtypeStruct((Q_WIDTH, t), bf16),
            jax.ShapeDtypeStruct((t // BLOCK, KV_WIDTH, BLOCK), bf16),
            jax.ShapeDtypeStruct((t // TK_FLASH, KV_WIDTH, TK_FLASH), bf16),
        ],
        compiler_params=pltpu.CompilerParams(
            dimension_semantics=("parallel",), vmem_limit_bytes=VMEM_LIMIT),
        name="in_proj",
    )(x2d, g1, w_in, wqt, wvt, lng, lnb, wsp, bsp, qgt, kg, cos_t, sin_t, cos_tt, sin_tt, seg)


def _band_bias_kernel(rel_bias_ref, bucket_ref, bias_ref):
    bucket = bucket_ref[...]
    for h in range(N_HEADS):
        val = jnp.full(bucket.shape, -jnp.inf, f32)
        for b in range(N_BUCKETS):
            val = jnp.where(bucket == b, rel_bias_ref[b, h] * LOG2E, val)
        gi = h % GROUP
        bias_ref[h // GROUP, :, gi * BLOCK:(gi + 1) * BLOCK] = val


def _band_bias(rel_bias, bucket_t):
    return pl.pallas_call(
        _band_bias_kernel,
        in_specs=[pl.BlockSpec(memory_space=pltpu.SMEM),
                  pl.BlockSpec((3 * BLOCK, BLOCK), lambda: (0, 0))],
        out_specs=pl.BlockSpec((N_KV, 3 * BLOCK, GROUP * BLOCK), lambda: (0, 0, 0)),
        out_shape=jax.ShapeDtypeStruct((N_KV, 3 * BLOCK, GROUP * BLOCK), f32),
        name="band_bias",
    )(rel_bias, bucket_t)


def _band_attn_kernel(nb, qt_ref, kp_ref, km_ref, kn_ref, vp_ref, vm_ref, vn_ref,
                      bias_ref, sink_ref, ot_ref):
    first = (pl.program_id(0) * BAND_R) % nb == 0
    last = ((pl.program_id(0) + 1) * BAND_R) % nb == 0
    k_win = jnp.concatenate([kp_ref[...], km_ref[...], kn_ref[...]], axis=0)
    v_blocks = [vp_ref[0]] + [vm_ref[r] for r in range(BAND_R)] + [vn_ref[0]]
    ones = jnp.concatenate([jnp.ones((2 * SUBLANES, 3 * BLOCK), bf16),
                            jnp.zeros((PV_ROWS - HEAD_DIM - 2 * SUBLANES, 3 * BLOCK), bf16)], axis=0)
    zeros = jnp.zeros((HEAD_DIM, GROUP * BLOCK), bf16)
    units = [(r, kv) for r in range(BAND_R) for kv in range(N_KV)]

    def logits(r, kv):
        qg = jnp.concatenate(
            [qt_ref[h * HEAD_DIM:(h + 1) * HEAD_DIM, r * BLOCK:(r + 1) * BLOCK]
             for h in range(kv * GROUP, (kv + 1) * GROUP)], axis=1)
        q_ext = jnp.concatenate([qg, zeros] if kv == 0 else [zeros, qg], axis=0)
        return _dot(k_win[r * BLOCK:(r + 3) * BLOCK], q_ext)

    s_queue = [logits(*u) for u in units[:BAND_LOOKAHEAD]]
    for i, (r, kv) in enumerate(units):
        if i + BAND_LOOKAHEAD < len(units):
            s_queue.append(logits(*units[i + BAND_LOOKAHEAD]))
        s = s_queue[i] + bias_ref[kv]
        s_queue[i] = None
        if r == 0:
            s = jnp.concatenate([jnp.where(first, -jnp.inf, s[0:BLOCK]), s[BLOCK:]], axis=0)
        if r == BAND_R - 1:
            s = jnp.concatenate([s[:2 * BLOCK], jnp.where(last, -jnp.inf, s[2 * BLOCK:])], axis=0)
        sink = sink_ref[kv]
        if i + 1 < len(units):
            sink = sink + _zero_like(s_queue[i + 1][0:1, :])
        m = jnp.maximum(jnp.max(s, axis=0, keepdims=True), sink)
        p = jnp.exp2(s - m).astype(bf16)
        v_band = jnp.concatenate(
            [blk[kv * HEAD_DIM:(kv + 1) * HEAD_DIM] for blk in v_blocks[r:r + 3]], axis=1)
        pv = _dot(jnp.concatenate([v_band, ones], axis=0), p)
        out = pv[0:HEAD_DIM] / (pv[HEAD_DIM:HEAD_DIM + 1] + jnp.exp2(sink - m))
        for gi in range(GROUP):
            h = kv * GROUP + gi
            ot_ref[h * HEAD_DIM:(h + 1) * HEAD_DIM, r * BLOCK:(r + 1) * BLOCK] = (
                out[:, gi * BLOCK:(gi + 1) * BLOCK].astype(bf16))


def _band_attn(qbt, kb, vbt, seq, sink_rows, bias_t):
    t = kb.shape[0]
    nb = seq // BLOCK
    nblocks = t // BLOCK
    r = BAND_R
    prev_blk = lambda i: jnp.maximum(i * r - 1, 0)
    next_blk = lambda i: jnp.minimum((i + 1) * r, nblocks - 1)
    return pl.pallas_call(
        functools.partial(_band_attn_kernel, nb),
        grid=(nblocks // r,),
        in_specs=[
            pl.BlockSpec((Q_WIDTH, r * BLOCK), lambda i: (0, i)),
            pl.BlockSpec((BLOCK, KV_WIDTH), lambda i: (prev_blk(i), 0)),
            pl.BlockSpec((r * BLOCK, KV_WIDTH), lambda i: (i, 0)),
            pl.BlockSpec((BLOCK, KV_WIDTH), lambda i: (next_blk(i), 0)),
            pl.BlockSpec((1, KV_WIDTH, BLOCK), lambda i: (prev_blk(i), 0, 0)),
            pl.BlockSpec((r, KV_WIDTH, BLOCK), lambda i: (i, 0, 0)),
            pl.BlockSpec((1, KV_WIDTH, BLOCK), lambda i: (next_blk(i), 0, 0)),
            pl.BlockSpec((N_KV, 3 * BLOCK, GROUP * BLOCK), lambda i: (0, 0, 0)),
            pl.BlockSpec((N_KV, 1, GROUP * BLOCK), lambda i: (0, 0, 0)),
        ],
        out_specs=pl.BlockSpec((Q_WIDTH, r * BLOCK), lambda i: (0, i)),
        out_shape=jax.ShapeDtypeStruct((Q_WIDTH, t), bf16),
        compiler_params=pltpu.CompilerParams(
            dimension_semantics=("parallel",), vmem_limit_bytes=VMEM_LIMIT),
        name="band_attn",
    )(qbt, kb, kb, kb, vbt, vbt, vbt, bias_t, sink_rows)


def _flash_queries(qt_ref, pad):
    tq = qt_ref.shape[1]
    zeros = jnp.zeros((HEAD_DIM, GROUP * tq), bf16)
    q_ext = []
    for kv in range(N_KV):
        qg = jnp.concatenate(
            [qt_ref[h * HEAD_DIM:(h + 1) * HEAD_DIM, :]
             for h in range(kv * GROUP, (kv + 1) * GROUP)], axis=1)
        q_ext.append(jnp.concatenate(([qg, zeros] if kv == 0 else [zeros, qg]) + [pad], axis=0))
    return q_ext


def _flash_store(ot_ref, kv, out):
    tq = ot_ref.shape[1]
    for gi in range(GROUP):
        h = kv * GROUP + gi
        ot_ref[h * HEAD_DIM:(h + 1) * HEAD_DIM, :] = out[:, gi * tq:(gi + 1) * tq].astype(bf16)


def _flash_online_kernel(qt_ref, k_ref, vt_ref, ot_ref):
    nq = GROUP * qt_ref.shape[1]
    n_chunks = vt_ref.shape[0]
    tk = vt_ref.shape[2]
    q_ext = _flash_queries(qt_ref, jnp.zeros((KV_WIDTH, nq), bf16))

    def logits(c):
        start = pl.multiple_of(c * tk, tk)
        kc = k_ref[pl.ds(start, tk), :]
        return tuple(_dot(kc, q_ext[kv]) for kv in range(N_KV))

    def body(c, carry):
        s_all, stats = carry
        s_next = logits(jnp.minimum(c + 1, n_chunks - 1))
        vt = vt_ref[c]
        out = []
        for kv in range(N_KV):
            m, l, acc = stats[kv]
            s = s_all[kv]
            m_new = jnp.maximum(m, jnp.max(s, axis=0, keepdims=True))
            alpha = jnp.exp2(m - m_new)
            p = jnp.exp2(s - m_new)
            l_new = alpha * l + jnp.sum(p, axis=0, keepdims=True)
            pv = _dot(vt[kv * HEAD_DIM:(kv + 1) * HEAD_DIM], p.astype(bf16))
            out.append((m_new, l_new, alpha * acc + pv))
        return s_next, tuple(out)

    init = tuple((jnp.full((1, nq), -jnp.inf, f32), jnp.zeros((1, nq), f32),
                  jnp.zeros((HEAD_DIM, nq), f32)) for _ in range(N_KV))
    _, final = lax.fori_loop(0, n_chunks, body, (logits(0), init))
    for kv in range(N_KV):
        _, l, acc = final[kv]
        _flash_store(ot_ref, kv, acc / l)


def _flash_bounded_kernel(qt_ref, pad_ref, k_ref, vt_ref, ot_ref):
    nq = GROUP * qt_ref.shape[1]
    n_chunks = vt_ref.shape[0]
    tk = vt_ref.shape[2]
    q_ext = _flash_queries(qt_ref, pad_ref[...])
    ones = jnp.concatenate([jnp.ones((2 * SUBLANES, tk), bf16),
                            jnp.zeros((PV_ROWS - HEAD_DIM - 2 * SUBLANES, tk), bf16)], axis=0)

    units = [(c, kv) for c in range(n_chunks) for kv in range(N_KV)]

    def logits(c, kv):
        return _dot(k_ref[c * tk:(c + 1) * tk, :], q_ext[kv])

    s_queue = [logits(*u) for u in units[:FLASH_LOOKAHEAD]]
    acc = [None] * N_KV
    for i, (c, kv) in enumerate(units):
        if i + FLASH_LOOKAHEAD < len(units):
            s_queue.append(logits(*units[i + FLASH_LOOKAHEAD]))
        p = jnp.exp2(s_queue[i]).astype(bf16)
        s_queue[i] = None
        v_ext = jnp.concatenate([vt_ref[c, kv * HEAD_DIM:(kv + 1) * HEAD_DIM, :], ones], axis=0)
        pv = _dot(v_ext, p)
        acc[kv] = pv if acc[kv] is None else acc[kv] + pv
    for kv in range(N_KV):
        _flash_store(ot_ref, kv, acc[kv][0:HEAD_DIM] / acc[kv][HEAD_DIM:HEAD_DIM + 1])


def _flash_attn(qct, pad, kc, vct, seq, bounded):
    t = kc.shape[0]
    tq = TQ_FLASH
    q_tiles = seq // tq
    tk = vct.shape[2]
    q_spec = pl.BlockSpec((Q_WIDTH, tq), lambda b, i: (0, b * q_tiles + i))
    kv_specs = [
        pl.BlockSpec((seq, 2 * KV_WIDTH), lambda b, i: (b, 0)),
        pl.BlockSpec((seq // tk, KV_WIDTH, tk), lambda b, i: (b, 0, 0)),
    ]
    pad_spec = pl.BlockSpec((KV_WIDTH, GROUP * tq), lambda b, i: (0, 0))
    return pl.pallas_call(
        _flash_bounded_kernel if bounded else _flash_online_kernel,
        grid=(t // seq, q_tiles),
        in_specs=[q_spec] + ([pad_spec] if bounded else []) + kv_specs,
        out_specs=q_spec,
        out_shape=jax.ShapeDtypeStruct((Q_WIDTH, t), bf16),
        compiler_params=pltpu.CompilerParams(
            dimension_semantics=("parallel", "parallel"), vmem_limit_bytes=VMEM_LIMIT),
        name="flash_bounded" if bounded else "flash_online",
    )(*([qct] + ([pad] if bounded else []) + [kc, vct]))


def _merge_kernel(x_ref, g1_ref, oa_ref, obt_ref, oct_ref, wg_ref, bg_ref, wbr_ref, wo_ref,
                  y_ref):
    x = x_ref[...]
    xn = (x * _rms_scale(x) * g1_ref[...]).astype(bf16)
    projs = (_dot(oa_ref[...], wbr_ref[0]), _dot_tn(obt_ref[...], wbr_ref[1]),
             _dot_tn(oct_ref[...], wbr_ref[2]))
    logits = [_dot(xn, wg_ref[:, n * D_MODEL:(n + 1) * D_MODEL]) for n in range(N_BRANCH)]
    merged = None
    for n, proj in enumerate(projs):
        gate = _sigmoid(logits[n] + bg_ref[:, n * D_MODEL:(n + 1) * D_MODEL])
        term = gate * proj
        merged = term if merged is None else merged + term
    y_ref[...] = x + _dot(merged.astype(bf16), wo_ref[...])


def _merge(x2d, g1, oa, obt, oct, wg, bg, wbr, wo, layer):
    t = x2d.shape[0]
    tm = TM_MERGE
    const = lambda i: (0, 0)
    row = lambda width: pl.BlockSpec((tm, width), lambda i: (i, 0))
    return pl.pallas_call(
        _merge_kernel,
        grid=(t // tm,),
        in_specs=[
            row(D_MODEL),
            pl.BlockSpec((1, D_MODEL), const),
            row(BR_WIDTH),
            pl.BlockSpec((BR_WIDTH, tm), lambda i: (0, i)),
            pl.BlockSpec((BR_WIDTH, tm), lambda i: (0, i)),
            _resident((D_MODEL, N_BRANCH * D_MODEL), layer),
            pl.BlockSpec((1, N_BRANCH * D_MODEL), const),
            _resident((N_BRANCH, BR_WIDTH, D_MODEL), layer),
            _resident((D_MODEL, D_MODEL), layer),
        ],
        out_specs=row(D_MODEL),
        out_shape=jax.ShapeDtypeStruct((t, D_MODEL), f32),
        compiler_params=pltpu.CompilerParams(
            dimension_semantics=("parallel",), vmem_limit_bytes=VMEM_LIMIT),
        name="merge",
    )(x2d, g1, oa, obt, oct, wg, bg, wbr, wo)


def _conv_ffn_kernel(tiles_per_seq, apply_final, xp_ref, x_ref, xq_ref, g2_ref, wup_ref,
                     cw_ref, cb_ref, wdn_ref, gf_ref, y_ref, act_ref):
    tm = x_ref.shape[0]
    i = pl.program_id(0) % tiles_per_seq
    x = x_ref[...]
    halo_p = jnp.where(i > 0, xp_ref[...], 0.0)
    halo_n = jnp.where(i < tiles_per_seq - 1, xq_ref[...], 0.0)
    xe = jnp.concatenate([halo_p, x, halo_n], axis=0)
    xn = (xe * _rms_scale(xe) * g2_ref[...]).astype(bf16)
    rows = tm + 2 * SUBLANES
    n_chunks = D_FF // FF_CHUNK
    chunk_cols = lambda c: (slice(c * FF_CHUNK, (c + 1) * FF_CHUNK),
                            slice(D_FF + c * FF_CHUNK, D_FF + (c + 1) * FF_CHUNK))

    def up(c):
        return tuple(_dot(xn, wup_ref[:, cols]) for cols in chunk_cols(c))

    h_queue = [up(c) for c in range(FFN_LOOKAHEAD)]
    for c in range(n_chunks):
        if c + FFN_LOOKAHEAD < n_chunks:
            h_queue.append(up(c + FFN_LOOKAHEAD))
        parts = []
        for h, cols in zip(h_queue[c], chunk_cols(c)):
            hc = (pltpu.roll(h, 1, 0) * cw_ref[0:1, cols]
                  + h * cw_ref[1:2, cols]
                  + pltpu.roll(h, rows - 1, 0) * cw_ref[2:3, cols]
                  + cb_ref[:, cols])
            parts.append(hc[SUBLANES:SUBLANES + tm])
        h_queue[c] = None
        act_ref[:, chunk_cols(c)[0]] = (parts[0] * _sigmoid(parts[0]) * parts[1]).astype(bf16)
    y = x + _dot(act_ref[...], wdn_ref[...])
    if apply_final:
        y = y * _rms_scale(y) * gf_ref[...]
    y_ref[...] = y


def _conv_ffn(x2d, seq, g2, wup, cw, cb, wdn, gf, layer, apply_final):
    t = x2d.shape[0]
    tm = TM_FFN
    tiles_per_seq = seq // tm
    halo_per_tile = tm // SUBLANES
    n_halo = t // SUBLANES
    const = lambda i: (0, 0)
    return pl.pallas_call(
        functools.partial(_conv_ffn_kernel, tiles_per_seq, apply_final),
        grid=(t // tm,),
        in_specs=[
            pl.BlockSpec((SUBLANES, D_MODEL),
                         lambda i: (jnp.maximum(i * halo_per_tile - 1, 0), 0)),
            pl.BlockSpec((tm, D_MODEL), lambda i: (i, 0)),
            pl.BlockSpec((SUBLANES, D_MODEL),
                         lambda i: (jnp.minimum((i + 1) * halo_per_tile, n_halo - 1), 0)),
            pl.BlockSpec((1, D_MODEL), const),
            _resident((D_MODEL, 2 * D_FF), layer),
            pl.BlockSpec((3, 2 * D_FF), const),
            pl.BlockSpec((1, 2 * D_FF), const),
            _resident((D_FF, D_MODEL), layer),
            pl.BlockSpec((1, D_MODEL), const),
        ],
        out_specs=pl.BlockSpec((tm, D_MODEL), lambda i: (i, 0)),
        out_shape=jax.ShapeDtypeStruct((t, D_MODEL), f32),
        scratch_shapes=[pltpu.VMEM((tm, D_FF), bf16)],
        compiler_params=pltpu.CompilerParams(
            dimension_semantics=("parallel",), vmem_limit_bytes=VMEM_LIMIT),
        name="conv_ffn",
    )(x2d, x2d, x2d, g2, wup, cw, cb, wdn, gf)


def _t5_bucket(rel):
    half = N_BUCKETS // 2
    max_exact = half // 2
    ret = jnp.where(rel > 0, half, 0)
    n = jnp.abs(rel)
    nf = jnp.maximum(n, 1).astype(f32)
    large = max_exact + (jnp.log(nf / max_exact) / math.log(MAX_DIST / max_exact)
                         * (half - max_exact)).astype(jnp.int32)
    large = jnp.minimum(large, half - 1)
    return ret + jnp.where(n < max_exact, n, large)


def _band_buckets():
    jpos = jnp.arange(3 * BLOCK)[:, None]
    qpos = jnp.arange(BLOCK)[None, :]
    rel = jpos - BLOCK - qpos
    return jnp.where(jnp.abs(rel) <= WINDOW, _t5_bucket(rel), -1).astype(jnp.int32)


def _rope_tables(seq):
    m = HEAD_DIM // 4
    pos = jnp.arange(seq)
    row = (pos // GRID_W).astype(f32)
    col = (pos % GRID_W).astype(f32)
    inv = ROPE_THETA ** (-jnp.arange(m, dtype=f32) / m)
    ang_r = row[:, None] * inv[None, :]
    ang_c = col[:, None] * inv[None, :]
    cos = jnp.concatenate([jnp.cos(ang_r), jnp.cos(ang_r), jnp.cos(ang_c), jnp.cos(ang_c)], axis=-1)
    sin = jnp.concatenate([-jnp.sin(ang_r), jnp.sin(ang_r), -jnp.sin(ang_c), jnp.sin(ang_c)], axis=-1)
    reps = LANES // HEAD_DIM
    return jnp.tile(cos, (1, reps)), jnp.tile(sin, (1, reps)), cos.T, sin.T


def _trunk(x, layers, stacked, bias, seg, final_g):
    bsz, seq, d = x.shape
    x2d = x.reshape(bsz * seq, d)
    cos_t, sin_t, cos_tt, sin_tt = _rope_tables(seq)
    for l, p in enumerate(layers):
        oa, kb, kc, qbt, qct, vbt, vct = _in_proj(
            x2d, seq, p["g1"], p["w_in"], p["wqt"], p["wvt"], p["lng"], p["lnb"], p["wsp"],
            p["bsp"], p["qgt"], p["kg"], cos_t, sin_t, cos_tt, sin_tt, seg)
        obt = _band_attn(qbt, kb, vbt, seq, p["sink"], bias)
        oct = lax.cond(
            p["logit_bound"] <= MAX_LOGIT_BOUND,
            lambda qct, pad, kc, vct: _flash_attn(qct, pad, kc, vct, seq, bounded=True),
            lambda qct, pad, kc, vct: _flash_attn(qct, pad, kc, vct, seq, bounded=False),
            qct, p["pad"], kc, vct)
        x2d = _merge(x2d, p["g1"], oa, obt, oct, stacked["wg"], p["bg"], stacked["wbr"],
                     stacked["wo"], l)
        x2d = _conv_ffn(x2d, seq, p["g2"], stacked["wup"], p["cw"], p["cb"], stacked["wdn"],
                        final_g, l, apply_final=(l == len(layers) - 1))
    return x2d.reshape(bsz, seq, d)


def kernel(x_prompt, x_sample, rel_bias, norm1_g, w_in, ln_v_g, ln_v_b, w_spatial, b_spatial,
           sink, q_norm_g, k_norm_g, w_gate, b_gate, w_branch, w_out, norm2_g, w_up, conv_w,
           conv_b, w_down, final_g):
    bias = _band_bias(rel_bias, _band_buckets())
    head_of_lane = jnp.arange(LANES) // HEAD_DIM
    seg = ((head_of_lane[:, None] == head_of_lane[None, :]).astype(f32) / HEAD_DIM).astype(bf16)
    reps = LANES // HEAD_DIM
    layers = []
    for l in range(DEPTH):
        ws = w_spatial[l].astype(bf16)
        wsp = jnp.concatenate([ws[0::2], ws[1::2]], axis=-1)
        wl = w_in[l].astype(bf16)
        b_q, b_k, b_v = A_IN, A_IN + Q_WIDTH, A_IN + Q_WIDTH + KV_WIDTH
        c_q, c_k, c_v = (b + QKV_WIDTH for b in (b_q, b_k, b_v))
        logit_bound = (1.02 * HEAD_DIM ** 0.5) * jnp.max(jnp.abs(q_norm_g[l])) * jnp.max(
            jnp.abs(k_norm_g[l]))
        shift = -(logit_bound * (LOG2E * (1.0 + 2.0 ** -7))).astype(bf16)
        pad = jnp.zeros((KV_WIDTH, GROUP * TQ_FLASH), bf16).at[0, :].set(shift)
        layers.append(dict(
            logit_bound=logit_bound,
            pad=pad,
            g1=norm1_g[l][None, :],
            w_in=jnp.concatenate(
                [wl[:, :A_IN], wl[:, b_k:b_k + KV_WIDTH], wl[:, c_k:c_k + KV_WIDTH]], axis=1),
            wqt=jnp.concatenate([wl[:, b_q:b_q + Q_WIDTH], wl[:, c_q:c_q + Q_WIDTH]], axis=1).T,
            wvt=jnp.concatenate([wl[:, b_v:b_v + KV_WIDTH], wl[:, c_v:c_v + KV_WIDTH]], axis=1).T,
            lng=ln_v_g[l][None, :],
            lnb=ln_v_b[l][None, :],
            wsp=wsp,
            bsp=jnp.repeat(b_spatial[l].T, HEAD_DIM, axis=1),
            qgt=jnp.broadcast_to(q_norm_g[l][:, None], (HEAD_DIM, TM_IN)),
            kg=jnp.tile(k_norm_g[l], reps)[None, :],
            sink=jnp.repeat(sink[l] * LOG2E, BLOCK).reshape(N_KV, 1, GROUP * BLOCK),
            bg=b_gate[l][None, :],
            g2=norm2_g[l][None, :],
            cw=conv_w[l],
            cb=conv_b[l][None, :],
        ))
    stacked = dict(wg=w_gate.astype(bf16), wbr=w_branch.astype(bf16), wo=w_out.astype(bf16),
                   wup=w_up.astype(bf16), wdn=w_down.astype(bf16))
    gf = final_g[None, :]
    y_prompt = _trunk(x_prompt, layers, stacked, bias, seg, gf)
    y_sample = _trunk(x_sample, layers, stacked, bias, seg, gf)
    return (y_prompt, y_sample)
```

```python
import functools
import math

import jax
import jax.numpy as jnp
from jax import lax
from jax.experimental import pallas as pl
from jax.experimental.pallas import tpu as pltpu

D_MODEL = 1024
DEPTH = 2
HEAD_DIM = 64
BLOCK = 128
A_GROUPS = 8
A_WIDTH = A_GROUPS * HEAD_DIM
N_HEADS = 8
N_KV = 2
GROUP = N_HEADS // N_KV
WINDOW = 128
ROPE_THETA = 10000.0
GRID_W = 64
N_BUCKETS = 32
MAX_DIST = 128
D_FF = 2816
EPS = 1e-6
N_BRANCH = 3
BR_WIDTH = 512
Q_WIDTH = N_HEADS * HEAD_DIM
KV_WIDTH = N_KV * HEAD_DIM
QKV_WIDTH = Q_WIDTH + 2 * KV_WIDTH
A_IN = 2 * A_WIDTH

LANES = 128
SUBLANES = 8
VMEM_LIMIT = 56 * 1024 * 1024

TM_IN = 512
TM_MERGE = 512
TM_FFN = 512
FF_CHUNK = 256
FFN_LOOKAHEAD = 2
BAND_R = 4
BAND_LOOKAHEAD = 2
PV_ROWS = 128
TQ_FLASH = 128
TK_FLASH = 256
FLASH_LOOKAHEAD = 2
LOG2E = math.log2(math.e)
MAX_LOGIT_BOUND = 30.0

f32 = jnp.float32
bf16 = jnp.bfloat16


def _resident(shape, layer=None):
    if layer is None:
        return pl.BlockSpec(shape, lambda *_: (0,) * len(shape), pipeline_mode=pl.Buffered(1))
    return pl.BlockSpec((None,) + tuple(shape), lambda *_: (layer,) + (0,) * len(shape),
                        pipeline_mode=pl.Buffered(1))


def _zero_like(x):
    bits = pltpu.bitcast(x, jnp.uint32)
    return pltpu.bitcast((bits >> 16) >> 16, f32)


def _rms_scale(x):
    return lax.rsqrt(jnp.mean(x * x, axis=-1, keepdims=True) + EPS)


def _gelu_tanh(x):
    c = math.sqrt(2.0 / math.pi)
    return x * (0.5 * (1.0 + jnp.tanh(c * (x + 0.044715 * (x * x * x)))))


def _sigmoid(x):
    return 1.0 / (1.0 + jnp.exp(-x))


def _dot(a, b):
    return jnp.dot(a, b, preferred_element_type=f32)


def _dot_tn(a, b):
    return lax.dot_general(a, b, (((0,), (0,)), ((), ())), preferred_element_type=f32)


def _dot_nt(a, b):
    return lax.dot_general(a, b, (((1,), (1,)), ((), ())), preferred_element_type=f32)


def _head_mean_sq(x, seg_ref):
    sq = x * x
    hi = sq.astype(bf16)
    lo = (sq - hi.astype(f32)).astype(bf16)
    seg = seg_ref[...]
    return _dot(hi, seg) + _dot(lo, seg)


def _swap16(x):
    lane = lax.broadcasted_iota(jnp.int32, x.shape, 1)
    first_half = (lane % 32) < 16
    return jnp.where(first_half, pltpu.roll(x, LANES - 16, 1), pltpu.roll(x, 16, 1))


def _swap16_rows(x):
    h = HEAD_DIM // 4
    return jnp.concatenate([x[h:2 * h], x[0:h], x[3 * h:4 * h], x[2 * h:3 * h]], axis=0)


def _in_proj_kernel(x_ref, g1_ref, w_ref, wqt_ref, wvt_ref, lng_ref, lnb_ref, wsp_ref, bsp_ref,
                    qgt_ref, kg_ref, cos_ref, sin_ref, cost_ref, sint_ref, seg_ref,
                    oa_ref, kb_ref, kc_ref, qbt_ref, qct_ref, vbt_ref, vct_ref):
    tm = x_ref.shape[0]
    x = x_ref[...]
    xn = (x * _rms_scale(x) * g1_ref[...]).astype(bf16)

    zv = _dot(xn, w_ref[:, A_WIDTH:A_IN])
    zu = _dot(xn, w_ref[:, 0:A_WIDTH])
    zk = _dot(xn, w_ref[:, A_IN:A_IN + 2 * KV_WIDTH])
    qbt = _dot_nt(wqt_ref[0:Q_WIDTH, :], xn)
    qt = _dot_nt(wqt_ref[Q_WIDTH:2 * Q_WIDTH, :], xn)
    vt = _dot_nt(wvt_ref[...], xn).astype(bf16)

    v = _gelu_tanh(zv)
    u = _gelu_tanh(zu)
    mu = jnp.mean(v, axis=-1, keepdims=True)
    vc = v - mu
    vn = vc * lax.rsqrt(jnp.mean(vc * vc, axis=-1, keepdims=True) + EPS)
    vn = vn * lng_ref[...] + lnb_ref[...]
    lane = lax.broadcasted_iota(jnp.int32, (BLOCK, LANES), 1)
    low_half = lane < HEAD_DIM
    for c in range(tm // BLOCK):
        rows = slice(c * BLOCK, (c + 1) * BLOCK)
        for j in range(A_WIDTH // LANES):
            cols = slice(j * LANES, (j + 1) * LANES)
            vp = vn[rows, cols]
            stacked = jnp.concatenate(
                [jnp.where(low_half, vp, 0.0), jnp.where(low_half, 0.0, vp)], axis=0)
            sv = _dot(wsp_ref[j], stacked.astype(bf16)) + bsp_ref[:, cols]
            oa_ref[rows, cols] = (u[rows, cols] * sv).astype(bf16)

    kb_ref[...] = zk[:, 0:KV_WIDTH].astype(bf16)

    t = zk[:, KV_WIDTH:2 * KV_WIDTH]
    tn = t * lax.rsqrt(_head_mean_sq(t, seg_ref) + EPS) * kg_ref[...]
    kc_ref[:, 0:KV_WIDTH] = (tn * cos_ref[...] + _swap16(tn) * sin_ref[...]).astype(bf16)
    lane = lax.broadcasted_iota(jnp.int32, (tm, KV_WIDTH), 1)
    kc_ref[:, KV_WIDTH:2 * KV_WIDTH] = jnp.where(lane == 0, 1.0, 0.0).astype(bf16)

    scale = HEAD_DIM ** -0.5 * LOG2E
    qbt_ref[...] = (qbt * scale).astype(bf16)
    cost = cost_ref[...]
    sint = sint_ref[...]
    for h in range(N_HEADS):
        rows = slice(h * HEAD_DIM, (h + 1) * HEAD_DIM)
        th = qt[rows]
        r = lax.rsqrt(jnp.mean(th * th, axis=0, keepdims=True) + EPS)
        tn = th * r * qgt_ref[...]
        qct_ref[rows, :] = ((tn * cost + _swap16_rows(tn) * sint) * scale).astype(bf16)
    for c in range(tm // BLOCK):
        vbt_ref[c] = vt[0:KV_WIDTH, c * BLOCK:(c + 1) * BLOCK]
    for c in range(tm // TK_FLASH):
        vct_ref[c] = vt[KV_WIDTH:2 * KV_WIDTH, c * TK_FLASH:(c + 1) * TK_FLASH]


def _in_proj(x2d, seq, g1, w_in, wqt, wvt, lng, lnb, wsp, bsp, qgt, kg, cos_t, sin_t,
             cos_tt, sin_tt, seg):
    t = x2d.shape[0]
    tm = TM_IN
    tiles_per_seq = seq // tm
    const = lambda i: (0, 0)
    return pl.pallas_call(
        _in_proj_kernel,
        grid=(t // tm,),
        in_specs=[
            pl.BlockSpec((tm, D_MODEL), lambda i: (i, 0)),
            pl.BlockSpec((1, D_MODEL), const),
            _resident((D_MODEL, w_in.shape[1])),
            _resident((2 * Q_WIDTH, D_MODEL)),
            _resident((2 * KV_WIDTH, D_MODEL)),
            pl.BlockSpec((1, A_WIDTH), const),
            pl.BlockSpec((1, A_WIDTH), const),
            pl.BlockSpec((A_WIDTH // LANES, BLOCK, 2 * BLOCK), lambda i: (0, 0, 0)),
            pl.BlockSpec((BLOCK, A_WIDTH), const),
            pl.BlockSpec((HEAD_DIM, tm), const),
            pl.BlockSpec((1, LANES), const),
            pl.BlockSpec((tm, LANES), lambda i: (i % tiles_per_seq, 0)),
            pl.BlockSpec((tm, LANES), lambda i: (i % tiles_per_seq, 0)),
            pl.BlockSpec((HEAD_DIM, tm), lambda i: (0, i % tiles_per_seq)),
            pl.BlockSpec((HEAD_DIM, tm), lambda i: (0, i % tiles_per_seq)),
            pl.BlockSpec((LANES, LANES), const),
        ],
        out_specs=[
            pl.BlockSpec((tm, A_WIDTH), lambda i: (i, 0)),
            pl.BlockSpec((tm, KV_WIDTH), lambda i: (i, 0)),
            pl.BlockSpec((tm, 2 * KV_WIDTH), lambda i: (i, 0)),
            pl.BlockSpec((Q_WIDTH, tm), lambda i: (0, i)),
            pl.BlockSpec((Q_WIDTH, tm), lambda i: (0, i)),
            pl.BlockSpec((tm // BLOCK, KV_WIDTH, BLOCK), lambda i: (i, 0, 0)),
            pl.BlockSpec((tm // TK_FLASH, KV_WIDTH, TK_FLASH), lambda i: (i, 0, 0)),
        ],
        out_shape=[
            jax.ShapeDtypeStruct((t, A_WIDTH), bf16),
            jax.ShapeDtypeStruct((t, KV_WIDTH), bf16),
            jax.ShapeDtypeStruct((t, 2 * KV_WIDTH), bf16),
            jax.ShapeDtypeStruct((Q_WIDTH, t), bf16),
            jax.ShapeDtypeStruct((Q_WIDTH, t), bf16),
            jax.ShapeDtypeStruct((t // BLOCK, KV_WIDTH, BLOCK), bf16),
            jax.ShapeDtypeStruct((t // TK_FLASH, KV_WIDTH, TK_FLASH), bf16),
        ],
        compiler_params=pltpu.CompilerParams(
            dimension_semantics=("parallel",), vmem_limit_bytes=VMEM_LIMIT),
        name="in_proj",
    )(x2d, g1, w_in, wqt, wvt, lng, lnb, wsp, bsp, qgt, kg, cos_t, sin_t, cos_tt, sin_tt, seg)


def _band_bias_kernel(rel_bias_ref, bucket_ref, bias_ref):
    bucket = bucket_ref[...]
    for h in range(N_HEADS):
        val = jnp.full(bucket.shape, -jnp.inf, f32)
        for b in range(N_BUCKETS):
            val = jnp.where(bucket == b, rel_bias_ref[b, h] * LOG2E, val)
        gi = h % GROUP
        bias_ref[h // GROUP, :, gi * BLOCK:(gi + 1) * BLOCK] = val


def _band_bias(rel_bias, bucket_t):
    return pl.pallas_call(
        _band_bias_kernel,
        in_specs=[pl.BlockSpec(memory_space=pltpu.SMEM),
                  pl.BlockSpec((3 * BLOCK, BLOCK), lambda: (0, 0))],
        out_specs=pl.BlockSpec((N_KV, 3 * BLOCK, GROUP * BLOCK), lambda: (0, 0, 0)),
        out_shape=jax.ShapeDtypeStruct((N_KV, 3 * BLOCK, GROUP * BLOCK), f32),
        name="band_bias",
    )(rel_bias, bucket_t)


def _band_attn_kernel(nb, qt_ref, kp_ref, km_ref, kn_ref, vp_ref, vm_ref, vn_ref,
                      bias_ref, sink_ref, ot_ref):
    first = (pl.program_id(0) * BAND_R) % nb == 0
    last = ((pl.program_id(0) + 1) * BAND_R) % nb == 0
    k_win = jnp.concatenate([kp_ref[...], km_ref[...], kn_ref[...]], axis=0)
    v_blocks = [vp_ref[0]] + [vm_ref[r] for r in range(BAND_R)] + [vn_ref[0]]
    ones = jnp.ones((2 * SUBLANES, 3 * BLOCK), bf16)
    zeros = jnp.zeros((HEAD_DIM, GROUP * BLOCK), bf16)
    units = [(r, kv) for r in range(BAND_R) for kv in range(N_KV)]

    def logits(r, kv):
        qg = jnp.concatenate(
            [qt_ref[h * HEAD_DIM:(h + 1) * HEAD_DIM, r * BLOCK:(r + 1) * BLOCK]
             for h in range(kv * GROUP, (kv + 1) * GROUP)], axis=1)
        q_ext = jnp.concatenate([qg, zeros] if kv == 0 else [zeros, qg], axis=0)
        return _dot(k_win[r * BLOCK:(r + 3) * BLOCK], q_ext)

    s_queue = [logits(*u) for u in units[:BAND_LOOKAHEAD]]
    for i, (r, kv) in enumerate(units):
        if i + BAND_LOOKAHEAD < len(units):
            s_queue.append(logits(*units[i + BAND_LOOKAHEAD]))
        s = s_queue[i] + bias_ref[kv]
        s_queue[i] = None
        if r == 0:
            s = jnp.concatenate([jnp.where(first, -jnp.inf, s[0:BLOCK]), s[BLOCK:]], axis=0)
        if r == BAND_R - 1:
            s = jnp.concatenate([s[:2 * BLOCK], jnp.where(last, -jnp.inf, s[2 * BLOCK:])], axis=0)
        sink = sink_ref[kv]
        if i + 1 < len(units):
            sink = sink + _zero_like(s_queue[i + 1][0:1, :])
        m = jnp.maximum(jnp.max(s, axis=0, keepdims=True), sink)
        p = jnp.exp2(s - m).astype(bf16)
        v_band = jnp.concatenate(
            [blk[kv * HEAD_DIM:(kv + 1) * HEAD_DIM] for blk in v_blocks[r:r + 3]], axis=1)
        pv = _dot(jnp.concatenate([v_band, ones], axis=0), p)
        out = pv[0:HEAD_DIM] / (pv[HEAD_DIM:HEAD_DIM + 1] + jnp.exp2(sink - m))
        for gi in range(GROUP):
            h = kv * GROUP + gi
            ot_ref[h * HEAD_DIM:(h + 1) * HEAD_DIM, r * BLOCK:(r + 1) * BLOCK] = (
                out[:, gi * BLOCK:(gi + 1) * BLOCK].astype(bf16))


def _band_attn(qbt, kb, vbt, seq, sink_rows, bias_t):
    t = kb.shape[0]
    nb = seq // BLOCK
    nblocks = t // BLOCK
    r = BAND_R
    prev_blk = lambda i: jnp.maximum(i * r - 1, 0)
    next_blk = lambda i: jnp.minimum((i + 1) * r, nblocks - 1)
    return pl.pallas_call(
        functools.partial(_band_attn_kernel, nb),
        grid=(nblocks // r,),
        in_specs=[
            pl.BlockSpec((Q_WIDTH, r * BLOCK), lambda i: (0, i)),
            pl.BlockSpec((BLOCK, KV_WIDTH), lambda i: (prev_blk(i), 0)),
            pl.BlockSpec((r * BLOCK, KV_WIDTH), lambda i: (i, 0)),
            pl.BlockSpec((BLOCK, KV_WIDTH), lambda i: (next_blk(i), 0)),
            pl.BlockSpec((1, KV_WIDTH, BLOCK), lambda i: (prev_blk(i), 0, 0)),
            pl.BlockSpec((r, KV_WIDTH, BLOCK), lambda i: (i, 0, 0)),
            pl.BlockSpec((1, KV_WIDTH, BLOCK), lambda i: (next_blk(i), 0, 0)),
            pl.BlockSpec((N_KV, 3 * BLOCK, GROUP * BLOCK), lambda i: (0, 0, 0)),
            pl.BlockSpec((N_KV, 1, GROUP * BLOCK), lambda i: (0, 0, 0)),
        ],
        out_specs=pl.BlockSpec((Q_WIDTH, r * BLOCK), lambda i: (0, i)),
        out_shape=jax.ShapeDtypeStruct((Q_WIDTH, t), bf16),
        compiler_params=pltpu.CompilerParams(
            dimension_semantics=("parallel",), vmem_limit_bytes=VMEM_LIMIT),
        name="band_attn",
    )(qbt, kb, kb, kb, vbt, vbt, vbt, bias_t, sink_rows)


def _flash_queries(qt_ref, pad):
    tq = qt_ref.shape[1]
    zeros = jnp.zeros((HEAD_DIM, GROUP * tq), bf16)
    q_ext = []
    for kv in range(N_KV):
        qg = jnp.concatenate(
            [qt_ref[h * HEAD_DIM:(h + 1) * HEAD_DIM, :]
             for h in range(kv * GROUP, (kv + 1) * GROUP)], axis=1)
        q_ext.append(jnp.concatenate(([qg, zeros] if kv == 0 else [zeros, qg]) + [pad], axis=0))
    return q_ext


def _flash_store(ot_ref, kv, out):
    tq = ot_ref.shape[1]
    for gi in range(GROUP):
        h = kv * GROUP + gi
        ot_ref[h * HEAD_DIM:(h + 1) * HEAD_DIM, :] = out[:, gi * tq:(gi + 1) * tq].astype(bf16)


def _flash_online_kernel(qt_ref, k_ref, vt_ref, ot_ref):
    nq = GROUP * qt_ref.shape[1]
    n_chunks = vt_ref.shape[0]
    tk = vt_ref.shape[2]
    q_ext = _flash_queries(qt_ref, jnp.zeros((KV_WIDTH, nq), bf16))

    def logits(c):
        start = pl.multiple_of(c * tk, tk)
        kc = k_ref[pl.ds(start, tk), :]
        return tuple(_dot(kc, q_ext[kv]) for kv in range(N_KV))

    def body(c, carry):
        s_all, stats = carry
        s_next = logits(jnp.minimum(c + 1, n_chunks - 1))
        vt = vt_ref[c]
        out = []
        for kv in range(N_KV):
            m, l, acc = stats[kv]
            s = s_all[kv]
            m_new = jnp.maximum(m, jnp.max(s, axis=0, keepdims=True))
            alpha = jnp.exp2(m - m_new)
            p = jnp.exp2(s - m_new)
            l_new = alpha * l + jnp.sum(p, axis=0, keepdims=True)
            pv = _dot(vt[kv * HEAD_DIM:(kv + 1) * HEAD_DIM], p.astype(bf16))
            out.append((m_new, l_new, alpha * acc + pv))
        return s_next, tuple(out)

    init = tuple((jnp.full((1, nq), -jnp.inf, f32), jnp.zeros((1, nq), f32),
                  jnp.zeros((HEAD_DIM, nq), f32)) for _ in range(N_KV))
    _, final = lax.fori_loop(0, n_chunks, body, (logits(0), init))
    for kv in range(N_KV):
        _, l, acc = final[kv]
        _flash_store(ot_ref, kv, acc / l)


def _flash_bounded_kernel(qt_ref, pad_ref, k_ref, vt_ref, ot_ref):
    nq = GROUP * qt_ref.shape[1]
    n_chunks = vt_ref.shape[0]
    tk = vt_ref.shape[2]
    q_ext = _flash_queries(qt_ref, pad_ref[...])
    ones = jnp.concatenate([jnp.ones((2 * SUBLANES, tk), bf16),
                            jnp.zeros((PV_ROWS - HEAD_DIM - 2 * SUBLANES, tk), bf16)], axis=0)

    units = [(c, kv) for c in range(n_chunks) for kv in range(N_KV)]

    def logits(c, kv):
        return _dot(k_ref[c * tk:(c + 1) * tk, :], q_ext[kv])

    s_queue = [logits(*u) for u in units[:FLASH_LOOKAHEAD]]
    acc = [None] * N_KV
    for i, (c, kv) in enumerate(units):
        if i + FLASH_LOOKAHEAD < len(units):
            s_queue.append(logits(*units[i + FLASH_LOOKAHEAD]))
        p = jnp.exp2(s_queue[i]).astype(bf16)
        s_queue[i] = None
        v_ext = jnp.concatenate([vt_ref[c, kv * HEAD_DIM:(kv + 1) * HEAD_DIM, :], ones], axis=0)
        pv = _dot(v_ext, p)
        acc[kv] = pv if acc[kv] is None else acc[kv] + pv
    for kv in range(N_KV):
        _flash_store(ot_ref, kv, acc[kv][0:HEAD_DIM] / acc[kv][HEAD_DIM:HEAD_DIM + 1])


def _flash_attn(qct, pad, kc, vct, seq, bounded):
    t = kc.shape[0]
    tq = TQ_FLASH
    q_tiles = seq // tq
    tk = vct.shape[2]
    q_spec = pl.BlockSpec((Q_WIDTH, tq), lambda b, i: (0, b * q_tiles + i))
    kv_specs = [
        pl.BlockSpec((seq, 2 * KV_WIDTH), lambda b, i: (b, 0)),
        pl.BlockSpec((seq // tk, KV_WIDTH, tk), lambda b, i: (b, 0, 0)),
    ]
    pad_spec = pl.BlockSpec((KV_WIDTH, GROUP * tq), lambda b, i: (0, 0))
    return pl.pallas_call(
        _flash_bounded_kernel if bounded else _flash_online_kernel,
        grid=(t // seq, q_tiles),
        in_specs=[q_spec] + ([pad_spec] if bounded else []) + kv_specs,
        out_specs=q_spec,
        out_shape=jax.ShapeDtypeStruct((Q_WIDTH, t), bf16),
        compiler_params=pltpu.CompilerParams(
            dimension_semantics=("parallel", "parallel"), vmem_limit_bytes=VMEM_LIMIT),
        name="flash_bounded" if bounded else "flash_online",
    )(*([qct] + ([pad] if bounded else []) + [kc, vct]))


def _merge_kernel(x_ref, g1_ref, oa_ref, obt_ref, oct_ref, wg_ref, bg_ref, wbr_ref, wo_ref,
                  y_ref):
    x = x_ref[...]
    xn = (x * _rms_scale(x) * g1_ref[...]).astype(bf16)
    projs = (_dot(oa_ref[...], wbr_ref[0]), _dot_tn(obt_ref[...], wbr_ref[1]),
             _dot_tn(oct_ref[...], wbr_ref[2]))
    logits = [_dot(xn, wg_ref[:, n * D_MODEL:(n + 1) * D_MODEL]) for n in range(N_BRANCH)]
    merged = None
    for n, proj in enumerate(projs):
        gate = _sigmoid(logits[n] + bg_ref[:, n * D_MODEL:(n + 1) * D_MODEL])
        term = gate * proj
        merged = term if merged is None else merged + term
    y_ref[...] = x + _dot(merged.astype(bf16), wo_ref[...])


def _merge(x2d, g1, oa, obt, oct, wg, bg, wbr, wo, layer):
    t = x2d.shape[0]
    tm = TM_MERGE
    const = lambda i: (0, 0)
    row = lambda width: pl.BlockSpec((tm, width), lambda i: (i, 0))
    return pl.pallas_call(
        _merge_kernel,
        grid=(t // tm,),
        in_specs=[
            row(D_MODEL),
            pl.BlockSpec((1, D_MODEL), const),
            row(BR_WIDTH),
            pl.BlockSpec((BR_WIDTH, tm), lambda i: (0, i)),
            pl.BlockSpec((BR_WIDTH, tm), lambda i: (0, i)),
            _resident((D_MODEL, N_BRANCH * D_MODEL), layer),
            pl.BlockSpec((1, N_BRANCH * D_MODEL), const),
            _resident((N_BRANCH, BR_WIDTH, D_MODEL), layer),
            _resident((D_MODEL, D_MODEL), layer),
        ],
        out_specs=row(D_MODEL),
        out_shape=jax.ShapeDtypeStruct((t, D_MODEL), f32),
        compiler_params=pltpu.CompilerParams(
            dimension_semantics=("parallel",), vmem_limit_bytes=VMEM_LIMIT),
        name="merge",
    )(x2d, g1, oa, obt, oct, wg, bg, wbr, wo)


def _conv_ffn_kernel(tiles_per_seq, apply_final, xp_ref, x_ref, xq_ref, g2_ref, wup_ref,
                     cw_ref, cb_ref, wdn_ref, gf_ref, y_ref, act_ref):
    tm = x_ref.shape[0]
    i = pl.program_id(0) % tiles_per_seq
    x = x_ref[...]
    halo_p = jnp.where(i > 0, xp_ref[...], 0.0)
    halo_n = jnp.where(i < tiles_per_seq - 1, xq_ref[...], 0.0)
    xe = jnp.concatenate([halo_p, x, halo_n], axis=0)
    xn = (xe * _rms_scale(xe) * g2_ref[...]).astype(bf16)
    rows = tm + 2 * SUBLANES
    n_chunks = D_FF // FF_CHUNK
    chunk_cols = lambda c: (slice(c * FF_CHUNK, (c + 1) * FF_CHUNK),
                            slice(D_FF + c * FF_CHUNK, D_FF + (c + 1) * FF_CHUNK))

    def up(c):
        return tuple(_dot(xn, wup_ref[:, cols]) for cols in chunk_cols(c))

    h_queue = [up(c) for c in range(FFN_LOOKAHEAD)]
    for c in range(n_chunks):
        if c + FFN_LOOKAHEAD < n_chunks:
            h_queue.append(up(c + FFN_LOOKAHEAD))
        parts = []
        for h, cols in zip(h_queue[c], chunk_cols(c)):
            hc = (pltpu.roll(h, 1, 0) * cw_ref[0:1, cols]
                  + h * cw_ref[1:2, cols]
                  + pltpu.roll(h, rows - 1, 0) * cw_ref[2:3, cols]
                  + cb_ref[:, cols])
            parts.append(hc[SUBLANES:SUBLANES + tm])
        h_queue[c] = None
        act_ref[:, chunk_cols(c)[0]] = (parts[0] * _sigmoid(parts[0]) * parts[1]).astype(bf16)
    y = x + _dot(act_ref[...], wdn_ref[...])
    if apply_final:
        y = y * _rms_scale(y) * gf_ref[...]
    y_ref[...] = y


def _conv_ffn(x2d, seq, g2, wup, cw, cb, wdn, gf, layer, apply_final):
    t = x2d.shape[0]
    tm = TM_FFN
    tiles_per_seq = seq // tm
    halo_per_tile = tm // SUBLANES
    n_halo = t // SUBLANES
    const = lambda i: (0, 0)
    return pl.pallas_call(
        functools.partial(_conv_ffn_kernel, tiles_per_seq, apply_final),
        grid=(t // tm,),
        in_specs=[
            pl.BlockSpec((SUBLANES, D_MODEL),
                         lambda i: (jnp.maximum(i * halo_per_tile - 1, 0), 0)),
            pl.BlockSpec((tm, D_MODEL), lambda i: (i, 0)),
            pl.BlockSpec((SUBLANES, D_MODEL),
                         lambda i: (jnp.minimum((i + 1) * halo_per_tile, n_halo - 1), 0)),
            pl.BlockSpec((1, D_MODEL), const),
            _resident((D_MODEL, 2 * D_FF), layer),
            pl.BlockSpec((3, 2 * D_FF), const),
            pl.BlockSpec((1, 2 * D_FF), const),
            _resident((D_FF, D_MODEL), layer),
            pl.BlockSpec((1, D_MODEL), const),
        ],
        out_specs=pl.BlockSpec((tm, D_MODEL), lambda i: (i, 0)),
        out_shape=jax.ShapeDtypeStruct((t, D_MODEL), f32),
        scratch_shapes=[pltpu.VMEM((tm, D_FF), bf16)],
        compiler_params=pltpu.CompilerParams(
            dimension_semantics=("parallel",), vmem_limit_bytes=VMEM_LIMIT),
        name="conv_ffn",
    )(x2d, x2d, x2d, g2, wup, cw, cb, wdn, gf)


def _t5_bucket(rel):
    half = N_BUCKETS // 2
    max_exact = half // 2
    ret = jnp.where(rel > 0, half, 0)
    n = jnp.abs(rel)
    nf = jnp.maximum(n, 1).astype(f32)
    large = max_exact + (jnp.log(nf / max_exact) / math.log(MAX_DIST / max_exact)
                         * (half - max_exact)).astype(jnp.int32)
    large = jnp.minimum(large, half - 1)
    return ret + jnp.where(n < max_exact, n, large)


def _band_buckets():
    jpos = jnp.arange(3 * BLOCK)[:, None]
    qpos = jnp.arange(BLOCK)[None, :]
    rel = jpos - BLOCK - qpos
    return jnp.where(jnp.abs(rel) <= WINDOW, _t5_bucket(rel), -1).astype(jnp.int32)


def _rope_tables(seq):
    m = HEAD_DIM // 4
    pos = jnp.arange(seq)
    row = (pos // GRID_W).astype(f32)
    col = (pos % GRID_W).astype(f32)
    inv = ROPE_THETA ** (-jnp.arange(m, dtype=f32) / m)
    ang_r = row[:, None] * inv[None, :]
    ang_c = col[:, None] * inv[None, :]
    cos = jnp.concatenate([jnp.cos(ang_r), jnp.cos(ang_r), jnp.cos(ang_c), jnp.cos(ang_c)], axis=-1)
    sin = jnp.concatenate([-jnp.sin(ang_r), jnp.sin(ang_r), -jnp.sin(ang_c), jnp.sin(ang_c)], axis=-1)
    reps = LANES // HEAD_DIM
    return jnp.tile(cos, (1, reps)), jnp.tile(sin, (1, reps)), cos.T, sin.T


def _trunk(x, layers, stacked, bias, seg, final_g):
    bsz, seq, d = x.shape
    x2d = x.reshape(bsz * seq, d)
    cos_t, sin_t, cos_tt, sin_tt = _rope_tables(seq)
    for l, p in enumerate(layers):
        oa, kb, kc, qbt, qct, vbt, vct = _in_proj(
            x2d, seq, p["g1"], p["w_in"], p["wqt"], p["wvt"], p["lng"], p["lnb"], p["wsp"],
            p["bsp"], p["qgt"], p["kg"], cos_t, sin_t, cos_tt, sin_tt, seg)
        obt = _band_attn(qbt, kb, vbt, seq, p["sink"], bias)
        oct = lax.cond(
            p["logit_bound"] <= MAX_LOGIT_BOUND,
            lambda qct, pad, kc, vct: _flash_attn(qct, pad, kc, vct, seq, bounded=True),
            lambda qct, pad, kc, vct: _flash_attn(qct, pad, kc, vct, seq, bounded=False),
            qct, p["pad"], kc, vct)
        x2d = _merge(x2d, p["g1"], oa, obt, oct, stacked["wg"], p["bg"], stacked["wbr"],
                     stacked["wo"], l)
        x2d = _conv_ffn(x2d, seq, p["g2"], stacked["wup"], p["cw"], p["cb"], stacked["wdn"],
                        final_g, l, apply_final=(l == len(layers) - 1))
    return x2d.reshape(bsz, seq, d)


def kernel(x_prompt, x_sample, rel_bias, norm1_g, w_in, ln_v_g, ln_v_b, w_spatial, b_spatial,
           sink, q_norm_g, k_norm_g, w_gate, b_gate, w_branch, w_out, norm2_g, w_up, conv_w,
           conv_b, w_down, final_g):
    bias = _band_bias(rel_bias, _band_buckets())
    head_of_lane = jnp.arange(LANES) // HEAD_DIM
    seg = ((head_of_lane[:, None] == head_of_lane[None, :]).astype(f32) / HEAD_DIM).astype(bf16)
    reps = LANES // HEAD_DIM
    layers = []
    for l in range(DEPTH):
        ws = w_spatial[l].astype(bf16)
        wsp = jnp.concatenate([ws[0::2], ws[1::2]], axis=-1)
        wl = w_in[l].astype(bf16)
        b_q, b_k, b_v = A_IN, A_IN + Q_WIDTH, A_IN + Q_WIDTH + KV_WIDTH
        c_q, c_k, c_v = (b + QKV_WIDTH for b in (b_q, b_k, b_v))
        logit_bound = (1.02 * HEAD_DIM ** 0.5) * jnp.max(jnp.abs(q_norm_g[l])) * jnp.max(
            jnp.abs(k_norm_g[l]))
        shift = -(logit_bound * (LOG2E * (1.0 + 2.0 ** -7))).astype(bf16)
        pad = jnp.zeros((KV_WIDTH, GROUP * TQ_FLASH), bf16).at[0, :].set(shift)
        layers.append(dict(
            logit_bound=logit_bound,
            pad=pad,
            g1=norm1_g[l][None, :],
            w_in=jnp.concatenate(
                [wl[:, :A_IN], wl[:, b_k:b_k + KV_WIDTH], wl[:, c_k:c_k + KV_WIDTH]], axis=1),
            wqt=jnp.concatenate([wl[:, b_q:b_q + Q_WIDTH], wl[:, c_q:c_q + Q_WIDTH]], axis=1).T,
            wvt=jnp.concatenate([wl[:, b_v:b_v + KV_WIDTH], wl[:, c_v:c_v + KV_WIDTH]], axis=1).T,
            lng=ln_v_g[l][None, :],
            lnb=ln_v_b[l][None, :],
            wsp=wsp,
            bsp=jnp.repeat(b_spatial[l].T, HEAD_DIM, axis=1),
            qgt=jnp.broadcast_to(q_norm_g[l][:, None], (HEAD_DIM, TM_IN)),
            kg=jnp.tile(k_norm_g[l], reps)[None, :],
            sink=jnp.repeat(sink[l] * LOG2E, BLOCK).reshape(N_KV, 1, GROUP * BLOCK),
            bg=b_gate[l][None, :],
            g2=norm2_g[l][None, :],
            cw=conv_w[l],
            cb=conv_b[l][None, :],
        ))
    stacked = dict(wg=w_gate.astype(bf16), wbr=w_branch.astype(bf16), wo=w_out.astype(bf16),
                   wup=w_up.astype(bf16), wdn=w_down.astype(bf16))
    gf = final_g[None, :]
    y_prompt = _trunk(x_prompt, layers, stacked, bias, seg, gf)
    y_sample = _trunk(x_sample, layers, stacked, bias, seg, gf)
    return (y_prompt, y_sample)
```

```python
import functools
import math

import jax
import jax.numpy as jnp
from jax import lax
from jax.experimental import pallas as pl
from jax.experimental.pallas import tpu as pltpu

D_MODEL = 1024
DEPTH = 2
HEAD_DIM = 64
BLOCK = 128
A_GROUPS = 8
A_WIDTH = A_GROUPS * HEAD_DIM
N_HEADS = 8
N_KV = 2
GROUP = N_HEADS // N_KV
WINDOW = 128
ROPE_THETA = 10000.0
GRID_W = 64
N_BUCKETS = 32
MAX_DIST = 128
D_FF = 2816
EPS = 1e-6
N_BRANCH = 3
BR_WIDTH = 512
Q_WIDTH = N_HEADS * HEAD_DIM
KV_WIDTH = N_KV * HEAD_DIM
QKV_WIDTH = Q_WIDTH + 2 * KV_WIDTH
A_IN = 2 * A_WIDTH

LANES = 128
SUBLANES = 8
VMEM_LIMIT = 56 * 1024 * 1024

TM_IN = 512
TM_MERGE = 512
TM_FFN = 512
FF_CHUNK = 256
FFN_LOOKAHEAD = 2
BAND_R = 16
BAND_LOOKAHEAD = 2
PV_ROWS = 128
TQ_FLASH = 128
FLASH_SUBTILES = 4
TK_FLASH = 256
FLASH_LOOKAHEAD = 2
LOG2E = math.log2(math.e)
MAX_LOGIT_BOUND = 30.0

f32 = jnp.float32
bf16 = jnp.bfloat16


def _resident(shape, layer=None):
    if layer is None:
        return pl.BlockSpec(shape, lambda *_: (0,) * len(shape), pipeline_mode=pl.Buffered(1))
    return pl.BlockSpec((None,) + tuple(shape), lambda *_: (layer,) + (0,) * len(shape),
                        pipeline_mode=pl.Buffered(1))


def _zero_like(x):
    bits = pltpu.bitcast(x, jnp.uint32)
    return pltpu.bitcast((bits >> 16) >> 16, f32)


def _rms_scale(x):
    return lax.rsqrt(jnp.mean(x * x, axis=-1, keepdims=True) + EPS)


def _gelu_tanh(x):
    c = math.sqrt(2.0 / math.pi)
    return x * (0.5 * (1.0 + jnp.tanh(c * (x + 0.044715 * (x * x * x)))))


def _sigmoid(x):
    return 1.0 / (1.0 + jnp.exp(-x))


def _dot(a, b):
    return jnp.dot(a, b, preferred_element_type=f32)


def _dot_tn(a, b):
    return lax.dot_general(a, b, (((0,), (0,)), ((), ())), preferred_element_type=f32)


def _dot_nt(a, b):
    return lax.dot_general(a, b, (((1,), (1,)), ((), ())), preferred_element_type=f32)


def _head_mean_sq(x, seg_ref):
    sq = x * x
    hi = sq.astype(bf16)
    lo = (sq - hi.astype(f32)).astype(bf16)
    seg = seg_ref[...]
    return _dot(hi, seg) + _dot(lo, seg)


def _swap16(x):
    lane = lax.broadcasted_iota(jnp.int32, x.shape, 1)
    first_half = (lane % 32) < 16
    return jnp.where(first_half, pltpu.roll(x, LANES - 16, 1), pltpu.roll(x, 16, 1))


def _swap16_rows(x):
    h = HEAD_DIM // 4
    return jnp.concatenate([x[h:2 * h], x[0:h], x[3 * h:4 * h], x[2 * h:3 * h]], axis=0)


def _in_proj_kernel(x_ref, g1_ref, w_ref, wqt_ref, wvt_ref, lng_ref, lnb_ref, wsp_ref, bsp_ref,
                    qgt_ref, kg_ref, cos_ref, sin_ref, cost_ref, sint_ref, seg_ref,
                    oa_ref, kb_ref, kc_ref, qbt_ref, qct_ref, vbt_ref, vct_ref):
    tm = x_ref.shape[0]
    x = x_ref[...]
    xn = (x * _rms_scale(x) * g1_ref[...]).astype(bf16)

    zv = _dot(xn, w_ref[:, A_WIDTH:A_IN])
    zu = _dot(xn, w_ref[:, 0:A_WIDTH])
    zk = _dot(xn, w_ref[:, A_IN:A_IN + 2 * KV_WIDTH])
    qbt = _dot_nt(wqt_ref[0:Q_WIDTH, :], xn)
    qt = _dot_nt(wqt_ref[Q_WIDTH:2 * Q_WIDTH, :], xn)
    vt = _dot_nt(wvt_ref[...], xn).astype(bf16)

    v = _gelu_tanh(zv)
    u = _gelu_tanh(zu)
    mu = jnp.mean(v, axis=-1, keepdims=True)
    vc = v - mu
    vn = vc * lax.rsqrt(jnp.mean(vc * vc, axis=-1, keepdims=True) + EPS)
    vn = vn * lng_ref[...] + lnb_ref[...]
    lane = lax.broadcasted_iota(jnp.int32, (BLOCK, LANES), 1)
    low_half = lane < HEAD_DIM
    for c in range(tm // BLOCK):
        rows = slice(c * BLOCK, (c + 1) * BLOCK)
        for j in range(A_WIDTH // LANES):
            cols = slice(j * LANES, (j + 1) * LANES)
            vp = vn[rows, cols]
            stacked = jnp.concatenate(
                [jnp.where(low_half, vp, 0.0), jnp.where(low_half, 0.0, vp)], axis=0)
            sv = _dot(wsp_ref[j], stacked.astype(bf16)) + bsp_ref[:, cols]
            oa_ref[rows, cols] = (u[rows, cols] * sv).astype(bf16)

    kb_ref[...] = zk[:, 0:KV_WIDTH].astype(bf16)

    t = zk[:, KV_WIDTH:2 * KV_WIDTH]
    tn = t * lax.rsqrt(_head_mean_sq(t, seg_ref) + EPS) * kg_ref[...]
    kc_ref[:, 0:KV_WIDTH] = (tn * cos_ref[...] + _swap16(tn) * sin_ref[...]).astype(bf16)
    lane = lax.broadcasted_iota(jnp.int32, (tm, KV_WIDTH), 1)
    kc_ref[:, KV_WIDTH:2 * KV_WIDTH] = jnp.where(lane == 0, 1.0, 0.0).astype(bf16)

    scale = HEAD_DIM ** -0.5 * LOG2E
    qbt_ref[...] = (qbt * scale).astype(bf16)
    cost = cost_ref[...]
    sint = sint_ref[...]
    for h in range(N_HEADS):
        rows = slice(h * HEAD_DIM, (h + 1) * HEAD_DIM)
        th = qt[rows]
        r = lax.rsqrt(jnp.mean(th * th, axis=0, keepdims=True) + EPS)
        tn = th * r * qgt_ref[...]
        qct_ref[rows, :] = ((tn * cost + _swap16_rows(tn) * sint) * scale).astype(bf16)
    for c in range(tm // BLOCK):
        vbt_ref[c] = vt[0:KV_WIDTH, c * BLOCK:(c + 1) * BLOCK]
    for c in range(tm // TK_FLASH):
        vct_ref[c] = vt[KV_WIDTH:2 * KV_WIDTH, c * TK_FLASH:(c + 1) * TK_FLASH]


def _in_proj(x2d, seq, g1, w_in, wqt, wvt, lng, lnb, wsp, bsp, qgt, kg, cos_t, sin_t,
             cos_tt, sin_tt, seg):
    t = x2d.shape[0]
    tm = TM_IN
    tiles_per_seq = seq // tm
    const = lambda i: (0, 0)
    return pl.pallas_call(
        _in_proj_kernel,
        grid=(t // tm,),
        in_specs=[
            pl.BlockSpec((tm, D_MODEL), lambda i: (i, 0)),
            pl.BlockSpec((1, D_MODEL), const),
            _resident((D_MODEL, w_in.shape[1])),
            _resident((2 * Q_WIDTH, D_MODEL)),
            _resident((2 * KV_WIDTH, D_MODEL)),
            pl.BlockSpec((1, A_WIDTH), const),
            pl.BlockSpec((1, A_WIDTH), const),
            pl.BlockSpec((A_WIDTH // LANES, BLOCK, 2 * BLOCK), lambda i: (0, 0, 0)),
            pl.BlockSpec((BLOCK, A_WIDTH), const),
            pl.BlockSpec((HEAD_DIM, tm), const),
            pl.BlockSpec((1, LANES), const),
            pl.BlockSpec((tm, LANES), lambda i: (i % tiles_per_seq, 0)),
            pl.BlockSpec((tm, LANES), lambda i: (i % tiles_per_seq, 0)),
            pl.BlockSpec((HEAD_DIM, tm), lambda i: (0, i % tiles_per_seq)),
            pl.BlockSpec((HEAD_DIM, tm), lambda i: (0, i % tiles_per_seq)),
            pl.BlockSpec((LANES, LANES), const),
        ],
        out_specs=[
            pl.BlockSpec((tm, A_WIDTH), lambda i: (i, 0)),
            pl.BlockSpec((tm, KV_WIDTH), lambda i: (i, 0)),
            pl.BlockSpec((tm, 2 * KV_WIDTH), lambda i: (i, 0)),
            pl.BlockSpec((Q_WIDTH, tm), lambda i: (0, i)),
            pl.BlockSpec((Q_WIDTH, tm), lambda i: (0, i)),
            pl.BlockSpec((tm // BLOCK, KV_WIDTH, BLOCK), lambda i: (i, 0, 0)),
            pl.BlockSpec((tm // TK_FLASH, KV_WIDTH, TK_FLASH), lambda i: (i, 0, 0)),
        ],
        out_shape=[
            jax.ShapeDtypeStruct((t, A_WIDTH), bf16),
            jax.ShapeDtypeStruct((t, KV_WIDTH), bf16),
            jax.ShapeDtypeStruct((t, 2 * KV_WIDTH), bf16),
            jax.ShapeDtypeStruct((Q_WIDTH, t), bf16),
            jax.ShapeDtypeStruct((Q_WIDTH, t), bf16),
            jax.ShapeDtypeStruct((t // BLOCK, KV_WIDTH, BLOCK), bf16),
            jax.ShapeDtypeStruct((t // TK_FLASH, KV_WIDTH, TK_FLASH), bf16),
        ],
        compiler_params=pltpu.CompilerParams(
            dimension_semantics=("parallel",), vmem_limit_bytes=VMEM_LIMIT),
        name="in_proj",
    )(x2d, g1, w_in, wqt, wvt, lng, lnb, wsp, bsp, qgt, kg, cos_t, sin_t, cos_tt, sin_tt, seg)


def _band_bias_kernel(rel_bias_ref, bucket_ref, bias_ref):
    bucket = bucket_ref[...]
    for h in range(N_HEADS):
        val = jnp.full(bucket.shape, -jnp.inf, f32)
        for b in range(N_BUCKETS):
            val = jnp.where(bucket == b, rel_bias_ref[b, h] * LOG2E, val)
        gi = h % GROUP
        bias_ref[h // GROUP, :, gi * BLOCK:(gi + 1) * BLOCK] = val


def _band_bias(rel_bias, bucket_t):
    return pl.pallas_call(
        _band_bias_kernel,
        in_specs=[pl.BlockSpec(memory_space=pltpu.SMEM),
                  pl.BlockSpec((3 * BLOCK, BLOCK), lambda: (0, 0))],
        out_specs=pl.BlockSpec((N_KV, 3 * BLOCK, GROUP * BLOCK), lambda: (0, 0, 0)),
        out_shape=jax.ShapeDtypeStruct((N_KV, 3 * BLOCK, GROUP * BLOCK), f32),
        name="band_bias",
    )(rel_bias, bucket_t)


def _band_attn_kernel(nb, qt_ref, kp_ref, km_ref, kn_ref, vp_ref, vm_ref, vn_ref,
                      bias_ref, sink_ref, ot_ref):
    first = (pl.program_id(0) * BAND_R) % nb == 0
    last = ((pl.program_id(0) + 1) * BAND_R) % nb == 0
    k_win = jnp.concatenate([kp_ref[...], km_ref[...], kn_ref[...]], axis=0)
    v_blocks = [vp_ref[0]] + [vm_ref[r] for r in range(BAND_R)] + [vn_ref[0]]
    ones = jnp.ones((2 * SUBLANES, 3 * BLOCK), bf16)
    zeros = jnp.zeros((HEAD_DIM, GROUP * BLOCK), bf16)
    units = [(r, kv) for r in range(BAND_R) for kv in range(N_KV)]

    def logits(r, kv):
        qg = jnp.concatenate(
            [qt_ref[h * HEAD_DIM:(h + 1) * HEAD_DIM, r * BLOCK:(r + 1) * BLOCK]
             for h in range(kv * GROUP, (kv + 1) * GROUP)], axis=1)
        q_ext = jnp.concatenate([qg, zeros] if kv == 0 else [zeros, qg], axis=0)
        return _dot(k_win[r * BLOCK:(r + 3) * BLOCK], q_ext)

    s_queue = [logits(*u) for u in units[:BAND_LOOKAHEAD]]
    for i, (r, kv) in enumerate(units):
        if i + BAND_LOOKAHEAD < len(units):
            s_queue.append(logits(*units[i + BAND_LOOKAHEAD]))
        s = s_queue[i] + bias_ref[kv]
        s_queue[i] = None
        if r == 0:
            s = jnp.concatenate([jnp.where(first, -jnp.inf, s[0:BLOCK]), s[BLOCK:]], axis=0)
        if r == BAND_R - 1:
            s = jnp.concatenate([s[:2 * BLOCK], jnp.where(last, -jnp.inf, s[2 * BLOCK:])], axis=0)
        sink = sink_ref[kv]
        if i + 1 < len(units):
            sink = sink + _zero_like(s_queue[i + 1][0:1, :])
        m = jnp.maximum(jnp.max(s, axis=0, keepdims=True), sink)
        p = jnp.exp2(s - m).astype(bf16)
        v_band = jnp.concatenate(
            [blk[kv * HEAD_DIM:(kv + 1) * HEAD_DIM] for blk in v_blocks[r:r + 3]], axis=1)
        pv = _dot(jnp.concatenate([v_band, ones], axis=0), p)
        out = pv[0:HEAD_DIM] / (pv[HEAD_DIM:HEAD_DIM + 1] + jnp.exp2(sink - m))
        for gi in range(GROUP):
            h = kv * GROUP + gi
            ot_ref[h * HEAD_DIM:(h + 1) * HEAD_DIM, r * BLOCK:(r + 1) * BLOCK] = (
                out[:, gi * BLOCK:(gi + 1) * BLOCK].astype(bf16))


def _band_attn(qbt, kb, vbt, seq, sink_rows, bias_t):
    t = kb.shape[0]
    nb = seq // BLOCK
    nblocks = t // BLOCK
    r = BAND_R
    prev_blk = lambda i: jnp.maximum(i * r - 1, 0)
    next_blk = lambda i: jnp.minimum((i + 1) * r, nblocks - 1)
    return pl.pallas_call(
        functools.partial(_band_attn_kernel, nb),
        grid=(nblocks // r,),
        in_specs=[
            pl.BlockSpec((Q_WIDTH, r * BLOCK), lambda i: (0, i)),
            pl.BlockSpec((BLOCK, KV_WIDTH), lambda i: (prev_blk(i), 0)),
            pl.BlockSpec((r * BLOCK, KV_WIDTH), lambda i: (i, 0)),
            pl.BlockSpec((BLOCK, KV_WIDTH), lambda i: (next_blk(i), 0)),
            pl.BlockSpec((1, KV_WIDTH, BLOCK), lambda i: (prev_blk(i), 0, 0)),
            pl.BlockSpec((r, KV_WIDTH, BLOCK), lambda i: (i, 0, 0)),
            pl.BlockSpec((1, KV_WIDTH, BLOCK), lambda i: (next_blk(i), 0, 0)),
            pl.BlockSpec((N_KV, 3 * BLOCK, GROUP * BLOCK), lambda i: (0, 0, 0)),
            pl.BlockSpec((N_KV, 1, GROUP * BLOCK), lambda i: (0, 0, 0)),
        ],
        out_specs=pl.BlockSpec((Q_WIDTH, r * BLOCK), lambda i: (0, i)),
        out_shape=jax.ShapeDtypeStruct((Q_WIDTH, t), bf16),
        compiler_params=pltpu.CompilerParams(
            dimension_semantics=("parallel",), vmem_limit_bytes=VMEM_LIMIT),
        name="band_attn",
    )(qbt, kb, kb, kb, vbt, vbt, vbt, bias_t, sink_rows)


def _flash_queries(qt_ref, pad, cols=slice(None)):
    zeros = jnp.zeros((HEAD_DIM, pad.shape[1]), bf16)
    q_ext = []
    for kv in range(N_KV):
        qg = jnp.concatenate(
            [qt_ref[h * HEAD_DIM:(h + 1) * HEAD_DIM, cols]
             for h in range(kv * GROUP, (kv + 1) * GROUP)], axis=1)
        q_ext.append(jnp.concatenate(([qg, zeros] if kv == 0 else [zeros, qg]) + [pad], axis=0))
    return q_ext


def _flash_store(ot_ref, kv, out, cols=slice(None)):
    tq = out.shape[1] // GROUP
    for gi in range(GROUP):
        h = kv * GROUP + gi
        ot_ref[h * HEAD_DIM:(h + 1) * HEAD_DIM, cols] = out[:, gi * tq:(gi + 1) * tq].astype(bf16)


def _flash_online_kernel(qt_ref, k_ref, vt_ref, ot_ref):
    nq = GROUP * qt_ref.shape[1]
    n_chunks = vt_ref.shape[0]
    tk = vt_ref.shape[2]
    q_ext = _flash_queries(qt_ref, jnp.zeros((KV_WIDTH, nq), bf16))

    def logits(c):
        start = pl.multiple_of(c * tk, tk)
        kc = k_ref[pl.ds(start, tk), :]
        return tuple(_dot(kc, q_ext[kv]) for kv in range(N_KV))

    def body(c, carry):
        s_all, stats = carry
        s_next = logits(jnp.minimum(c + 1, n_chunks - 1))
        vt = vt_ref[c]
        out = []
        for kv in range(N_KV):
            m, l, acc = stats[kv]
            s = s_all[kv]
            m_new = jnp.maximum(m, jnp.max(s, axis=0, keepdims=True))
            alpha = jnp.exp2(m - m_new)
            p = jnp.exp2(s - m_new)
            l_new = alpha * l + jnp.sum(p, axis=0, keepdims=True)
            pv = _dot(vt[kv * HEAD_DIM:(kv + 1) * HEAD_DIM], p.astype(bf16))
            out.append((m_new, l_new, alpha * acc + pv))
        return s_next, tuple(out)

    init = tuple((jnp.full((1, nq), -jnp.inf, f32), jnp.zeros((1, nq), f32),
                  jnp.zeros((HEAD_DIM, nq), f32)) for _ in range(N_KV))
    _, final = lax.fori_loop(0, n_chunks, body, (logits(0), init))
    for kv in range(N_KV):
        _, l, acc = final[kv]
        _flash_store(ot_ref, kv, acc / l)


def _flash_bounded_kernel(qt_ref, pad_ref, k_ref, vt_ref, ot_ref):
    n_chunks = vt_ref.shape[0]
    tk = vt_ref.shape[2]
    ones = jnp.concatenate([jnp.ones((2 * SUBLANES, tk), bf16),
                            jnp.zeros((PV_ROWS - HEAD_DIM - 2 * SUBLANES, tk), bf16)], axis=0)
    units = [(c, kv) for c in range(n_chunks) for kv in range(N_KV)]

    for t in range(qt_ref.shape[1] // TQ_FLASH):
        cols = slice(t * TQ_FLASH, (t + 1) * TQ_FLASH)
        q_ext = _flash_queries(qt_ref, pad_ref[...], cols)

        def logits(c, kv):
            return _dot(k_ref[c * tk:(c + 1) * tk, :], q_ext[kv])

        s_queue = [logits(*u) for u in units[:FLASH_LOOKAHEAD]]
        acc = [None] * N_KV
        for i, (c, kv) in enumerate(units):
            if i + FLASH_LOOKAHEAD < len(units):
                s_queue.append(logits(*units[i + FLASH_LOOKAHEAD]))
            p = jnp.exp2(s_queue[i]).astype(bf16)
            s_queue[i] = None
            v_ext = jnp.concatenate(
                [vt_ref[c, kv * HEAD_DIM:(kv + 1) * HEAD_DIM, :], ones], axis=0)
            pv = _dot(v_ext, p)
            acc[kv] = pv if acc[kv] is None else acc[kv] + pv
        for kv in range(N_KV):
            _flash_store(ot_ref, kv, acc[kv][0:HEAD_DIM] / acc[kv][HEAD_DIM:HEAD_DIM + 1], cols)


def _flash_attn(qct, pad, kc, vct, seq, bounded):
    t = kc.shape[0]
    tq = TQ_FLASH * (FLASH_SUBTILES if bounded else 1)
    q_tiles = seq // tq
    tk = vct.shape[2]
    q_spec = pl.BlockSpec((Q_WIDTH, tq), lambda b, i: (0, b * q_tiles + i))
    kv_specs = [
        pl.BlockSpec((seq, 2 * KV_WIDTH), lambda b, i: (b, 0)),
        pl.BlockSpec((seq // tk, KV_WIDTH, tk), lambda b, i: (b, 0, 0)),
    ]
    pad_spec = pl.BlockSpec((KV_WIDTH, GROUP * TQ_FLASH), lambda b, i: (0, 0))
    return pl.pallas_call(
        _flash_bounded_kernel if bounded else _flash_online_kernel,
        grid=(t // seq, q_tiles),
        in_specs=[q_spec] + ([pad_spec] if bounded else []) + kv_specs,
        out_specs=q_spec,
        out_shape=jax.ShapeDtypeStruct((Q_WIDTH, t), bf16),
        compiler_params=pltpu.CompilerParams(
            dimension_semantics=("parallel", "parallel"), vmem_limit_bytes=VMEM_LIMIT),
        name="flash_bounded" if bounded else "flash_online",
    )(*([qct] + ([pad] if bounded else []) + [kc, vct]))


def _merge_kernel(x_ref, g1_ref, oa_ref, obt_ref, oct_ref, wg_ref, bg_ref, wbr_ref, wo_ref,
                  y_ref):
    x = x_ref[...]
    xn = (x * _rms_scale(x) * g1_ref[...]).astype(bf16)
    projs = (_dot(oa_ref[...], wbr_ref[0]), _dot_tn(obt_ref[...], wbr_ref[1]),
             _dot_tn(oct_ref[...], wbr_ref[2]))
    logits = [_dot(xn, wg_ref[:, n * D_MODEL:(n + 1) * D_MODEL]) for n in range(N_BRANCH)]
    merged = None
    for n, proj in enumerate(projs):
        gate = _sigmoid(logits[n] + bg_ref[:, n * D_MODEL:(n + 1) * D_MODEL])
        term = gate * proj
        merged = term if merged is None else merged + term
    y_ref[...] = x + _dot(merged.astype(bf16), wo_ref[...])


def _merge(x2d, g1, oa, obt, oct, wg, bg, wbr, wo, layer):
    t = x2d.shape[0]
    tm = TM_MERGE
    const = lambda i: (0, 0)
    row = lambda width: pl.BlockSpec((tm, width), lambda i: (i, 0))
    return pl.pallas_call(
        _merge_kernel,
        grid=(t // tm,),
        in_specs=[
            row(D_MODEL),
            pl.BlockSpec((1, D_MODEL), const),
            row(BR_WIDTH),
            pl.BlockSpec((BR_WIDTH, tm), lambda i: (0, i)),
            pl.BlockSpec((BR_WIDTH, tm), lambda i: (0, i)),
            _resident((D_MODEL, N_BRANCH * D_MODEL), layer),
            pl.BlockSpec((1, N_BRANCH * D_MODEL), const),
            _resident((N_BRANCH, BR_WIDTH, D_MODEL), layer),
            _resident((D_MODEL, D_MODEL), layer),
        ],
        out_specs=row(D_MODEL),
        out_shape=jax.ShapeDtypeStruct((t, D_MODEL), f32),
        compiler_params=pltpu.CompilerParams(
            dimension_semantics=("parallel",), vmem_limit_bytes=VMEM_LIMIT),
        name="merge",
    )(x2d, g1, oa, obt, oct, wg, bg, wbr, wo)


def _conv_ffn_kernel(tiles_per_seq, apply_final, xp_ref, x_ref, xq_ref, g2_ref, wup_ref,
                     cw_ref, cb_ref, wdn_ref, gf_ref, y_ref, act_ref):
    tm = x_ref.shape[0]
    i = pl.program_id(0) % tiles_per_seq
    x = x_ref[...]
    halo_p = jnp.where(i > 0, xp_ref[...], 0.0)
    halo_n = jnp.where(i < tiles_per_seq - 1, xq_ref[...], 0.0)
    xe = jnp.concatenate([halo_p, x, halo_n], axis=0)
    xn = (xe * _rms_scale(xe) * g2_ref[...]).astype(bf16)
    rows = tm + 2 * SUBLANES
    n_chunks = D_FF // FF_CHUNK
    chunk_cols = lambda c: (slice(c * FF_CHUNK, (c + 1) * FF_CHUNK),
                            slice(D_FF + c * FF_CHUNK, D_FF + (c + 1) * FF_CHUNK))

    def up(c):
        return tuple(_dot(xn, wup_ref[:, cols]) for cols in chunk_cols(c))

    h_queue = [up(c) for c in range(FFN_LOOKAHEAD)]
    for c in range(n_chunks):
        if c + FFN_LOOKAHEAD < n_chunks:
            h_queue.append(up(c + FFN_LOOKAHEAD))
        parts = []
        for h, cols in zip(h_queue[c], chunk_cols(c)):
            hc = (pltpu.roll(h, 1, 0) * cw_ref[0:1, cols]
                  + h * cw_ref[1:2, cols]
                  + pltpu.roll(h, rows - 1, 0) * cw_ref[2:3, cols]
                  + cb_ref[:, cols])
            parts.append(hc[SUBLANES:SUBLANES + tm])
        h_queue[c] = None
        act_ref[:, chunk_cols(c)[0]] = (parts[0] * _sigmoid(parts[0]) * parts[1]).astype(bf16)
    y = x + _dot(act_ref[...], wdn_ref[...])
    if apply_final:
        y = y * _rms_scale(y) * gf_ref[...]
    y_ref[...] = y


def _conv_ffn(x2d, seq, g2, wup, cw, cb, wdn, gf, layer, apply_final):
    t = x2d.shape[0]
    tm = TM_FFN
    tiles_per_seq = seq // tm
    halo_per_tile = tm // SUBLANES
    n_halo = t // SUBLANES
    const = lambda i: (0, 0)
    return pl.pallas_call(
        functools.partial(_conv_ffn_kernel, tiles_per_seq, apply_final),
        grid=(t // tm,),
        in_specs=[
            pl.BlockSpec((SUBLANES, D_MODEL),
                         lambda i: (jnp.maximum(i * halo_per_tile - 1, 0), 0)),
            pl.BlockSpec((tm, D_MODEL), lambda i: (i, 0)),
            pl.BlockSpec((SUBLANES, D_MODEL),
                         lambda i: (jnp.minimum((i + 1) * halo_per_tile, n_halo - 1), 0)),
            pl.BlockSpec((1, D_MODEL), const),
            _resident((D_MODEL, 2 * D_FF), layer),
            pl.BlockSpec((3, 2 * D_FF), const),
            pl.BlockSpec((1, 2 * D_FF), const),
            _resident((D_FF, D_MODEL), layer),
            pl.BlockSpec((1, D_MODEL), const),
        ],
        out_specs=pl.BlockSpec((tm, D_MODEL), lambda i: (i, 0)),
        out_shape=jax.ShapeDtypeStruct((t, D_MODEL), f32),
        scratch_shapes=[pltpu.VMEM((tm, D_FF), bf16)],
        compiler_params=pltpu.CompilerParams(
            dimension_semantics=("parallel",), vmem_limit_bytes=VMEM_LIMIT),
        name="conv_ffn",
    )(x2d, x2d, x2d, g2, wup, cw, cb, wdn, gf)


def _t5_bucket(rel):
    half = N_BUCKETS // 2
    max_exact = half // 2
    ret = jnp.where(rel > 0, half, 0)
    n = jnp.abs(rel)
    nf = jnp.maximum(n, 1).astype(f32)
    large = max_exact + (jnp.log(nf / max_exact) / math.log(MAX_DIST / max_exact)
                         * (half - max_exact)).astype(jnp.int32)
    large = jnp.minimum(large, half - 1)
    return ret + jnp.where(n < max_exact, n, large)


def _band_buckets():
    jpos = jnp.arange(3 * BLOCK)[:, None]
    qpos = jnp.arange(BLOCK)[None, :]
    rel = jpos - BLOCK - qpos
    return jnp.where(jnp.abs(rel) <= WINDOW, _t5_bucket(rel), -1).astype(jnp.int32)


def _rope_tables(seq):
    m = HEAD_DIM // 4
    pos = jnp.arange(seq)
    row = (pos // GRID_W).astype(f32)
    col = (pos % GRID_W).astype(f32)
    inv = ROPE_THETA ** (-jnp.arange(m, dtype=f32) / m)
    ang_r = row[:, None] * inv[None, :]
    ang_c = col[:, None] * inv[None, :]
    cos = jnp.concatenate([jnp.cos(ang_r), jnp.cos(ang_r), jnp.cos(ang_c), jnp.cos(ang_c)], axis=-1)
    sin = jnp.concatenate([-jnp.sin(ang_r), jnp.sin(ang_r), -jnp.sin(ang_c), jnp.sin(ang_c)], axis=-1)
    reps = LANES // HEAD_DIM
    return jnp.tile(cos, (1, reps)), jnp.tile(sin, (1, reps)), cos.T, sin.T


def _trunk(x, layers, stacked, bias, seg, final_g):
    bsz, seq, d = x.shape
    x2d = x.reshape(bsz * seq, d)
    cos_t, sin_t, cos_tt, sin_tt = _rope_tables(seq)
    for l, p in enumerate(layers):
        oa, kb, kc, qbt, qct, vbt, vct = _in_proj(
            x2d, seq, p["g1"], p["w_in"], p["wqt"], p["wvt"], p["lng"], p["lnb"], p["wsp"],
            p["bsp"], p["qgt"], p["kg"], cos_t, sin_t, cos_tt, sin_tt, seg)
        obt = _band_attn(qbt, kb, vbt, seq, p["sink"], bias)
        oct = lax.cond(
            p["logit_bound"] <= MAX_LOGIT_BOUND,
            lambda qct, pad, kc, vct: _flash_attn(qct, pad, kc, vct, seq, bounded=True),
            lambda qct, pad, kc, vct: _flash_attn(qct, pad, kc, vct, seq, bounded=False),
            qct, p["pad"], kc, vct)
        x2d = _merge(x2d, p["g1"], oa, obt, oct, stacked["wg"], p["bg"], stacked["wbr"],
                     stacked["wo"], l)
        x2d = _conv_ffn(x2d, seq, p["g2"], stacked["wup"], p["cw"], p["cb"], stacked["wdn"],
                        final_g, l, apply_final=(l == len(layers) - 1))
    return x2d.reshape(bsz, seq, d)


def kernel(x_prompt, x_sample, rel_bias, norm1_g, w_in, ln_v_g, ln_v_b, w_spatial, b_spatial,
           sink, q_norm_g, k_norm_g, w_gate, b_gate, w_branch, w_out, norm2_g, w_up, conv_w,
           conv_b, w_down, final_g):
    bias = _band_bias(rel_bias, _band_buckets())
    head_of_lane = jnp.arange(LANES) // HEAD_DIM
    seg = ((head_of_lane[:, None] == head_of_lane[None, :]).astype(f32) / HEAD_DIM).astype(bf16)
    reps = LANES // HEAD_DIM
    layers = []
    for l in range(DEPTH):
        ws = w_spatial[l].astype(bf16)
        wsp = jnp.concatenate([ws[0::2], ws[1::2]], axis=-1)
        wl = w_in[l].astype(bf16)
        b_q, b_k, b_v = A_IN, A_IN + Q_WIDTH, A_IN + Q_WIDTH + KV_WIDTH
        c_q, c_k, c_v = (b + QKV_WIDTH for b in (b_q, b_k, b_v))
        logit_bound = (1.02 * HEAD_DIM ** 0.5) * jnp.max(jnp.abs(q_norm_g[l])) * jnp.max(
            jnp.abs(k_norm_g[l]))
        shift = -(logit_bound * (LOG2E * (1.0 + 2.0 ** -7))).astype(bf16)
        pad = jnp.zeros((KV_WIDTH, GROUP * TQ_FLASH), bf16).at[0, :].set(shift)
        layers.append(dict(
            logit_bound=logit_bound,
            pad=pad,
            g1=norm1_g[l][None, :],
            w_in=jnp.concatenate(
                [wl[:, :A_IN], wl[:, b_k:b_k + KV_WIDTH], wl[:, c_k:c_k + KV_WIDTH]], axis=1),
            wqt=jnp.concatenate([wl[:, b_q:b_q + Q_WIDTH], wl[:, c_q:c_q + Q_WIDTH]], axis=1).T,
            wvt=jnp.concatenate([wl[:, b_v:b_v + KV_WIDTH], wl[:, c_v:c_v + KV_WIDTH]], axis=1).T,
            lng=ln_v_g[l][None, :],
            lnb=ln_v_b[l][None, :],
            wsp=wsp,
            bsp=jnp.repeat(b_spatial[l].T, HEAD_DIM, axis=1),
            qgt=jnp.broadcast_to(q_norm_g[l][:, None], (HEAD_DIM, TM_IN)),
            kg=jnp.tile(k_norm_g[l], reps)[None, :],
            sink=jnp.repeat(sink[l] * LOG2E, BLOCK).reshape(N_KV, 1, GROUP * BLOCK),
            bg=b_gate[l][None, :],
            g2=norm2_g[l][None, :],
            cw=conv_w[l],
            cb=conv_b[l][None, :],
        ))
    stacked = dict(wg=w_gate.astype(bf16), wbr=w_branch.astype(bf16), wo=w_out.astype(bf16),
                   wup=w_up.astype(bf16), wdn=w_down.astype(bf16))
    gf = final_g[None, :]
    y_prompt = _trunk(x_prompt, layers, stacked, bias, seg, gf)
    y_sample = _trunk(x_sample, layers, stacked, bias, seg, gf)
    return (y_prompt, y_sample)
```

```python
import functools
import math

import jax
import jax.numpy as jnp
from jax import lax
from jax.experimental import pallas as pl
from jax.experimental.pallas import tpu as pltpu

D_MODEL = 1024
DEPTH = 2
HEAD_DIM = 64
BLOCK = 128
A_GROUPS = 8
A_WIDTH = A_GROUPS * HEAD_DIM
N_HEADS = 8
N_KV = 2
GROUP = N_HEADS // N_KV
WINDOW = 128
ROPE_THETA = 10000.0
GRID_W = 64
N_BUCKETS = 32
MAX_DIST = 128
D_FF = 2816
EPS = 1e-6
N_BRANCH = 3
BR_WIDTH = 512
Q_WIDTH = N_HEADS * HEAD_DIM
KV_WIDTH = N_KV * HEAD_DIM
QKV_WIDTH = Q_WIDTH + 2 * KV_WIDTH
A_IN = 2 * A_WIDTH

LANES = 128
SUBLANES = 8
VMEM_LIMIT = 56 * 1024 * 1024

TM_IN = 512
TM_MERGE = 512
MERGE_ROWS = 256
TM_FFN = 512
FF_CHUNK = 256
FFN_LOOKAHEAD = 2
BAND_R = 16
BAND_LOOKAHEAD = 2
PV_ROWS = 128
TQ_FLASH = 128
FLASH_SUBTILES = 4
TK_FLASH = 256
FLASH_LOOKAHEAD = 2
LOG2E = math.log2(math.e)
MAX_LOGIT_BOUND = 30.0

f32 = jnp.float32
bf16 = jnp.bfloat16


def _resident(shape, layer=None):
    if layer is None:
        return pl.BlockSpec(shape, lambda *_: (0,) * len(shape), pipeline_mode=pl.Buffered(1))
    return pl.BlockSpec((None,) + tuple(shape), lambda *_: (layer,) + (0,) * len(shape),
                        pipeline_mode=pl.Buffered(1))


def _zero_like(x):
    bits = pltpu.bitcast(x, jnp.uint32)
    return pltpu.bitcast((bits >> 16) >> 16, f32)


def _rms_scale(x):
    return lax.rsqrt(jnp.mean(x * x, axis=-1, keepdims=True) + EPS)


def _gelu_tanh(x):
    c = math.sqrt(2.0 / math.pi)
    return x * (0.5 * (1.0 + jnp.tanh(c * (x + 0.044715 * (x * x * x)))))


def _sigmoid(x):
    return 1.0 / (1.0 + jnp.exp(-x))


def _dot(a, b):
    return jnp.dot(a, b, preferred_element_type=f32)


def _dot_tn(a, b):
    return lax.dot_general(a, b, (((0,), (0,)), ((), ())), preferred_element_type=f32)


def _dot_nt(a, b):
    return lax.dot_general(a, b, (((1,), (1,)), ((), ())), preferred_element_type=f32)


def _head_mean_sq(x, seg_ref):
    sq = x * x
    hi = sq.astype(bf16)
    lo = (sq - hi.astype(f32)).astype(bf16)
    seg = seg_ref[...]
    return _dot(hi, seg) + _dot(lo, seg)


def _swap16(x):
    lane = lax.broadcasted_iota(jnp.int32, x.shape, 1)
    first_half = (lane % 32) < 16
    return jnp.where(first_half, pltpu.roll(x, LANES - 16, 1), pltpu.roll(x, 16, 1))


def _swap16_rows(x):
    h = HEAD_DIM // 4
    return jnp.concatenate([x[h:2 * h], x[0:h], x[3 * h:4 * h], x[2 * h:3 * h]], axis=0)


def _in_proj_kernel(x_ref, g1_ref, w_ref, wqt_ref, wvt_ref, lng_ref, lnb_ref, wsp_ref, bsp_ref,
                    qgt_ref, kg_ref, cos_ref, sin_ref, cost_ref, sint_ref, seg_ref,
                    oa_ref, kb_ref, kc_ref, qbt_ref, qct_ref, vbt_ref, vct_ref):
    tm = x_ref.shape[0]
    x = x_ref[...]
    xn = (x * _rms_scale(x) * g1_ref[...]).astype(bf16)

    zv = _dot(xn, w_ref[:, A_WIDTH:A_IN])
    zu = _dot(xn, w_ref[:, 0:A_WIDTH])
    zk = _dot(xn, w_ref[:, A_IN:A_IN + 2 * KV_WIDTH])
    qbt = _dot_nt(wqt_ref[0:Q_WIDTH, :], xn)
    qt = _dot_nt(wqt_ref[Q_WIDTH:2 * Q_WIDTH, :], xn)
    vt = _dot_nt(wvt_ref[...], xn).astype(bf16)

    v = _gelu_tanh(zv)
    u = _gelu_tanh(zu)
    mu = jnp.mean(v, axis=-1, keepdims=True)
    vc = v - mu
    vn = vc * lax.rsqrt(jnp.mean(vc * vc, axis=-1, keepdims=True) + EPS)
    vn = vn * lng_ref[...] + lnb_ref[...]
    lane = lax.broadcasted_iota(jnp.int32, (BLOCK, LANES), 1)
    low_half = lane < HEAD_DIM
    for c in range(tm // BLOCK):
        rows = slice(c * BLOCK, (c + 1) * BLOCK)
        for j in range(A_WIDTH // LANES):
            cols = slice(j * LANES, (j + 1) * LANES)
            vp = vn[rows, cols]
            stacked = jnp.concatenate(
                [jnp.where(low_half, vp, 0.0), jnp.where(low_half, 0.0, vp)], axis=0)
            sv = _dot(wsp_ref[j], stacked.astype(bf16)) + bsp_ref[:, cols]
            oa_ref[rows, cols] = (u[rows, cols] * sv).astype(bf16)

    kb_ref[...] = zk[:, 0:KV_WIDTH].astype(bf16)

    t = zk[:, KV_WIDTH:2 * KV_WIDTH]
    tn = t * lax.rsqrt(_head_mean_sq(t, seg_ref) + EPS) * kg_ref[...]
    kc_ref[:, 0:KV_WIDTH] = (tn * cos_ref[...] + _swap16(tn) * sin_ref[...]).astype(bf16)
    lane = lax.broadcasted_iota(jnp.int32, (tm, KV_WIDTH), 1)
    kc_ref[:, KV_WIDTH:2 * KV_WIDTH] = jnp.where(lane == 0, 1.0, 0.0).astype(bf16)

    scale = HEAD_DIM ** -0.5 * LOG2E
    qbt_ref[...] = (qbt * scale).astype(bf16)
    cost = cost_ref[...]
    sint = sint_ref[...]
    for h in range(N_HEADS):
        rows = slice(h * HEAD_DIM, (h + 1) * HEAD_DIM)
        th = qt[rows]
        r = lax.rsqrt(jnp.mean(th * th, axis=0, keepdims=True) + EPS)
        tn = th * r * qgt_ref[...]
        qct_ref[rows, :] = ((tn * cost + _swap16_rows(tn) * sint) * scale).astype(bf16)
    for c in range(tm // BLOCK):
        vbt_ref[c] = vt[0:KV_WIDTH, c * BLOCK:(c + 1) * BLOCK]
    for c in range(tm // TK_FLASH):
        vct_ref[c] = vt[KV_WIDTH:2 * KV_WIDTH, c * TK_FLASH:(c + 1) * TK_FLASH]


def _in_proj(x2d, seq, g1, w_in, wqt, wvt, lng, lnb, wsp, bsp, qgt, kg, cos_t, sin_t,
             cos_tt, sin_tt, seg):
    t = x2d.shape[0]
    tm = TM_IN
    tiles_per_seq = seq // tm
    const = lambda i: (0, 0)
    return pl.pallas_call(
        _in_proj_kernel,
        grid=(t // tm,),
        in_specs=[
            pl.BlockSpec((tm, D_MODEL), lambda i: (i, 0)),
            pl.BlockSpec((1, D_MODEL), const),
            _resident((D_MODEL, w_in.shape[1])),
            _resident((2 * Q_WIDTH, D_MODEL)),
            _resident((2 * KV_WIDTH, D_MODEL)),
            pl.BlockSpec((1, A_WIDTH), const),
            pl.BlockSpec((1, A_WIDTH), const),
            pl.BlockSpec((A_WIDTH // LANES, BLOCK, 2 * BLOCK), lambda i: (0, 0, 0)),
            pl.BlockSpec((BLOCK, A_WIDTH), const),
            pl.BlockSpec((HEAD_DIM, tm), const),
            pl.BlockSpec((1, LANES), const),
            pl.BlockSpec((tm, LANES), lambda i: (i % tiles_per_seq, 0)),
            pl.BlockSpec((tm, LANES), lambda i: (i % tiles_per_seq, 0)),
            pl.BlockSpec((HEAD_DIM, tm), lambda i: (0, i % tiles_per_seq)),
            pl.BlockSpec((HEAD_DIM, tm), lambda i: (0, i % tiles_per_seq)),
            pl.BlockSpec((LANES, LANES), const),
        ],
        out_specs=[
            pl.BlockSpec((tm, A_WIDTH), lambda i: (i, 0)),
            pl.BlockSpec((tm, KV_WIDTH), lambda i: (i, 0)),
            pl.BlockSpec((tm, 2 * KV_WIDTH), lambda i: (i, 0)),
            pl.BlockSpec((Q_WIDTH, tm), lambda i: (0, i)),
            pl.BlockSpec((Q_WIDTH, tm), lambda i: (0, i)),
            pl.BlockSpec((tm // BLOCK, KV_WIDTH, BLOCK), lambda i: (i, 0, 0)),
            pl.BlockSpec((tm // TK_FLASH, KV_WIDTH, TK_FLASH), lambda i: (i, 0, 0)),
        ],
        out_shape=[
            jax.ShapeDtypeStruct((t, A_WIDTH), bf16),
            jax.ShapeDtypeStruct((t, KV_WIDTH), bf16),
            jax.ShapeDtypeStruct((t, 2 * KV_WIDTH), bf16),
            jax.ShapeDtypeStruct((Q_WIDTH, t), bf16),
            jax.ShapeDtypeStruct((Q_WIDTH, t), bf16),
            jax.ShapeDtypeStruct((t // BLOCK, KV_WIDTH, BLOCK), bf16),
            jax.ShapeDtypeStruct((t // TK_FLASH, KV_WIDTH, TK_FLASH), bf16),
        ],
        compiler_params=pltpu.CompilerParams(
            dimension_semantics=("parallel",), vmem_limit_bytes=VMEM_LIMIT),
        name="in_proj",
    )(x2d, g1, w_in, wqt, wvt, lng, lnb, wsp, bsp, qgt, kg, cos_t, sin_t, cos_tt, sin_tt, seg)


def _band_bias_kernel(rel_bias_ref, bucket_ref, bias_ref):
    bucket = bucket_ref[...]
    for h in range(N_HEADS):
        val = jnp.full(bucket.shape, -jnp.inf, f32)
        for b in range(N_BUCKETS):
            val = jnp.where(bucket == b, rel_bias_ref[b, h] * LOG2E, val)
        gi = h % GROUP
        bias_ref[h // GROUP, :, gi * BLOCK:(gi + 1) * BLOCK] = val


def _band_bias(rel_bias, bucket_t):
    return pl.pallas_call(
        _band_bias_kernel,
        in_specs=[pl.BlockSpec(memory_space=pltpu.SMEM),
                  pl.BlockSpec((3 * BLOCK, BLOCK), lambda: (0, 0))],
        out_specs=pl.BlockSpec((N_KV, 3 * BLOCK, GROUP * BLOCK), lambda: (0, 0, 0)),
        out_shape=jax.ShapeDtypeStruct((N_KV, 3 * BLOCK, GROUP * BLOCK), f32),
        name="band_bias",
    )(rel_bias, bucket_t)


def _band_attn_kernel(nb, qt_ref, kp_ref, km_ref, kn_ref, vp_ref, vm_ref, vn_ref,
                      bias_ref, sink_ref, ot_ref):
    first = (pl.program_id(0) * BAND_R) % nb == 0
    last = ((pl.program_id(0) + 1) * BAND_R) % nb == 0
    k_win = jnp.concatenate([kp_ref[...], km_ref[...], kn_ref[...]], axis=0)
    v_blocks = [vp_ref[0]] + [vm_ref[r] for r in range(BAND_R)] + [vn_ref[0]]
    ones = jnp.ones((2 * SUBLANES, 3 * BLOCK), bf16)
    zeros = jnp.zeros((HEAD_DIM, GROUP * BLOCK), bf16)
    units = [(r, kv) for r in range(BAND_R) for kv in range(N_KV)]

    def logits(r, kv):
        qg = jnp.concatenate(
            [qt_ref[h * HEAD_DIM:(h + 1) * HEAD_DIM, r * BLOCK:(r + 1) * BLOCK]
             for h in range(kv * GROUP, (kv + 1) * GROUP)], axis=1)
        q_ext = jnp.concatenate([qg, zeros] if kv == 0 else [zeros, qg], axis=0)
        return _dot(k_win[r * BLOCK:(r + 3) * BLOCK], q_ext)

    s_queue = [logits(*u) for u in units[:BAND_LOOKAHEAD]]
    for i, (r, kv) in enumerate(units):
        if i + BAND_LOOKAHEAD < len(units):
            s_queue.append(logits(*units[i + BAND_LOOKAHEAD]))
        s = s_queue[i] + bias_ref[kv]
        s_queue[i] = None
        if r == 0:
            s = jnp.concatenate([jnp.where(first, -jnp.inf, s[0:BLOCK]), s[BLOCK:]], axis=0)
        if r == BAND_R - 1:
            s = jnp.concatenate([s[:2 * BLOCK], jnp.where(last, -jnp.inf, s[2 * BLOCK:])], axis=0)
        sink = sink_ref[kv]
        if i + 1 < len(units):
            sink = sink + _zero_like(s_queue[i + 1][0:1, :])
        m = jnp.maximum(jnp.max(s, axis=0, keepdims=True), sink)
        p = jnp.exp2(s - m).astype(bf16)
        v_band = jnp.concatenate(
            [blk[kv * HEAD_DIM:(kv + 1) * HEAD_DIM] for blk in v_blocks[r:r + 3]], axis=1)
        pv = _dot(jnp.concatenate([v_band, ones], axis=0), p)
        out = pv[0:HEAD_DIM] / (pv[HEAD_DIM:HEAD_DIM + 1] + jnp.exp2(sink - m))
        for gi in range(GROUP):
            h = kv * GROUP + gi
            ot_ref[h * HEAD_DIM:(h + 1) * HEAD_DIM, r * BLOCK:(r + 1) * BLOCK] = (
                out[:, gi * BLOCK:(gi + 1) * BLOCK].astype(bf16))


def _band_attn(qbt, kb, vbt, seq, sink_rows, bias_t):
    t = kb.shape[0]
    nb = seq // BLOCK
    nblocks = t // BLOCK
    r = BAND_R
    prev_blk = lambda i: jnp.maximum(i * r - 1, 0)
    next_blk = lambda i: jnp.minimum((i + 1) * r, nblocks - 1)
    return pl.pallas_call(
        functools.partial(_band_attn_kernel, nb),
        grid=(nblocks // r,),
        in_specs=[
            pl.BlockSpec((Q_WIDTH, r * BLOCK), lambda i: (0, i)),
            pl.BlockSpec((BLOCK, KV_WIDTH), lambda i: (prev_blk(i), 0)),
            pl.BlockSpec((r * BLOCK, KV_WIDTH), lambda i: (i, 0)),
            pl.BlockSpec((BLOCK, KV_WIDTH), lambda i: (next_blk(i), 0)),
            pl.BlockSpec((1, KV_WIDTH, BLOCK), lambda i: (prev_blk(i), 0, 0)),
            pl.BlockSpec((r, KV_WIDTH, BLOCK), lambda i: (i, 0, 0)),
            pl.BlockSpec((1, KV_WIDTH, BLOCK), lambda i: (next_blk(i), 0, 0)),
            pl.BlockSpec((N_KV, 3 * BLOCK, GROUP * BLOCK), lambda i: (0, 0, 0)),
            pl.BlockSpec((N_KV, 1, GROUP * BLOCK), lambda i: (0, 0, 0)),
        ],
        out_specs=pl.BlockSpec((Q_WIDTH, r * BLOCK), lambda i: (0, i)),
        out_shape=jax.ShapeDtypeStruct((Q_WIDTH, t), bf16),
        compiler_params=pltpu.CompilerParams(
            dimension_semantics=("parallel",), vmem_limit_bytes=VMEM_LIMIT),
        name="band_attn",
    )(qbt, kb, kb, kb, vbt, vbt, vbt, bias_t, sink_rows)


def _flash_queries(qt_ref, pad, cols=slice(None)):
    zeros = jnp.zeros((HEAD_DIM, pad.shape[1]), bf16)
    q_ext = []
    for kv in range(N_KV):
        qg = jnp.concatenate(
            [qt_ref[h * HEAD_DIM:(h + 1) * HEAD_DIM, cols]
             for h in range(kv * GROUP, (kv + 1) * GROUP)], axis=1)
        q_ext.append(jnp.concatenate(([qg, zeros] if kv == 0 else [zeros, qg]) + [pad], axis=0))
    return q_ext


def _flash_store(ot_ref, kv, out, cols=slice(None)):
    tq = out.shape[1] // GROUP
    for gi in range(GROUP):
        h = kv * GROUP + gi
        ot_ref[h * HEAD_DIM:(h + 1) * HEAD_DIM, cols] = out[:, gi * tq:(gi + 1) * tq].astype(bf16)


def _flash_online_kernel(qt_ref, k_ref, vt_ref, ot_ref):
    nq = GROUP * qt_ref.shape[1]
    n_chunks = vt_ref.shape[0]
    tk = vt_ref.shape[2]
    q_ext = _flash_queries(qt_ref, jnp.zeros((KV_WIDTH, nq), bf16))

    def logits(c):
        start = pl.multiple_of(c * tk, tk)
        kc = k_ref[pl.ds(start, tk), :]
        return tuple(_dot(kc, q_ext[kv]) for kv in range(N_KV))

    def body(c, carry):
        s_all, stats = carry
        s_next = logits(jnp.minimum(c + 1, n_chunks - 1))
        vt = vt_ref[c]
        out = []
        for kv in range(N_KV):
            m, l, acc = stats[kv]
            s = s_all[kv]
            m_new = jnp.maximum(m, jnp.max(s, axis=0, keepdims=True))
            alpha = jnp.exp2(m - m_new)
            p = jnp.exp2(s - m_new)
            l_new = alpha * l + jnp.sum(p, axis=0, keepdims=True)
            pv = _dot(vt[kv * HEAD_DIM:(kv + 1) * HEAD_DIM], p.astype(bf16))
            out.append((m_new, l_new, alpha * acc + pv))
        return s_next, tuple(out)

    init = tuple((jnp.full((1, nq), -jnp.inf, f32), jnp.zeros((1, nq), f32),
                  jnp.zeros((HEAD_DIM, nq), f32)) for _ in range(N_KV))
    _, final = lax.fori_loop(0, n_chunks, body, (logits(0), init))
    for kv in range(N_KV):
        _, l, acc = final[kv]
        _flash_store(ot_ref, kv, acc / l)


def _flash_bounded_kernel(qt_ref, pad_ref, k_ref, vt_ref, ot_ref):
    n_chunks = vt_ref.shape[0]
    tk = vt_ref.shape[2]
    ones = jnp.concatenate([jnp.ones((2 * SUBLANES, tk), bf16),
                            jnp.zeros((PV_ROWS - HEAD_DIM - 2 * SUBLANES, tk), bf16)], axis=0)
    units = [(c, kv) for c in range(n_chunks) for kv in range(N_KV)]

    for t in range(qt_ref.shape[1] // TQ_FLASH):
        cols = slice(t * TQ_FLASH, (t + 1) * TQ_FLASH)
        q_ext = _flash_queries(qt_ref, pad_ref[...], cols)

        def logits(c, kv):
            return _dot(k_ref[c * tk:(c + 1) * tk, :], q_ext[kv])

        s_queue = [logits(*u) for u in units[:FLASH_LOOKAHEAD]]
        acc = [None] * N_KV
        for i, (c, kv) in enumerate(units):
            if i + FLASH_LOOKAHEAD < len(units):
                s_queue.append(logits(*units[i + FLASH_LOOKAHEAD]))
            p = jnp.exp2(s_queue[i]).astype(bf16)
            s_queue[i] = None
            v_ext = jnp.concatenate(
                [vt_ref[c, kv * HEAD_DIM:(kv + 1) * HEAD_DIM, :], ones], axis=0)
            pv = _dot(v_ext, p)
            acc[kv] = pv if acc[kv] is None else acc[kv] + pv
        for kv in range(N_KV):
            _flash_store(ot_ref, kv, acc[kv][0:HEAD_DIM] / acc[kv][HEAD_DIM:HEAD_DIM + 1], cols)


def _flash_attn(qct, pad, kc, vct, seq, bounded):
    t = kc.shape[0]
    tq = TQ_FLASH * (FLASH_SUBTILES if bounded else 1)
    q_tiles = seq // tq
    tk = vct.shape[2]
    q_spec = pl.BlockSpec((Q_WIDTH, tq), lambda b, i: (0, b * q_tiles + i))
    kv_specs = [
        pl.BlockSpec((seq, 2 * KV_WIDTH), lambda b, i: (b, 0)),
        pl.BlockSpec((seq // tk, KV_WIDTH, tk), lambda b, i: (b, 0, 0)),
    ]
    pad_spec = pl.BlockSpec((KV_WIDTH, GROUP * TQ_FLASH), lambda b, i: (0, 0))
    return pl.pallas_call(
        _flash_bounded_kernel if bounded else _flash_online_kernel,
        grid=(t // seq, q_tiles),
        in_specs=[q_spec] + ([pad_spec] if bounded else []) + kv_specs,
        out_specs=q_spec,
        out_shape=jax.ShapeDtypeStruct((Q_WIDTH, t), bf16),
        compiler_params=pltpu.CompilerParams(
            dimension_semantics=("parallel", "parallel"), vmem_limit_bytes=VMEM_LIMIT),
        name="flash_bounded" if bounded else "flash_online",
    )(*([qct] + ([pad] if bounded else []) + [kc, vct]))


def _merge_kernel(x_ref, g1_ref, oa_ref, obt_ref, oct_ref, wg_ref, bg_ref, wbr_ref, wo_ref,
                  y_ref):
    tm = x_ref.shape[0]
    blocks = [slice(r, r + MERGE_ROWS) for r in range(0, tm, MERGE_ROWS)]

    def projections(rows):
        x = x_ref[rows, :]
        xn = (x * _rms_scale(x) * g1_ref[...]).astype(bf16)
        projs = (_dot(oa_ref[rows, :], wbr_ref[0]), _dot_tn(obt_ref[:, rows], wbr_ref[1]),
                 _dot_tn(oct_ref[:, rows], wbr_ref[2]))
        logits = [_dot(xn, wg_ref[:, n * D_MODEL:(n + 1) * D_MODEL]) for n in range(N_BRANCH)]
        return x, projs, logits

    staged = [projections(rows) for rows in blocks]
    for rows, (x, projs, logits) in zip(blocks, staged):
        merged = None
        for n, proj in enumerate(projs):
            gate = _sigmoid(logits[n] + bg_ref[:, n * D_MODEL:(n + 1) * D_MODEL])
            term = gate * proj
            merged = term if merged is None else merged + term
        y_ref[rows, :] = x + _dot(merged.astype(bf16), wo_ref[...])


def _merge(x2d, g1, oa, obt, oct, wg, bg, wbr, wo, layer):
    t = x2d.shape[0]
    tm = TM_MERGE
    const = lambda i: (0, 0)
    row = lambda width: pl.BlockSpec((tm, width), lambda i: (i, 0))
    return pl.pallas_call(
        _merge_kernel,
        grid=(t // tm,),
        in_specs=[
            row(D_MODEL),
            pl.BlockSpec((1, D_MODEL), const),
            row(BR_WIDTH),
            pl.BlockSpec((BR_WIDTH, tm), lambda i: (0, i)),
            pl.BlockSpec((BR_WIDTH, tm), lambda i: (0, i)),
            _resident((D_MODEL, N_BRANCH * D_MODEL), layer),
            pl.BlockSpec((1, N_BRANCH * D_MODEL), const),
            _resident((N_BRANCH, BR_WIDTH, D_MODEL), layer),
            _resident((D_MODEL, D_MODEL), layer),
        ],
        out_specs=row(D_MODEL),
        out_shape=jax.ShapeDtypeStruct((t, D_MODEL), f32),
        compiler_params=pltpu.CompilerParams(
            dimension_semantics=("parallel",), vmem_limit_bytes=VMEM_LIMIT),
        name="merge",
    )(x2d, g1, oa, obt, oct, wg, bg, wbr, wo)


def _conv_ffn_kernel(tiles_per_seq, apply_final, xp_ref, x_ref, xq_ref, g2_ref, wup_ref,
                     cw_ref, cb_ref, wdn_ref, gf_ref, y_ref, act_ref):
    tm = x_ref.shape[0]
    i = pl.program_id(0) % tiles_per_seq
    x = x_ref[...]
    halo_p = jnp.where(i > 0, xp_ref[...], 0.0)
    halo_n = jnp.where(i < tiles_per_seq - 1, xq_ref[...], 0.0)
    xe = jnp.concatenate([halo_p, x, halo_n], axis=0)
    xn = (xe * _rms_scale(xe) * g2_ref[...]).astype(bf16)
    rows = tm + 2 * SUBLANES
    n_chunks = D_FF // FF_CHUNK
    chunk_cols = lambda c: (slice(c * FF_CHUNK, (c + 1) * FF_CHUNK),
                            slice(D_FF + c * FF_CHUNK, D_FF + (c + 1) * FF_CHUNK))

    def up(c):
        return tuple(_dot(xn, wup_ref[:, cols]) for cols in chunk_cols(c))

    h_queue = [up(c) for c in range(FFN_LOOKAHEAD)]
    for c in range(n_chunks):
        if c + FFN_LOOKAHEAD < n_chunks:
            h_queue.append(up(c + FFN_LOOKAHEAD))
        parts = []
        for h, cols in zip(h_queue[c], chunk_cols(c)):
            hc = (pltpu.roll(h, 1, 0) * cw_ref[0:1, cols]
                  + h * cw_ref[1:2, cols]
                  + pltpu.roll(h, rows - 1, 0) * cw_ref[2:3, cols]
                  + cb_ref[:, cols])
            parts.append(hc[SUBLANES:SUBLANES + tm])
        h_queue[c] = None
        act_ref[:, chunk_cols(c)[0]] = (parts[0] * _sigmoid(parts[0]) * parts[1]).astype(bf16)
    y = x + _dot(act_ref[...], wdn_ref[...])
    if apply_final:
        y = y * _rms_scale(y) * gf_ref[...]
    y_ref[...] = y


def _conv_ffn(x2d, seq, g2, wup, cw, cb, wdn, gf, layer, apply_final):
    t = x2d.shape[0]
    tm = TM_FFN
    tiles_per_seq = seq // tm
    halo_per_tile = tm // SUBLANES
    n_halo = t // SUBLANES
    const = lambda i: (0, 0)
    return pl.pallas_call(
        functools.partial(_conv_ffn_kernel, tiles_per_seq, apply_final),
        grid=(t // tm,),
        in_specs=[
            pl.BlockSpec((SUBLANES, D_MODEL),
                         lambda i: (jnp.maximum(i * halo_per_tile - 1, 0), 0)),
            pl.BlockSpec((tm, D_MODEL), lambda i: (i, 0)),
            pl.BlockSpec((SUBLANES, D_MODEL),
                         lambda i: (jnp.minimum((i + 1) * halo_per_tile, n_halo - 1), 0)),
            pl.BlockSpec((1, D_MODEL), const),
            _resident((D_MODEL, 2 * D_FF), layer),
            pl.BlockSpec((3, 2 * D_FF), const),
            pl.BlockSpec((1, 2 * D_FF), const),
            _resident((D_FF, D_MODEL), layer),
            pl.BlockSpec((1, D_MODEL), const),
        ],
        out_specs=pl.BlockSpec((tm, D_MODEL), lambda i: (i, 0)),
        out_shape=jax.ShapeDtypeStruct((t, D_MODEL), f32),
        scratch_shapes=[pltpu.VMEM((tm, D_FF), bf16)],
        compiler_params=pltpu.CompilerParams(
            dimension_semantics=("parallel",), vmem_limit_bytes=VMEM_LIMIT),
        name="conv_ffn",
    )(x2d, x2d, x2d, g2, wup, cw, cb, wdn, gf)


def _t5_bucket(rel):
    half = N_BUCKETS // 2
    max_exact = half // 2
    ret = jnp.where(rel > 0, half, 0)
    n = jnp.abs(rel)
    nf = jnp.maximum(n, 1).astype(f32)
    large = max_exact + (jnp.log(nf / max_exact) / math.log(MAX_DIST / max_exact)
                         * (half - max_exact)).astype(jnp.int32)
    large = jnp.minimum(large, half - 1)
    return ret + jnp.where(n < max_exact, n, large)


def _band_buckets():
    jpos = jnp.arange(3 * BLOCK)[:, None]
    qpos = jnp.arange(BLOCK)[None, :]
    rel = jpos - BLOCK - qpos
    return jnp.where(jnp.abs(rel) <= WINDOW, _t5_bucket(rel), -1).astype(jnp.int32)


def _rope_tables(seq):
    m = HEAD_DIM // 4
    pos = jnp.arange(seq)
    row = (pos // GRID_W).astype(f32)
    col = (pos % GRID_W).astype(f32)
    inv = ROPE_THETA ** (-jnp.arange(m, dtype=f32) / m)
    ang_r = row[:, None] * inv[None, :]
    ang_c = col[:, None] * inv[None, :]
    cos = jnp.concatenate([jnp.cos(ang_r), jnp.cos(ang_r), jnp.cos(ang_c), jnp.cos(ang_c)], axis=-1)
    sin = jnp.concatenate([-jnp.sin(ang_r), jnp.sin(ang_r), -jnp.sin(ang_c), jnp.sin(ang_c)], axis=-1)
    reps = LANES // HEAD_DIM
    return jnp.tile(cos, (1, reps)), jnp.tile(sin, (1, reps)), cos.T, sin.T


def _trunk(x, layers, stacked, bias, seg, final_g):
    bsz, seq, d = x.shape
    x2d = x.reshape(bsz * seq, d)
    cos_t, sin_t, cos_tt, sin_tt = _rope_tables(seq)
    for l, p in enumerate(layers):
        oa, kb, kc, qbt, qct, vbt, vct = _in_proj(
            x2d, seq, p["g1"], p["w_in"], p["wqt"], p["wvt"], p["lng"], p["lnb"], p["wsp"],
            p["bsp"], p["qgt"], p["kg"], cos_t, sin_t, cos_tt, sin_tt, seg)
        obt = _band_attn(qbt, kb, vbt, seq, p["sink"], bias)
        oct = lax.cond(
            p["logit_bound"] <= MAX_LOGIT_BOUND,
            lambda qct, pad, kc, vct: _flash_attn(qct, pad, kc, vct, seq, bounded=True),
            lambda qct, pad, kc, vct: _flash_attn(qct, pad, kc, vct, seq, bounded=False),
            qct, p["pad"], kc, vct)
        x2d = _merge(x2d, p["g1"], oa, obt, oct, stacked["wg"], p["bg"], stacked["wbr"],
                     stacked["wo"], l)
        x2d = _conv_ffn(x2d, seq, p["g2"], stacked["wup"], p["cw"], p["cb"], stacked["wdn"],
                        final_g, l, apply_final=(l == len(layers) - 1))
    return x2d.reshape(bsz, seq, d)


def kernel(x_prompt, x_sample, rel_bias, norm1_g, w_in, ln_v_g, ln_v_b, w_spatial, b_spatial,
           sink, q_norm_g, k_norm_g, w_gate, b_gate, w_branch, w_out, norm2_g, w_up, conv_w,
           conv_b, w_down, final_g):
    bias = _band_bias(rel_bias, _band_buckets())
    head_of_lane = jnp.arange(LANES) // HEAD_DIM
    seg = ((head_of_lane[:, None] == head_of_lane[None, :]).astype(f32) / HEAD_DIM).astype(bf16)
    reps = LANES // HEAD_DIM
    layers = []
    for l in range(DEPTH):
        ws = w_spatial[l].astype(bf16)
        wsp = jnp.concatenate([ws[0::2], ws[1::2]], axis=-1)
        wl = w_in[l].astype(bf16)
        b_q, b_k, b_v = A_IN, A_IN + Q_WIDTH, A_IN + Q_WIDTH + KV_WIDTH
        c_q, c_k, c_v = (b + QKV_WIDTH for b in (b_q, b_k, b_v))
        logit_bound = (1.02 * HEAD_DIM ** 0.5) * jnp.max(jnp.abs(q_norm_g[l])) * jnp.max(
            jnp.abs(k_norm_g[l]))
        shift = -(logit_bound * (LOG2E * (1.0 + 2.0 ** -7))).astype(bf16)
        pad = jnp.zeros((KV_WIDTH, GROUP * TQ_FLASH), bf16).at[0, :].set(shift)
        layers.append(dict(
            logit_bound=logit_bound,
            pad=pad,
            g1=norm1_g[l][None, :],
            w_in=jnp.concatenate(
                [wl[:, :A_IN], wl[:, b_k:b_k + KV_WIDTH], wl[:, c_k:c_k + KV_WIDTH]], axis=1),
            wqt=jnp.concatenate([wl[:, b_q:b_q + Q_WIDTH], wl[:, c_q:c_q + Q_WIDTH]], axis=1).T,
            wvt=jnp.concatenate([wl[:, b_v:b_v + KV_WIDTH], wl[:, c_v:c_v + KV_WIDTH]], axis=1).T,
            lng=ln_v_g[l][None, :],
            lnb=ln_v_b[l][None, :],
            wsp=wsp,
            bsp=jnp.repeat(b_spatial[l].T, HEAD_DIM, axis=1),
            qgt=jnp.broadcast_to(q_norm_g[l][:, None], (HEAD_DIM, TM_IN)),
            kg=jnp.tile(k_norm_g[l], reps)[None, :],
            sink=jnp.repeat(sink[l] * LOG2E, BLOCK).reshape(N_KV, 1, GROUP * BLOCK),
            bg=b_gate[l][None, :],
            g2=norm2_g[l][None, :],
            cw=conv_w[l],
            cb=conv_b[l][None, :],
        ))
    stacked = dict(wg=w_gate.astype(bf16), wbr=w_branch.astype(bf16), wo=w_out.astype(bf16),
                   wup=w_up.astype(bf16), wdn=w_down.astype(bf16))
    gf = final_g[None, :]
    y_prompt = _trunk(x_prompt, layers, stacked, bias, seg, gf)
    y_sample = _trunk(x_sample, layers, stacked, bias, seg, gf)
    return (y_prompt, y_sample)
```

```python
import functools
import math

import jax
import jax.numpy as jnp
from jax import lax
from jax.experimental import pallas as pl
from jax.experimental.pallas import tpu as pltpu

D_MODEL = 1024
DEPTH = 2
HEAD_DIM = 64
BLOCK = 128
A_GROUPS = 8
A_WIDTH = A_GROUPS * HEAD_DIM
N_HEADS = 8
N_KV = 2
GROUP = N_HEADS // N_KV
WINDOW = 128
ROPE_THETA = 10000.0
GRID_W = 64
N_BUCKETS = 32
MAX_DIST = 128
D_FF = 2816
EPS = 1e-6
N_BRANCH = 3
BR_WIDTH = 512
Q_WIDTH = N_HEADS * HEAD_DIM
KV_WIDTH = N_KV * HEAD_DIM
QKV_WIDTH = Q_WIDTH + 2 * KV_WIDTH
A_IN = 2 * A_WIDTH

LANES = 128
SUBLANES = 8
VMEM_LIMIT = 56 * 1024 * 1024

TM_IN = 512
TM_MERGE = 512
MERGE_ROWS = 256
TM_FFN = 512
FF_CHUNK = 256
FFN_LOOKAHEAD = 2
FFN_UP_ROWS = 176
BAND_R = 16
BAND_LOOKAHEAD = 2
PV_ROWS = 128
TQ_FLASH = 128
FLASH_SUBTILES = 4
TK_FLASH = 256
FLASH_LOOKAHEAD = 2
LOG2E = math.log2(math.e)
MAX_LOGIT_BOUND = 30.0

f32 = jnp.float32
bf16 = jnp.bfloat16


def _resident(shape, layer=None):
    if layer is None:
        return pl.BlockSpec(shape, lambda *_: (0,) * len(shape), pipeline_mode=pl.Buffered(1))
    return pl.BlockSpec((None,) + tuple(shape), lambda *_: (layer,) + (0,) * len(shape),
                        pipeline_mode=pl.Buffered(1))


def _zero_like(x):
    bits = pltpu.bitcast(x, jnp.uint32)
    return pltpu.bitcast((bits >> 16) >> 16, f32)


def _rms_scale(x):
    return lax.rsqrt(jnp.mean(x * x, axis=-1, keepdims=True) + EPS)


def _gelu_tanh(x):
    c = math.sqrt(2.0 / math.pi)
    return x * (0.5 * (1.0 + jnp.tanh(c * (x + 0.044715 * (x * x * x)))))


def _sigmoid(x):
    return 1.0 / (1.0 + jnp.exp(-x))


def _dot(a, b):
    return jnp.dot(a, b, preferred_element_type=f32)


def _dot_tn(a, b):
    return lax.dot_general(a, b, (((0,), (0,)), ((), ())), preferred_element_type=f32)


def _dot_nt(a, b):
    return lax.dot_general(a, b, (((1,), (1,)), ((), ())), preferred_element_type=f32)


def _head_mean_sq(x, seg_ref):
    sq = x * x
    hi = sq.astype(bf16)
    lo = (sq - hi.astype(f32)).astype(bf16)
    seg = seg_ref[...]
    return _dot(hi, seg) + _dot(lo, seg)


def _swap16(x):
    lane = lax.broadcasted_iota(jnp.int32, x.shape, 1)
    first_half = (lane % 32) < 16
    return jnp.where(first_half, pltpu.roll(x, LANES - 16, 1), pltpu.roll(x, 16, 1))


def _swap16_rows(x):
    h = HEAD_DIM // 4
    return jnp.concatenate([x[h:2 * h], x[0:h], x[3 * h:4 * h], x[2 * h:3 * h]], axis=0)


def _in_proj_kernel(x_ref, g1_ref, w_ref, wqt_ref, wvt_ref, lng_ref, lnb_ref, wsp_ref, bsp_ref,
                    qgt_ref, kg_ref, cos_ref, sin_ref, cost_ref, sint_ref, seg_ref,
                    oa_ref, kb_ref, kc_ref, qbt_ref, qct_ref, vbt_ref, vct_ref):
    tm = x_ref.shape[0]
    x = x_ref[...]
    xn = (x * _rms_scale(x) * g1_ref[...]).astype(bf16)

    zv = _dot(xn, w_ref[:, A_WIDTH:A_IN])
    zu = _dot(xn, w_ref[:, 0:A_WIDTH])
    zk = _dot(xn, w_ref[:, A_IN:A_IN + 2 * KV_WIDTH])
    qbt = _dot_nt(wqt_ref[0:Q_WIDTH, :], xn)
    qt = _dot_nt(wqt_ref[Q_WIDTH:2 * Q_WIDTH, :], xn)
    vt = _dot_nt(wvt_ref[...], xn).astype(bf16)

    v = _gelu_tanh(zv)
    u = _gelu_tanh(zu)
    mu = jnp.mean(v, axis=-1, keepdims=True)
    vc = v - mu
    vn = vc * lax.rsqrt(jnp.mean(vc * vc, axis=-1, keepdims=True) + EPS)
    vn = vn * lng_ref[...] + lnb_ref[...]
    lane = lax.broadcasted_iota(jnp.int32, (BLOCK, LANES), 1)
    low_half = lane < HEAD_DIM
    for c in range(tm // BLOCK):
        rows = slice(c * BLOCK, (c + 1) * BLOCK)
        for j in range(A_WIDTH // LANES):
            cols = slice(j * LANES, (j + 1) * LANES)
            vp = vn[rows, cols]
            stacked = jnp.concatenate(
                [jnp.where(low_half, vp, 0.0), jnp.where(low_half, 0.0, vp)], axis=0)
            sv = _dot(wsp_ref[j], stacked.astype(bf16)) + bsp_ref[:, cols]
            oa_ref[rows, cols] = (u[rows, cols] * sv).astype(bf16)

    kb_ref[...] = zk[:, 0:KV_WIDTH].astype(bf16)

    t = zk[:, KV_WIDTH:2 * KV_WIDTH]
    tn = t * lax.rsqrt(_head_mean_sq(t, seg_ref) + EPS) * kg_ref[...]
    kc_ref[:, 0:KV_WIDTH] = (tn * cos_ref[...] + _swap16(tn) * sin_ref[...]).astype(bf16)
    lane = lax.broadcasted_iota(jnp.int32, (tm, KV_WIDTH), 1)
    kc_ref[:, KV_WIDTH:2 * KV_WIDTH] = jnp.where(lane == 0, 1.0, 0.0).astype(bf16)

    scale = HEAD_DIM ** -0.5 * LOG2E
    qbt_ref[...] = (qbt * scale).astype(bf16)
    cost = cost_ref[...]
    sint = sint_ref[...]
    for h in range(N_HEADS):
        rows = slice(h * HEAD_DIM, (h + 1) * HEAD_DIM)
        th = qt[rows]
        r = lax.rsqrt(jnp.mean(th * th, axis=0, keepdims=True) + EPS)
        tn = th * r * qgt_ref[...]
        qct_ref[rows, :] = ((tn * cost + _swap16_rows(tn) * sint) * scale).astype(bf16)
    for c in range(tm // BLOCK):
        vbt_ref[c] = vt[0:KV_WIDTH, c * BLOCK:(c + 1) * BLOCK]
    for c in range(tm // TK_FLASH):
        vct_ref[c] = vt[KV_WIDTH:2 * KV_WIDTH, c * TK_FLASH:(c + 1) * TK_FLASH]


def _in_proj(x2d, seq, g1, w_in, wqt, wvt, lng, lnb, wsp, bsp, qgt, kg, cos_t, sin_t,
             cos_tt, sin_tt, seg):
    t = x2d.shape[0]
    tm = TM_IN
    tiles_per_seq = seq // tm
    const = lambda i: (0, 0)
    return pl.pallas_call(
        _in_proj_kernel,
        grid=(t // tm,),
        in_specs=[
            pl.BlockSpec((tm, D_MODEL), lambda i: (i, 0)),
            pl.BlockSpec((1, D_MODEL), const),
            _resident((D_MODEL, w_in.shape[1])),
            _resident((2 * Q_WIDTH, D_MODEL)),
            _resident((2 * KV_WIDTH, D_MODEL)),
            pl.BlockSpec((1, A_WIDTH), const),
            pl.BlockSpec((1, A_WIDTH), const),
            pl.BlockSpec((A_WIDTH // LANES, BLOCK, 2 * BLOCK), lambda i: (0, 0, 0)),
            pl.BlockSpec((BLOCK, A_WIDTH), const),
            pl.BlockSpec((HEAD_DIM, tm), const),
            pl.BlockSpec((1, LANES), const),
            pl.BlockSpec((tm, LANES), lambda i: (i % tiles_per_seq, 0)),
            pl.BlockSpec((tm, LANES), lambda i: (i % tiles_per_seq, 0)),
            pl.BlockSpec((HEAD_DIM, tm), lambda i: (0, i % tiles_per_seq)),
            pl.BlockSpec((HEAD_DIM, tm), lambda i: (0, i % tiles_per_seq)),
            pl.BlockSpec((LANES, LANES), const),
        ],
        out_specs=[
            pl.BlockSpec((tm, A_WIDTH), lambda i: (i, 0)),
            pl.BlockSpec((tm, KV_WIDTH), lambda i: (i, 0)),
            pl.BlockSpec((tm, 2 * KV_WIDTH), lambda i: (i, 0)),
            pl.BlockSpec((Q_WIDTH, tm), lambda i: (0, i)),
            pl.BlockSpec((Q_WIDTH, tm), lambda i: (0, i)),
            pl.BlockSpec((tm // BLOCK, KV_WIDTH, BLOCK), lambda i: (i, 0, 0)),
            pl.BlockSpec((tm // TK_FLASH, KV_WIDTH, TK_FLASH), lambda i: (i, 0, 0)),
        ],
        out_shape=[
            jax.ShapeDtypeStruct((t, A_WIDTH), bf16),
            jax.ShapeDtypeStruct((t, KV_WIDTH), bf16),
            jax.ShapeDtypeStruct((t, 2 * KV_WIDTH), bf16),
            jax.ShapeDtypeStruct((Q_WIDTH, t), bf16),
            jax.ShapeDtypeStruct((Q_WIDTH, t), bf16),
            jax.ShapeDtypeStruct((t // BLOCK, KV_WIDTH, BLOCK), bf16),
            jax.ShapeDtypeStruct((t // TK_FLASH, KV_WIDTH, TK_FLASH), bf16),
        ],
        compiler_params=pltpu.CompilerParams(
            dimension_semantics=("parallel",), vmem_limit_bytes=VMEM_LIMIT),
        name="in_proj",
    )(x2d, g1, w_in, wqt, wvt, lng, lnb, wsp, bsp, qgt, kg, cos_t, sin_t, cos_tt, sin_tt, seg)


def _band_bias_kernel(rel_bias_ref, bucket_ref, bias_ref):
    bucket = bucket_ref[...]
    for h in range(N_HEADS):
        val = jnp.full(bucket.shape, -jnp.inf, f32)
        for b in range(N_BUCKETS):
            val = jnp.where(bucket == b, rel_bias_ref[b, h] * LOG2E, val)
        gi = h % GROUP
        bias_ref[h // GROUP, :, gi * BLOCK:(gi + 1) * BLOCK] = val


def _band_bias(rel_bias, bucket_t):
    return pl.pallas_call(
        _band_bias_kernel,
        in_specs=[pl.BlockSpec(memory_space=pltpu.SMEM),
                  pl.BlockSpec((3 * BLOCK, BLOCK), lambda: (0, 0))],
        out_specs=pl.BlockSpec((N_KV, 3 * BLOCK, GROUP * BLOCK), lambda: (0, 0, 0)),
        out_shape=jax.ShapeDtypeStruct((N_KV, 3 * BLOCK, GROUP * BLOCK), f32),
        name="band_bias",
    )(rel_bias, bucket_t)


def _band_attn_kernel(nb, qt_ref, kp_ref, km_ref, kn_ref, vp_ref, vm_ref, vn_ref,
                      bias_ref, sink_ref, ot_ref):
    first = (pl.program_id(0) * BAND_R) % nb == 0
    last = ((pl.program_id(0) + 1) * BAND_R) % nb == 0
    k_win = jnp.concatenate([kp_ref[...], km_ref[...], kn_ref[...]], axis=0)
    v_blocks = [vp_ref[0]] + [vm_ref[r] for r in range(BAND_R)] + [vn_ref[0]]
    ones = jnp.ones((2 * SUBLANES, 3 * BLOCK), bf16)
    zeros = jnp.zeros((HEAD_DIM, GROUP * BLOCK), bf16)
    units = [(r, kv) for r in range(BAND_R) for kv in range(N_KV)]

    def logits(r, kv):
        qg = jnp.concatenate(
            [qt_ref[h * HEAD_DIM:(h + 1) * HEAD_DIM, r * BLOCK:(r + 1) * BLOCK]
             for h in range(kv * GROUP, (kv + 1) * GROUP)], axis=1)
        q_ext = jnp.concatenate([qg, zeros] if kv == 0 else [zeros, qg], axis=0)
        return _dot(k_win[r * BLOCK:(r + 3) * BLOCK], q_ext)

    s_queue = [logits(*u) for u in units[:BAND_LOOKAHEAD]]
    for i, (r, kv) in enumerate(units):
        if i + BAND_LOOKAHEAD < len(units):
            s_queue.append(logits(*units[i + BAND_LOOKAHEAD]))
        s = s_queue[i] + bias_ref[kv]
        s_queue[i] = None
        if r == 0:
            s = jnp.concatenate([jnp.where(first, -jnp.inf, s[0:BLOCK]), s[BLOCK:]], axis=0)
        if r == BAND_R - 1:
            s = jnp.concatenate([s[:2 * BLOCK], jnp.where(last, -jnp.inf, s[2 * BLOCK:])], axis=0)
        sink = sink_ref[kv]
        if i + 1 < len(units):
            sink = sink + _zero_like(s_queue[i + 1][0:1, :])
        m = jnp.maximum(jnp.max(s, axis=0, keepdims=True), sink)
        p = jnp.exp2(s - m).astype(bf16)
        v_band = jnp.concatenate(
            [blk[kv * HEAD_DIM:(kv + 1) * HEAD_DIM] for blk in v_blocks[r:r + 3]], axis=1)
        pv = _dot(jnp.concatenate([v_band, ones], axis=0), p)
        out = pv[0:HEAD_DIM] / (pv[HEAD_DIM:HEAD_DIM + 1] + jnp.exp2(sink - m))
        for gi in range(GROUP):
            h = kv * GROUP + gi
            ot_ref[h * HEAD_DIM:(h + 1) * HEAD_DIM, r * BLOCK:(r + 1) * BLOCK] = (
                out[:, gi * BLOCK:(gi + 1) * BLOCK].astype(bf16))


def _band_attn(qbt, kb, vbt, seq, sink_rows, bias_t):
    t = kb.shape[0]
    nb = seq // BLOCK
    nblocks = t // BLOCK
    r = BAND_R
    prev_blk = lambda i: jnp.maximum(i * r - 1, 0)
    next_blk = lambda i: jnp.minimum((i + 1) * r, nblocks - 1)
    return pl.pallas_call(
        functools.partial(_band_attn_kernel, nb),
        grid=(nblocks // r,),
        in_specs=[
            pl.BlockSpec((Q_WIDTH, r * BLOCK), lambda i: (0, i)),
            pl.BlockSpec((BLOCK, KV_WIDTH), lambda i: (prev_blk(i), 0)),
            pl.BlockSpec((r * BLOCK, KV_WIDTH), lambda i: (i, 0)),
            pl.BlockSpec((BLOCK, KV_WIDTH), lambda i: (next_blk(i), 0)),
            pl.BlockSpec((1, KV_WIDTH, BLOCK), lambda i: (prev_blk(i), 0, 0)),
            pl.BlockSpec((r, KV_WIDTH, BLOCK), lambda i: (i, 0, 0)),
            pl.BlockSpec((1, KV_WIDTH, BLOCK), lambda i: (next_blk(i), 0, 0)),
            pl.BlockSpec((N_KV, 3 * BLOCK, GROUP * BLOCK), lambda i: (0, 0, 0)),
            pl.BlockSpec((N_KV, 1, GROUP * BLOCK), lambda i: (0, 0, 0)),
        ],
        out_specs=pl.BlockSpec((Q_WIDTH, r * BLOCK), lambda i: (0, i)),
        out_shape=jax.ShapeDtypeStruct((Q_WIDTH, t), bf16),
        compiler_params=pltpu.CompilerParams(
            dimension_semantics=("parallel",), vmem_limit_bytes=VMEM_LIMIT),
        name="band_attn",
    )(qbt, kb, kb, kb, vbt, vbt, vbt, bias_t, sink_rows)


def _flash_queries(qt_ref, pad, cols=slice(None)):
    zeros = jnp.zeros((HEAD_DIM, pad.shape[1]), bf16)
    q_ext = []
    for kv in range(N_KV):
        qg = jnp.concatenate(
            [qt_ref[h * HEAD_DIM:(h + 1) * HEAD_DIM, cols]
             for h in range(kv * GROUP, (kv + 1) * GROUP)], axis=1)
        q_ext.append(jnp.concatenate(([qg, zeros] if kv == 0 else [zeros, qg]) + [pad], axis=0))
    return q_ext


def _flash_store(ot_ref, kv, out, cols=slice(None)):
    tq = out.shape[1] // GROUP
    for gi in range(GROUP):
        h = kv * GROUP + gi
        ot_ref[h * HEAD_DIM:(h + 1) * HEAD_DIM, cols] = out[:, gi * tq:(gi + 1) * tq].astype(bf16)


def _flash_online_kernel(qt_ref, k_ref, vt_ref, ot_ref):
    nq = GROUP * qt_ref.shape[1]
    n_chunks = vt_ref.shape[0]
    tk = vt_ref.shape[2]
    q_ext = _flash_queries(qt_ref, jnp.zeros((KV_WIDTH, nq), bf16))

    def logits(c):
        start = pl.multiple_of(c * tk, tk)
        kc = k_ref[pl.ds(start, tk), :]
        return tuple(_dot(kc, q_ext[kv]) for kv in range(N_KV))

    def body(c, carry):
        s_all, stats = carry
        s_next = logits(jnp.minimum(c + 1, n_chunks - 1))
        vt = vt_ref[c]
        out = []
        for kv in range(N_KV):
            m, l, acc = stats[kv]
            s = s_all[kv]
            m_new = jnp.maximum(m, jnp.max(s, axis=0, keepdims=True))
            alpha = jnp.exp2(m - m_new)
            p = jnp.exp2(s - m_new)
            l_new = alpha * l + jnp.sum(p, axis=0, keepdims=True)
            pv = _dot(vt[kv * HEAD_DIM:(kv + 1) * HEAD_DIM], p.astype(bf16))
            out.append((m_new, l_new, alpha * acc + pv))
        return s_next, tuple(out)

    init = tuple((jnp.full((1, nq), -jnp.inf, f32), jnp.zeros((1, nq), f32),
                  jnp.zeros((HEAD_DIM, nq), f32)) for _ in range(N_KV))
    _, final = lax.fori_loop(0, n_chunks, body, (logits(0), init))
    for kv in range(N_KV):
        _, l, acc = final[kv]
        _flash_store(ot_ref, kv, acc / l)


def _flash_bounded_kernel(qt_ref, pad_ref, k_ref, vt_ref, ot_ref):
    n_chunks = vt_ref.shape[0]
    tk = vt_ref.shape[2]
    ones = jnp.concatenate([jnp.ones((2 * SUBLANES, tk), bf16),
                            jnp.zeros((PV_ROWS - HEAD_DIM - 2 * SUBLANES, tk), bf16)], axis=0)
    units = [(c, kv) for c in range(n_chunks) for kv in range(N_KV)]

    for t in range(qt_ref.shape[1] // TQ_FLASH):
        cols = slice(t * TQ_FLASH, (t + 1) * TQ_FLASH)
        q_ext = _flash_queries(qt_ref, pad_ref[...], cols)

        def logits(c, kv):
            return _dot(k_ref[c * tk:(c + 1) * tk, :], q_ext[kv])

        s_queue = [logits(*u) for u in units[:FLASH_LOOKAHEAD]]
        acc = [None] * N_KV
        for i, (c, kv) in enumerate(units):
            if i + FLASH_LOOKAHEAD < len(units):
                s_queue.append(logits(*units[i + FLASH_LOOKAHEAD]))
            p = jnp.exp2(s_queue[i]).astype(bf16)
            s_queue[i] = None
            v_ext = jnp.concatenate(
                [vt_ref[c, kv * HEAD_DIM:(kv + 1) * HEAD_DIM, :], ones], axis=0)
            pv = _dot(v_ext, p)
            acc[kv] = pv if acc[kv] is None else acc[kv] + pv
        for kv in range(N_KV):
            _flash_store(ot_ref, kv, acc[kv][0:HEAD_DIM] / acc[kv][HEAD_DIM:HEAD_DIM + 1], cols)


def _flash_attn(qct, pad, kc, vct, seq, bounded):
    t = kc.shape[0]
    tq = TQ_FLASH * (FLASH_SUBTILES if bounded else 1)
    q_tiles = seq // tq
    tk = vct.shape[2]
    q_spec = pl.BlockSpec((Q_WIDTH, tq), lambda b, i: (0, b * q_tiles + i))
    kv_specs = [
        pl.BlockSpec((seq, 2 * KV_WIDTH), lambda b, i: (b, 0)),
        pl.BlockSpec((seq // tk, KV_WIDTH, tk), lambda b, i: (b, 0, 0)),
    ]
    pad_spec = pl.BlockSpec((KV_WIDTH, GROUP * TQ_FLASH), lambda b, i: (0, 0))
    return pl.pallas_call(
        _flash_bounded_kernel if bounded else _flash_online_kernel,
        grid=(t // seq, q_tiles),
        in_specs=[q_spec] + ([pad_spec] if bounded else []) + kv_specs,
        out_specs=q_spec,
        out_shape=jax.ShapeDtypeStruct((Q_WIDTH, t), bf16),
        compiler_params=pltpu.CompilerParams(
            dimension_semantics=("parallel", "parallel"), vmem_limit_bytes=VMEM_LIMIT),
        name="flash_bounded" if bounded else "flash_online",
    )(*([qct] + ([pad] if bounded else []) + [kc, vct]))


def _merge_kernel(x_ref, g1_ref, oa_ref, obt_ref, oct_ref, wg_ref, bg_ref, wbr_ref, wo_ref,
                  y_ref):
    tm = x_ref.shape[0]
    blocks = [slice(r, r + MERGE_ROWS) for r in range(0, tm, MERGE_ROWS)]

    def projections(rows):
        x = x_ref[rows, :]
        xn = (x * _rms_scale(x) * g1_ref[...]).astype(bf16)
        projs = (_dot(oa_ref[rows, :], wbr_ref[0]), _dot_tn(obt_ref[:, rows], wbr_ref[1]),
                 _dot_tn(oct_ref[:, rows], wbr_ref[2]))
        logits = [_dot(xn, wg_ref[:, n * D_MODEL:(n + 1) * D_MODEL]) for n in range(N_BRANCH)]
        return x, projs, logits

    staged = [projections(rows) for rows in blocks]
    for rows, (x, projs, logits) in zip(blocks, staged):
        merged = None
        for n, proj in enumerate(projs):
            gate = _sigmoid(logits[n] + bg_ref[:, n * D_MODEL:(n + 1) * D_MODEL])
            term = gate * proj
            merged = term if merged is None else merged + term
        y_ref[rows, :] = x + _dot(merged.astype(bf16), wo_ref[...])


def _merge(x2d, g1, oa, obt, oct, wg, bg, wbr, wo, layer):
    t = x2d.shape[0]
    tm = TM_MERGE
    const = lambda i: (0, 0)
    row = lambda width: pl.BlockSpec((tm, width), lambda i: (i, 0))
    return pl.pallas_call(
        _merge_kernel,
        grid=(t // tm,),
        in_specs=[
            row(D_MODEL),
            pl.BlockSpec((1, D_MODEL), const),
            row(BR_WIDTH),
            pl.BlockSpec((BR_WIDTH, tm), lambda i: (0, i)),
            pl.BlockSpec((BR_WIDTH, tm), lambda i: (0, i)),
            _resident((D_MODEL, N_BRANCH * D_MODEL), layer),
            pl.BlockSpec((1, N_BRANCH * D_MODEL), const),
            _resident((N_BRANCH, BR_WIDTH, D_MODEL), layer),
            _resident((D_MODEL, D_MODEL), layer),
        ],
        out_specs=row(D_MODEL),
        out_shape=jax.ShapeDtypeStruct((t, D_MODEL), f32),
        compiler_params=pltpu.CompilerParams(
            dimension_semantics=("parallel",), vmem_limit_bytes=VMEM_LIMIT),
        name="merge",
    )(x2d, g1, oa, obt, oct, wg, bg, wbr, wo)


def _conv_ffn_kernel(tiles_per_seq, apply_final, xp_ref, x_ref, xq_ref, g2_ref, wup_ref,
                     cw_ref, cb_ref, wdn_ref, gf_ref, y_ref, act_ref):
    tm = x_ref.shape[0]
    i = pl.program_id(0) % tiles_per_seq
    x = x_ref[...]
    halo_p = jnp.where(i > 0, xp_ref[...], 0.0)
    halo_n = jnp.where(i < tiles_per_seq - 1, xq_ref[...], 0.0)
    xe = jnp.concatenate([halo_p, x, halo_n], axis=0)
    xn = (xe * _rms_scale(xe) * g2_ref[...]).astype(bf16)
    rows = tm + 2 * SUBLANES
    n_chunks = D_FF // FF_CHUNK
    chunk_cols = lambda c: (slice(c * FF_CHUNK, (c + 1) * FF_CHUNK),
                            slice(D_FF + c * FF_CHUNK, D_FF + (c + 1) * FF_CHUNK))

    def up(c):
        edges = list(range(0, rows, FFN_UP_ROWS)) + [rows]
        return tuple(jnp.concatenate(
            [_dot(xn[a:b], wup_ref[:, cols]) for a, b in zip(edges[:-1], edges[1:])], axis=0)
            for cols in chunk_cols(c))

    h_queue = [up(c) for c in range(FFN_LOOKAHEAD)]
    for c in range(n_chunks):
        if c + FFN_LOOKAHEAD < n_chunks:
            h_queue.append(up(c + FFN_LOOKAHEAD))
        parts = []
        for h, cols in zip(h_queue[c], chunk_cols(c)):
            hc = (pltpu.roll(h, 1, 0) * cw_ref[0:1, cols]
                  + h * cw_ref[1:2, cols]
                  + pltpu.roll(h, rows - 1, 0) * cw_ref[2:3, cols]
                  + cb_ref[:, cols])
            parts.append(hc[SUBLANES:SUBLANES + tm])
        h_queue[c] = None
        act_ref[:, chunk_cols(c)[0]] = (parts[0] * _sigmoid(parts[0]) * parts[1]).astype(bf16)
    y = x + _dot(act_ref[...], wdn_ref[...])
    if apply_final:
        y = y * _rms_scale(y) * gf_ref[...]
    y_ref[...] = y


def _conv_ffn(x2d, seq, g2, wup, cw, cb, wdn, gf, layer, apply_final):
    t = x2d.shape[0]
    tm = TM_FFN
    tiles_per_seq = seq // tm
    halo_per_tile = tm // SUBLANES
    n_halo = t // SUBLANES
    const = lambda i: (0, 0)
    return pl.pallas_call(
        functools.partial(_conv_ffn_kernel, tiles_per_seq, apply_final),
        grid=(t // tm,),
        in_specs=[
            pl.BlockSpec((SUBLANES, D_MODEL),
                         lambda i: (jnp.maximum(i * halo_per_tile - 1, 0), 0)),
            pl.BlockSpec((tm, D_MODEL), lambda i: (i, 0)),
            pl.BlockSpec((SUBLANES, D_MODEL),
                         lambda i: (jnp.minimum((i + 1) * halo_per_tile, n_halo - 1), 0)),
            pl.BlockSpec((1, D_MODEL), const),
            _resident((D_MODEL, 2 * D_FF), layer),
            pl.BlockSpec((3, 2 * D_FF), const),
            pl.BlockSpec((1, 2 * D_FF), const),
            _resident((D_FF, D_MODEL), layer),
            pl.BlockSpec((1, D_MODEL), const),
        ],
        out_specs=pl.BlockSpec((tm, D_MODEL), lambda i: (i, 0)),
        out_shape=jax.ShapeDtypeStruct((t, D_MODEL), f32),
        scratch_shapes=[pltpu.VMEM((tm, D_FF), bf16)],
        compiler_params=pltpu.CompilerParams(
            dimension_semantics=("parallel",), vmem_limit_bytes=VMEM_LIMIT),
        name="conv_ffn",
    )(x2d, x2d, x2d, g2, wup, cw, cb, wdn, gf)


def _t5_bucket(rel):
    half = N_BUCKETS // 2
    max_exact = half // 2
    ret = jnp.where(rel > 0, half, 0)
    n = jnp.abs(rel)
    nf = jnp.maximum(n, 1).astype(f32)
    large = max_exact + (jnp.log(nf / max_exact) / math.log(MAX_DIST / max_exact)
                         * (half - max_exact)).astype(jnp.int32)
    large = jnp.minimum(large, half - 1)
    return ret + jnp.where(n < max_exact, n, large)


def _band_buckets():
    jpos = jnp.arange(3 * BLOCK)[:, None]
    qpos = jnp.arange(BLOCK)[None, :]
    rel = jpos - BLOCK - qpos
    return jnp.where(jnp.abs(rel) <= WINDOW, _t5_bucket(rel), -1).astype(jnp.int32)


def _rope_tables(seq):
    m = HEAD_DIM // 4
    pos = jnp.arange(seq)
    row = (pos // GRID_W).astype(f32)
    col = (pos % GRID_W).astype(f32)
    inv = ROPE_THETA ** (-jnp.arange(m, dtype=f32) / m)
    ang_r = row[:, None] * inv[None, :]
    ang_c = col[:, None] * inv[None, :]
    cos = jnp.concatenate([jnp.cos(ang_r), jnp.cos(ang_r), jnp.cos(ang_c), jnp.cos(ang_c)], axis=-1)
    sin = jnp.concatenate([-jnp.sin(ang_r), jnp.sin(ang_r), -jnp.sin(ang_c), jnp.sin(ang_c)], axis=-1)
    reps = LANES // HEAD_DIM
    return jnp.tile(cos, (1, reps)), jnp.tile(sin, (1, reps)), cos.T, sin.T


def _trunk(x, layers, stacked, bias, seg, final_g):
    bsz, seq, d = x.shape
    x2d = x.reshape(bsz * seq, d)
    cos_t, sin_t, cos_tt, sin_tt = _rope_tables(seq)
    for l, p in enumerate(layers):
        oa, kb, kc, qbt, qct, vbt, vct = _in_proj(
            x2d, seq, p["g1"], p["w_in"], p["wqt"], p["wvt"], p["lng"], p["lnb"], p["wsp"],
            p["bsp"], p["qgt"], p["kg"], cos_t, sin_t, cos_tt, sin_tt, seg)
        obt = _band_attn(qbt, kb, vbt, seq, p["sink"], bias)
        oct = lax.cond(
            p["logit_bound"] <= MAX_LOGIT_BOUND,
            lambda qct, pad, kc, vct: _flash_attn(qct, pad, kc, vct, seq, bounded=True),
            lambda qct, pad, kc, vct: _flash_attn(qct, pad, kc, vct, seq, bounded=False),
            qct, p["pad"], kc, vct)
        x2d = _merge(x2d, p["g1"], oa, obt, oct, stacked["wg"], p["bg"], stacked["wbr"],
                     stacked["wo"], l)
        x2d = _conv_ffn(x2d, seq, p["g2"], stacked["wup"], p["cw"], p["cb"], stacked["wdn"],
                        final_g, l, apply_final=(l == len(layers) - 1))
    return x2d.reshape(bsz, seq, d)


def kernel(x_prompt, x_sample, rel_bias, norm1_g, w_in, ln_v_g, ln_v_b, w_spatial, b_spatial,
           sink, q_norm_g, k_norm_g, w_gate, b_gate, w_branch, w_out, norm2_g, w_up, conv_w,
           conv_b, w_down, final_g):
    bias = _band_bias(rel_bias, _band_buckets())
    head_of_lane = jnp.arange(LANES) // HEAD_DIM
    seg = ((head_of_lane[:, None] == head_of_lane[None, :]).astype(f32) / HEAD_DIM).astype(bf16)
    reps = LANES // HEAD_DIM
    layers = []
    for l in range(DEPTH):
        ws = w_spatial[l].astype(bf16)
        wsp = jnp.concatenate([ws[0::2], ws[1::2]], axis=-1)
        wl = w_in[l].astype(bf16)
        b_q, b_k, b_v = A_IN, A_IN + Q_WIDTH, A_IN + Q_WIDTH + KV_WIDTH
        c_q, c_k, c_v = (b + QKV_WIDTH for b in (b_q, b_k, b_v))
        logit_bound = (1.02 * HEAD_DIM ** 0.5) * jnp.max(jnp.abs(q_norm_g[l])) * jnp.max(
            jnp.abs(k_norm_g[l]))
        shift = -(logit_bound * (LOG2E * (1.0 + 2.0 ** -7))).astype(bf16)
        pad = jnp.zeros((KV_WIDTH, GROUP * TQ_FLASH), bf16).at[0, :].set(shift)
        layers.append(dict(
            logit_bound=logit_bound,
            pad=pad,
            g1=norm1_g[l][None, :],
            w_in=jnp.concatenate(
                [wl[:, :A_IN], wl[:, b_k:b_k + KV_WIDTH], wl[:, c_k:c_k + KV_WIDTH]], axis=1),
            wqt=jnp.concatenate([wl[:, b_q:b_q + Q_WIDTH], wl[:, c_q:c_q + Q_WIDTH]], axis=1).T,
            wvt=jnp.concatenate([wl[:, b_v:b_v + KV_WIDTH], wl[:, c_v:c_v + KV_WIDTH]], axis=1).T,
            lng=ln_v_g[l][None, :],
            lnb=ln_v_b[l][None, :],
            wsp=wsp,
            bsp=jnp.repeat(b_spatial[l].T, HEAD_DIM, axis=1),
            qgt=jnp.broadcast_to(q_norm_g[l][:, None], (HEAD_DIM, TM_IN)),
            kg=jnp.tile(k_norm_g[l], reps)[None, :],
            sink=jnp.repeat(sink[l] * LOG2E, BLOCK).reshape(N_KV, 1, GROUP * BLOCK),
            bg=b_gate[l][None, :],
            g2=norm2_g[l][None, :],
            cw=conv_w[l],
            cb=conv_b[l][None, :],
        ))
    stacked = dict(wg=w_gate.astype(bf16), wbr=w_branch.astype(bf16), wo=w_out.astype(bf16),
                   wup=w_up.astype(bf16), wdn=w_down.astype(bf16))
    gf = final_g[None, :]
    y_prompt = _trunk(x_prompt, layers, stacked, bias, seg, gf)
    y_sample = _trunk(x_sample, layers, stacked, bias, seg, gf)
    return (y_prompt, y_sample)
```

```python
import functools
import math

import jax
import jax.numpy as jnp
from jax import lax
from jax.experimental import pallas as pl
from jax.experimental.pallas import tpu as pltpu

D_MODEL = 1024
DEPTH = 2
HEAD_DIM = 64
BLOCK = 128
A_GROUPS = 8
A_WIDTH = A_GROUPS * HEAD_DIM
N_HEADS = 8
N_KV = 2
GROUP = N_HEADS // N_KV
WINDOW = 128
ROPE_THETA = 10000.0
GRID_W = 64
N_BUCKETS = 32
MAX_DIST = 128
D_FF = 2816
EPS = 1e-6
N_BRANCH = 3
BR_WIDTH = 512
Q_WIDTH = N_HEADS * HEAD_DIM
KV_WIDTH = N_KV * HEAD_DIM
QKV_WIDTH = Q_WIDTH + 2 * KV_WIDTH
A_IN = 2 * A_WIDTH

LANES = 128
SUBLANES = 8
VMEM_LIMIT = 56 * 1024 * 1024

TM_IN = 512
TM_MERGE = 512
MERGE_ROWS = 256
TM_FFN = 512
FF_CHUNK = 256
FFN_LOOKAHEAD = 2
FFN_UP_ROWS = 176
BAND_R = 16
BAND_LOOKAHEAD = 2
PV_ROWS = 128
TQ_FLASH = 128
FLASH_SUBTILES = 4
TK_FLASH = 256
FLASH_LOOKAHEAD = 2
LOG2E = math.log2(math.e)
MAX_LOGIT_BOUND = 30.0

f32 = jnp.float32
bf16 = jnp.bfloat16


def _resident(shape, layer=None):
    if layer is None:
        return pl.BlockSpec(shape, lambda *_: (0,) * len(shape), pipeline_mode=pl.Buffered(1))
    return pl.BlockSpec((None,) + tuple(shape), lambda *_: (layer,) + (0,) * len(shape),
                        pipeline_mode=pl.Buffered(1))


def _zero_like(x):
    bits = pltpu.bitcast(x, jnp.uint32)
    return pltpu.bitcast((bits >> 16) >> 16, f32)


def _rms_scale(x):
    return lax.rsqrt(jnp.mean(x * x, axis=-1, keepdims=True) + EPS)


def _gelu_tanh(x):
    c = math.sqrt(2.0 / math.pi)
    return x * (0.5 * (1.0 + jnp.tanh(c * (x + 0.044715 * (x * x * x)))))


def _sigmoid(x):
    return 1.0 / (1.0 + jnp.exp(-x))


def _dot(a, b):
    return jnp.dot(a, b, preferred_element_type=f32)


def _dot_tn(a, b):
    return lax.dot_general(a, b, (((0,), (0,)), ((), ())), preferred_element_type=f32)


def _dot_nt(a, b):
    return lax.dot_general(a, b, (((1,), (1,)), ((), ())), preferred_element_type=f32)


def _head_mean_sq(x, seg_ref):
    sq = x * x
    hi = sq.astype(bf16)
    lo = (sq - hi.astype(f32)).astype(bf16)
    seg = seg_ref[...]
    return _dot(hi, seg) + _dot(lo, seg)


def _swap16(x):
    lane = lax.broadcasted_iota(jnp.int32, x.shape, 1)
    first_half = (lane % 32) < 16
    return jnp.where(first_half, pltpu.roll(x, LANES - 16, 1), pltpu.roll(x, 16, 1))


def _swap16_rows(x):
    h = HEAD_DIM // 4
    return jnp.concatenate([x[h:2 * h], x[0:h], x[3 * h:4 * h], x[2 * h:3 * h]], axis=0)


def _in_proj_kernel(x_ref, g1_ref, w_ref, wqt_ref, wvt_ref, lng_ref, lnb_ref, wsp_ref, bsp_ref,
                    qgt_ref, kg_ref, cos_ref, sin_ref, cost_ref, sint_ref, seg_ref,
                    oa_ref, kb_ref, kc_ref, qbt_ref, qct_ref, vbt_ref, vct_ref):
    tm = x_ref.shape[0]
    x = x_ref[...]
    xn = (x * _rms_scale(x) * g1_ref[...]).astype(bf16)

    zv = _dot(xn, w_ref[:, A_WIDTH:A_IN])
    zu = _dot(xn, w_ref[:, 0:A_WIDTH])
    zk = _dot(xn, w_ref[:, A_IN:A_IN + 2 * KV_WIDTH])
    qbt = _dot_nt(wqt_ref[0:Q_WIDTH, :], xn)
    qt = _dot_nt(wqt_ref[Q_WIDTH:2 * Q_WIDTH, :], xn)
    vt = _dot_nt(wvt_ref[...], xn).astype(bf16)

    v = _gelu_tanh(zv)
    u = _gelu_tanh(zu)
    mu = jnp.mean(v, axis=-1, keepdims=True)
    vc = v - mu
    vn = vc * lax.rsqrt(jnp.mean(vc * vc, axis=-1, keepdims=True) + EPS)
    vn = vn * lng_ref[...] + lnb_ref[...]
    lane = lax.broadcasted_iota(jnp.int32, (BLOCK, LANES), 1)
    low_half = lane < HEAD_DIM
    for c in range(tm // BLOCK):
        rows = slice(c * BLOCK, (c + 1) * BLOCK)
        for j in range(A_WIDTH // LANES):
            cols = slice(j * LANES, (j + 1) * LANES)
            vp = vn[rows, cols]
            stacked = jnp.concatenate(
                [jnp.where(low_half, vp, 0.0), jnp.where(low_half, 0.0, vp)], axis=0)
            sv = _dot(wsp_ref[j], stacked.astype(bf16)) + bsp_ref[:, cols]
            oa_ref[rows, cols] = (u[rows, cols] * sv).astype(bf16)

    kb_ref[...] = zk[:, 0:KV_WIDTH].astype(bf16)

    t = zk[:, KV_WIDTH:2 * KV_WIDTH]
    tn = t * lax.rsqrt(_head_mean_sq(t, seg_ref) + EPS) * kg_ref[...]
    kc_ref[:, 0:KV_WIDTH] = (tn * cos_ref[...] + _swap16(tn) * sin_ref[...]).astype(bf16)
    lane = lax.broadcasted_iota(jnp.int32, (tm, KV_WIDTH), 1)
    kc_ref[:, KV_WIDTH:2 * KV_WIDTH] = jnp.where(lane == 0, 1.0, 0.0).astype(bf16)

    scale = HEAD_DIM ** -0.5 * LOG2E
    qbt_ref[...] = (qbt * scale).astype(bf16)
    cost = cost_ref[...]
    sint = sint_ref[...]
    for h in range(N_HEADS):
        rows = slice(h * HEAD_DIM, (h + 1) * HEAD_DIM)
        th = qt[rows]
        r = lax.rsqrt(jnp.mean(th * th, axis=0, keepdims=True) + EPS)
        tn = th * r * qgt_ref[...]
        qct_ref[rows, :] = ((tn * cost + _swap16_rows(tn) * sint) * scale).astype(bf16)
    for c in range(tm // BLOCK):
        vbt_ref[c] = vt[0:KV_WIDTH, c * BLOCK:(c + 1) * BLOCK]
    for c in range(tm // TK_FLASH):
        vct_ref[c] = vt[KV_WIDTH:2 * KV_WIDTH, c * TK_FLASH:(c + 1) * TK_FLASH]


def _in_proj(x2d, seq, g1, w_in, wqt, wvt, lng, lnb, wsp, bsp, qgt, kg, cos_t, sin_t,
             cos_tt, sin_tt, seg):
    t = x2d.shape[0]
    tm = TM_IN
    tiles_per_seq = seq // tm
    const = lambda i: (0, 0)
    return pl.pallas_call(
        _in_proj_kernel,
        grid=(t // tm,),
        in_specs=[
            pl.BlockSpec((tm, D_MODEL), lambda i: (i, 0)),
            pl.BlockSpec((1, D_MODEL), const),
            _resident((D_MODEL, w_in.shape[1])),
            _resident((2 * Q_WIDTH, D_MODEL)),
            _resident((2 * KV_WIDTH, D_MODEL)),
            pl.BlockSpec((1, A_WIDTH), const),
            pl.BlockSpec((1, A_WIDTH), const),
            pl.BlockSpec((A_WIDTH // LANES, BLOCK, 2 * BLOCK), lambda i: (0, 0, 0)),
            pl.BlockSpec((BLOCK, A_WIDTH), const),
            pl.BlockSpec((HEAD_DIM, tm), const),
            pl.BlockSpec((1, LANES), const),
            pl.BlockSpec((tm, LANES), lambda i: (i % tiles_per_seq, 0)),
            pl.BlockSpec((tm, LANES), lambda i: (i % tiles_per_seq, 0)),
            pl.BlockSpec((HEAD_DIM, tm), lambda i: (0, i % tiles_per_seq)),
            pl.BlockSpec((HEAD_DIM, tm), lambda i: (0, i % tiles_per_seq)),
            pl.BlockSpec((LANES, LANES), const),
        ],
        out_specs=[
            pl.BlockSpec((tm, A_WIDTH), lambda i: (i, 0)),
            pl.BlockSpec((tm, KV_WIDTH), lambda i: (i, 0)),
            pl.BlockSpec((tm, 2 * KV_WIDTH), lambda i: (i, 0)),
            pl.BlockSpec((Q_WIDTH, tm), lambda i: (0, i)),
            pl.BlockSpec((Q_WIDTH, tm), lambda i: (0, i)),
            pl.BlockSpec((tm // BLOCK, KV_WIDTH, BLOCK), lambda i: (i, 0, 0)),
            pl.BlockSpec((tm // TK_FLASH, KV_WIDTH, TK_FLASH), lambda i: (i, 0, 0)),
        ],
        out_shape=[
            jax.ShapeDtypeStruct((t, A_WIDTH), bf16),
            jax.ShapeDtypeStruct((t, KV_WIDTH), bf16),
            jax.ShapeDtypeStruct((t, 2 * KV_WIDTH), bf16),
            jax.ShapeDtypeStruct((Q_WIDTH, t), bf16),
            jax.ShapeDtypeStruct((Q_WIDTH, t), bf16),
            jax.ShapeDtypeStruct((t // BLOCK, KV_WIDTH, BLOCK), bf16),
            jax.ShapeDtypeStruct((t // TK_FLASH, KV_WIDTH, TK_FLASH), bf16),
        ],
        compiler_params=pltpu.CompilerParams(
            dimension_semantics=("parallel",), vmem_limit_bytes=VMEM_LIMIT),
        name="in_proj",
    )(x2d, g1, w_in, wqt, wvt, lng, lnb, wsp, bsp, qgt, kg, cos_t, sin_t, cos_tt, sin_tt, seg)


def _band_bias_kernel(rel_bias_ref, bucket_ref, bias_ref):
    bucket = bucket_ref[...]
    for h in range(N_HEADS):
        val = jnp.full(bucket.shape, -jnp.inf, f32)
        for b in range(N_BUCKETS):
            val = jnp.where(bucket == b, rel_bias_ref[b, h] * LOG2E, val)
        gi = h % GROUP
        bias_ref[h // GROUP, :, gi * BLOCK:(gi + 1) * BLOCK] = val


def _band_bias(rel_bias, bucket_t):
    return pl.pallas_call(
        _band_bias_kernel,
        in_specs=[pl.BlockSpec(memory_space=pltpu.SMEM),
                  pl.BlockSpec((3 * BLOCK, BLOCK), lambda: (0, 0))],
        out_specs=pl.BlockSpec((N_KV, 3 * BLOCK, GROUP * BLOCK), lambda: (0, 0, 0)),
        out_shape=jax.ShapeDtypeStruct((N_KV, 3 * BLOCK, GROUP * BLOCK), f32),
        name="band_bias",
    )(rel_bias, bucket_t)


def _band_attn_kernel(nb, qt_ref, kp_ref, km_ref, kn_ref, vp_ref, vm_ref, vn_ref,
                      bias_ref, sink_ref, ot_ref):
    first = (pl.program_id(0) * BAND_R) % nb == 0
    last = ((pl.program_id(0) + 1) * BAND_R) % nb == 0
    k_win = jnp.concatenate([kp_ref[...], km_ref[...], kn_ref[...]], axis=0)
    v_blocks = [vp_ref[0]] + [vm_ref[r] for r in range(BAND_R)] + [vn_ref[0]]
    ones = jnp.ones((2 * SUBLANES, 3 * BLOCK), bf16)
    zeros = jnp.zeros((HEAD_DIM, GROUP * BLOCK), bf16)
    units = [(r, kv) for r in range(BAND_R) for kv in range(N_KV)]

    def logits(r, kv):
        qg = jnp.concatenate(
            [qt_ref[h * HEAD_DIM:(h + 1) * HEAD_DIM, r * BLOCK:(r + 1) * BLOCK]
             for h in range(kv * GROUP, (kv + 1) * GROUP)], axis=1)
        q_ext = jnp.concatenate([qg, zeros] if kv == 0 else [zeros, qg], axis=0)
        return jnp.concatenate([_dot(k_win[(r + j) * BLOCK:(r + j + 1) * BLOCK], q_ext)
                                for j in range(3)], axis=0)

    s_queue = [logits(*u) for u in units[:BAND_LOOKAHEAD]]
    for i, (r, kv) in enumerate(units):
        if i + BAND_LOOKAHEAD < len(units):
            s_queue.append(logits(*units[i + BAND_LOOKAHEAD]))
        s = s_queue[i] + bias_ref[kv]
        s_queue[i] = None
        if r == 0:
            s = jnp.concatenate([jnp.where(first, -jnp.inf, s[0:BLOCK]), s[BLOCK:]], axis=0)
        if r == BAND_R - 1:
            s = jnp.concatenate([s[:2 * BLOCK], jnp.where(last, -jnp.inf, s[2 * BLOCK:])], axis=0)
        sink = sink_ref[kv]
        if i + 1 < len(units):
            sink = sink + _zero_like(s_queue[i + 1][0:1, :])
        m = jnp.maximum(jnp.max(s, axis=0, keepdims=True), sink)
        p = jnp.exp2(s - m).astype(bf16)
        v_band = jnp.concatenate(
            [blk[kv * HEAD_DIM:(kv + 1) * HEAD_DIM] for blk in v_blocks[r:r + 3]], axis=1)
        pv = _dot(jnp.concatenate([v_band, ones], axis=0), p)
        out = pv[0:HEAD_DIM] / (pv[HEAD_DIM:HEAD_DIM + 1] + jnp.exp2(sink - m))
        for gi in range(GROUP):
            h = kv * GROUP + gi
            ot_ref[h * HEAD_DIM:(h + 1) * HEAD_DIM, r * BLOCK:(r + 1) * BLOCK] = (
                out[:, gi * BLOCK:(gi + 1) * BLOCK].astype(bf16))


def _band_attn(qbt, kb, vbt, seq, sink_rows, bias_t):
    t = kb.shape[0]
    nb = seq // BLOCK
    nblocks = t // BLOCK
    r = BAND_R
    prev_blk = lambda i: jnp.maximum(i * r - 1, 0)
    next_blk = lambda i: jnp.minimum((i + 1) * r, nblocks - 1)
    return pl.pallas_call(
        functools.partial(_band_attn_kernel, nb),
        grid=(nblocks // r,),
        in_specs=[
            pl.BlockSpec((Q_WIDTH, r * BLOCK), lambda i: (0, i)),
            pl.BlockSpec((BLOCK, KV_WIDTH), lambda i: (prev_blk(i), 0)),
            pl.BlockSpec((r * BLOCK, KV_WIDTH), lambda i: (i, 0)),
            pl.BlockSpec((BLOCK, KV_WIDTH), lambda i: (next_blk(i), 0)),
            pl.BlockSpec((1, KV_WIDTH, BLOCK), lambda i: (prev_blk(i), 0, 0)),
            pl.BlockSpec((r, KV_WIDTH, BLOCK), lambda i: (i, 0, 0)),
            pl.BlockSpec((1, KV_WIDTH, BLOCK), lambda i: (next_blk(i), 0, 0)),
            pl.BlockSpec((N_KV, 3 * BLOCK, GROUP * BLOCK), lambda i: (0, 0, 0)),
            pl.BlockSpec((N_KV, 1, GROUP * BLOCK), lambda i: (0, 0, 0)),
        ],
        out_specs=pl.BlockSpec((Q_WIDTH, r * BLOCK), lambda i: (0, i)),
        out_shape=jax.ShapeDtypeStruct((Q_WIDTH, t), bf16),
        compiler_params=pltpu.CompilerParams(
            dimension_semantics=("parallel",), vmem_limit_bytes=VMEM_LIMIT),
        name="band_attn",
    )(qbt, kb, kb, kb, vbt, vbt, vbt, bias_t, sink_rows)


def _flash_queries(qt_ref, pad, cols=slice(None)):
    zeros = jnp.zeros((HEAD_DIM, pad.shape[1]), bf16)
    q_ext = []
    for kv in range(N_KV):
        qg = jnp.concatenate(
            [qt_ref[h * HEAD_DIM:(h + 1) * HEAD_DIM, cols]
             for h in range(kv * GROUP, (kv + 1) * GROUP)], axis=1)
        q_ext.append(jnp.concatenate(([qg, zeros] if kv == 0 else [zeros, qg]) + [pad], axis=0))
    return q_ext


def _flash_store(ot_ref, kv, out, cols=slice(None)):
    tq = out.shape[1] // GROUP
    for gi in range(GROUP):
        h = kv * GROUP + gi
        ot_ref[h * HEAD_DIM:(h + 1) * HEAD_DIM, cols] = out[:, gi * tq:(gi + 1) * tq].astype(bf16)


def _flash_online_kernel(qt_ref, k_ref, vt_ref, ot_ref):
    nq = GROUP * qt_ref.shape[1]
    n_chunks = vt_ref.shape[0]
    tk = vt_ref.shape[2]
    q_ext = _flash_queries(qt_ref, jnp.zeros((KV_WIDTH, nq), bf16))

    def logits(c):
        start = pl.multiple_of(c * tk, tk)
        kc = k_ref[pl.ds(start, tk), :]
        return tuple(_dot(kc, q_ext[kv]) for kv in range(N_KV))

    def body(c, carry):
        s_all, stats = carry
        s_next = logits(jnp.minimum(c + 1, n_chunks - 1))
        vt = vt_ref[c]
        out = []
        for kv in range(N_KV):
            m, l, acc = stats[kv]
            s = s_all[kv]
            m_new = jnp.maximum(m, jnp.max(s, axis=0, keepdims=True))
            alpha = jnp.exp2(m - m_new)
            p = jnp.exp2(s - m_new)
            l_new = alpha * l + jnp.sum(p, axis=0, keepdims=True)
            pv = _dot(vt[kv * HEAD_DIM:(kv + 1) * HEAD_DIM], p.astype(bf16))
            out.append((m_new, l_new, alpha * acc + pv))
        return s_next, tuple(out)

    init = tuple((jnp.full((1, nq), -jnp.inf, f32), jnp.zeros((1, nq), f32),
                  jnp.zeros((HEAD_DIM, nq), f32)) for _ in range(N_KV))
    _, final = lax.fori_loop(0, n_chunks, body, (logits(0), init))
    for kv in range(N_KV):
        _, l, acc = final[kv]
        _flash_store(ot_ref, kv, acc / l)


def _flash_bounded_kernel(qt_ref, pad_ref, k_ref, vt_ref, ot_ref):
    n_chunks = vt_ref.shape[0]
    tk = vt_ref.shape[2]
    ones = jnp.concatenate([jnp.ones((2 * SUBLANES, tk), bf16),
                            jnp.zeros((PV_ROWS - HEAD_DIM - 2 * SUBLANES, tk), bf16)], axis=0)
    units = [(c, kv) for c in range(n_chunks) for kv in range(N_KV)]

    for t in range(qt_ref.shape[1] // TQ_FLASH):
        cols = slice(t * TQ_FLASH, (t + 1) * TQ_FLASH)
        q_ext = _flash_queries(qt_ref, pad_ref[...], cols)

        def logits(c, kv):
            return _dot(k_ref[c * tk:(c + 1) * tk, :], q_ext[kv])

        s_queue = [logits(*u) for u in units[:FLASH_LOOKAHEAD]]
        acc = [None] * N_KV
        for i, (c, kv) in enumerate(units):
            if i + FLASH_LOOKAHEAD < len(units):
                s_queue.append(logits(*units[i + FLASH_LOOKAHEAD]))
            p = jnp.exp2(s_queue[i]).astype(bf16)
            s_queue[i] = None
            v_ext = jnp.concatenate(
                [vt_ref[c, kv * HEAD_DIM:(kv + 1) * HEAD_DIM, :], ones], axis=0)
            pv = _dot(v_ext, p)
            acc[kv] = pv if acc[kv] is None else acc[kv] + pv
        for kv in range(N_KV):
            _flash_store(ot_ref, kv, acc[kv][0:HEAD_DIM] / acc[kv][HEAD_DIM:HEAD_DIM + 1], cols)


def _flash_attn(qct, pad, kc, vct, seq, bounded):
    t = kc.shape[0]
    tq = TQ_FLASH * (FLASH_SUBTILES if bounded else 1)
    q_tiles = seq // tq
    tk = vct.shape[2]
    q_spec = pl.BlockSpec((Q_WIDTH, tq), lambda b, i: (0, b * q_tiles + i))
    kv_specs = [
        pl.BlockSpec((seq, 2 * KV_WIDTH), lambda b, i: (b, 0)),
        pl.BlockSpec((seq // tk, KV_WIDTH, tk), lambda b, i: (b, 0, 0)),
    ]
    pad_spec = pl.BlockSpec((KV_WIDTH, GROUP * TQ_FLASH), lambda b, i: (0, 0))
    return pl.pallas_call(
        _flash_bounded_kernel if bounded else _flash_online_kernel,
        grid=(t // seq, q_tiles),
        in_specs=[q_spec] + ([pad_spec] if bounded else []) + kv_specs,
        out_specs=q_spec,
        out_shape=jax.ShapeDtypeStruct((Q_WIDTH, t), bf16),
        compiler_params=pltpu.CompilerParams(
            dimension_semantics=("parallel", "parallel"), vmem_limit_bytes=VMEM_LIMIT),
        name="flash_bounded" if bounded else "flash_online",
    )(*([qct] + ([pad] if bounded else []) + [kc, vct]))


def _merge_kernel(x_ref, g1_ref, oa_ref, obt_ref, oct_ref, wg_ref, bg_ref, wbr_ref, wo_ref,
                  y_ref):
    tm = x_ref.shape[0]
    blocks = [slice(r, r + MERGE_ROWS) for r in range(0, tm, MERGE_ROWS)]

    def projections(rows):
        x = x_ref[rows, :]
        xn = (x * _rms_scale(x) * g1_ref[...]).astype(bf16)
        projs = (_dot(oa_ref[rows, :], wbr_ref[0]), _dot_tn(obt_ref[:, rows], wbr_ref[1]),
                 _dot_tn(oct_ref[:, rows], wbr_ref[2]))
        logits = [_dot(xn, wg_ref[:, n * D_MODEL:(n + 1) * D_MODEL]) for n in range(N_BRANCH)]
        return x, projs, logits

    staged = [projections(rows) for rows in blocks]
    for rows, (x, projs, logits) in zip(blocks, staged):
        merged = None
        for n, proj in enumerate(projs):
            gate = _sigmoid(logits[n] + bg_ref[:, n * D_MODEL:(n + 1) * D_MODEL])
            term = gate * proj
            merged = term if merged is None else merged + term
        y_ref[rows, :] = x + _dot(merged.astype(bf16), wo_ref[...])


def _merge(x2d, g1, oa, obt, oct, wg, bg, wbr, wo, layer):
    t = x2d.shape[0]
    tm = TM_MERGE
    const = lambda i: (0, 0)
    row = lambda width: pl.BlockSpec((tm, width), lambda i: (i, 0))
    return pl.pallas_call(
        _merge_kernel,
        grid=(t // tm,),
        in_specs=[
            row(D_MODEL),
            pl.BlockSpec((1, D_MODEL), const),
            row(BR_WIDTH),
            pl.BlockSpec((BR_WIDTH, tm), lambda i: (0, i)),
            pl.BlockSpec((BR_WIDTH, tm), lambda i: (0, i)),
            _resident((D_MODEL, N_BRANCH * D_MODEL), layer),
            pl.BlockSpec((1, N_BRANCH * D_MODEL), const),
            _resident((N_BRANCH, BR_WIDTH, D_MODEL), layer),
            _resident((D_MODEL, D_MODEL), layer),
        ],
        out_specs=row(D_MODEL),
        out_shape=jax.ShapeDtypeStruct((t, D_MODEL), f32),
        compiler_params=pltpu.CompilerParams(
            dimension_semantics=("parallel",), vmem_limit_bytes=VMEM_LIMIT),
        name="merge",
    )(x2d, g1, oa, obt, oct, wg, bg, wbr, wo)


def _conv_ffn_kernel(tiles_per_seq, apply_final, xp_ref, x_ref, xq_ref, g2_ref, wup_ref,
                     cw_ref, cb_ref, wdn_ref, gf_ref, y_ref, act_ref):
    tm = x_ref.shape[0]
    i = pl.program_id(0) % tiles_per_seq
    x = x_ref[...]
    halo_p = jnp.where(i > 0, xp_ref[...], 0.0)
    halo_n = jnp.where(i < tiles_per_seq - 1, xq_ref[...], 0.0)
    xe = jnp.concatenate([halo_p, x, halo_n], axis=0)
    xn = (xe * _rms_scale(xe) * g2_ref[...]).astype(bf16)
    rows = tm + 2 * SUBLANES
    n_chunks = D_FF // FF_CHUNK
    chunk_cols = lambda c: (slice(c * FF_CHUNK, (c + 1) * FF_CHUNK),
                            slice(D_FF + c * FF_CHUNK, D_FF + (c + 1) * FF_CHUNK))

    def up(c):
        edges = list(range(0, rows, FFN_UP_ROWS)) + [rows]
        return tuple(jnp.concatenate(
            [_dot(xn[a:b], wup_ref[:, cols]) for a, b in zip(edges[:-1], edges[1:])], axis=0)
            for cols in chunk_cols(c))

    h_queue = [up(c) for c in range(FFN_LOOKAHEAD)]
    for c in range(n_chunks):
        if c + FFN_LOOKAHEAD < n_chunks:
            h_queue.append(up(c + FFN_LOOKAHEAD))
        parts = []
        for h, cols in zip(h_queue[c], chunk_cols(c)):
            hc = (pltpu.roll(h, 1, 0) * cw_ref[0:1, cols]
                  + h * cw_ref[1:2, cols]
                  + pltpu.roll(h, rows - 1, 0) * cw_ref[2:3, cols]
                  + cb_ref[:, cols])
            parts.append(hc[SUBLANES:SUBLANES + tm])
        h_queue[c] = None
        act_ref[:, chunk_cols(c)[0]] = (parts[0] * _sigmoid(parts[0]) * parts[1]).astype(bf16)
    y = x + _dot(act_ref[...], wdn_ref[...])
    if apply_final:
        y = y * _rms_scale(y) * gf_ref[...]
    y_ref[...] = y


def _conv_ffn(x2d, seq, g2, wup, cw, cb, wdn, gf, layer, apply_final):
    t = x2d.shape[0]
    tm = TM_FFN
    tiles_per_seq = seq // tm
    halo_per_tile = tm // SUBLANES
    n_halo = t // SUBLANES
    const = lambda i: (0, 0)
    return pl.pallas_call(
        functools.partial(_conv_ffn_kernel, tiles_per_seq, apply_final),
        grid=(t // tm,),
        in_specs=[
            pl.BlockSpec((SUBLANES, D_MODEL),
                         lambda i: (jnp.maximum(i * halo_per_tile - 1, 0), 0)),
            pl.BlockSpec((tm, D_MODEL), lambda i: (i, 0)),
            pl.BlockSpec((SUBLANES, D_MODEL),
                         lambda i: (jnp.minimum((i + 1) * halo_per_tile, n_halo - 1), 0)),
            pl.BlockSpec((1, D_MODEL), const),
            _resident((D_MODEL, 2 * D_FF), layer),
            pl.BlockSpec((3, 2 * D_FF), const),
            pl.BlockSpec((1, 2 * D_FF), const),
            _resident((D_FF, D_MODEL), layer),
            pl.BlockSpec((1, D_MODEL), const),
        ],
        out_specs=pl.BlockSpec((tm, D_MODEL), lambda i: (i, 0)),
        out_shape=jax.ShapeDtypeStruct((t, D_MODEL), f32),
        scratch_shapes=[pltpu.VMEM((tm, D_FF), bf16)],
        compiler_params=pltpu.CompilerParams(
            dimension_semantics=("parallel",), vmem_limit_bytes=VMEM_LIMIT),
        name="conv_ffn",
    )(x2d, x2d, x2d, g2, wup, cw, cb, wdn, gf)


def _t5_bucket(rel):
    half = N_BUCKETS // 2
    max_exact = half // 2
    ret = jnp.where(rel > 0, half, 0)
    n = jnp.abs(rel)
    nf = jnp.maximum(n, 1).astype(f32)
    large = max_exact + (jnp.log(nf / max_exact) / math.log(MAX_DIST / max_exact)
                         * (half - max_exact)).astype(jnp.int32)
    large = jnp.minimum(large, half - 1)
    return ret + jnp.where(n < max_exact, n, large)


def _band_buckets():
    jpos = jnp.arange(3 * BLOCK)[:, None]
    qpos = jnp.arange(BLOCK)[None, :]
    rel = jpos - BLOCK - qpos
    return jnp.where(jnp.abs(rel) <= WINDOW, _t5_bucket(rel), -1).astype(jnp.int32)


def _rope_tables(seq):
    m = HEAD_DIM // 4
    pos = jnp.arange(seq)
    row = (pos // GRID_W).astype(f32)
    col = (pos % GRID_W).astype(f32)
    inv = ROPE_THETA ** (-jnp.arange(m, dtype=f32) / m)
    ang_r = row[:, None] * inv[None, :]
    ang_c = col[:, None] * inv[None, :]
    cos = jnp.concatenate([jnp.cos(ang_r), jnp.cos(ang_r), jnp.cos(ang_c), jnp.cos(ang_c)], axis=-1)
    sin = jnp.concatenate([-jnp.sin(ang_r), jnp.sin(ang_r), -jnp.sin(ang_c), jnp.sin(ang_c)], axis=-1)
    reps = LANES // HEAD_DIM
    return jnp.tile(cos, (1, reps)), jnp.tile(sin, (1, reps)), cos.T, sin.T


def _trunk(x, layers, stacked, bias, seg, final_g):
    bsz, seq, d = x.shape
    x2d = x.reshape(bsz * seq, d)
    cos_t, sin_t, cos_tt, sin_tt = _rope_tables(seq)
    for l, p in enumerate(layers):
        oa, kb, kc, qbt, qct, vbt, vct = _in_proj(
            x2d, seq, p["g1"], p["w_in"], p["wqt"], p["wvt"], p["lng"], p["lnb"], p["wsp"],
            p["bsp"], p["qgt"], p["kg"], cos_t, sin_t, cos_tt, sin_tt, seg)
        obt = _band_attn(qbt, kb, vbt, seq, p["sink"], bias)
        oct = lax.cond(
            p["logit_bound"] <= MAX_LOGIT_BOUND,
            lambda qct, pad, kc, vct: _flash_attn(qct, pad, kc, vct, seq, bounded=True),
            lambda qct, pad, kc, vct: _flash_attn(qct, pad, kc, vct, seq, bounded=False),
            qct, p["pad"], kc, vct)
        x2d = _merge(x2d, p["g1"], oa, obt, oct, stacked["wg"], p["bg"], stacked["wbr"],
                     stacked["wo"], l)
        x2d = _conv_ffn(x2d, seq, p["g2"], stacked["wup"], p["cw"], p["cb"], stacked["wdn"],
                        final_g, l, apply_final=(l == len(layers) - 1))
    return x2d.reshape(bsz, seq, d)


def kernel(x_prompt, x_sample, rel_bias, norm1_g, w_in, ln_v_g, ln_v_b, w_spatial, b_spatial,
           sink, q_norm_g, k_norm_g, w_gate, b_gate, w_branch, w_out, norm2_g, w_up, conv_w,
           conv_b, w_down, final_g):
    bias = _band_bias(rel_bias, _band_buckets())
    head_of_lane = jnp.arange(LANES) // HEAD_DIM
    seg = ((head_of_lane[:, None] == head_of_lane[None, :]).astype(f32) / HEAD_DIM).astype(bf16)
    reps = LANES // HEAD_DIM
    layers = []
    for l in range(DEPTH):
        ws = w_spatial[l].astype(bf16)
        wsp = jnp.concatenate([ws[0::2], ws[1::2]], axis=-1)
        wl = w_in[l].astype(bf16)
        b_q, b_k, b_v = A_IN, A_IN + Q_WIDTH, A_IN + Q_WIDTH + KV_WIDTH
        c_q, c_k, c_v = (b + QKV_WIDTH for b in (b_q, b_k, b_v))
        logit_bound = (1.02 * HEAD_DIM ** 0.5) * jnp.max(jnp.abs(q_norm_g[l])) * jnp.max(
            jnp.abs(k_norm_g[l]))
        shift = -(logit_bound * (LOG2E * (1.0 + 2.0 ** -7))).astype(bf16)
        pad = jnp.zeros((KV_WIDTH, GROUP * TQ_FLASH), bf16).at[0, :].set(shift)
        layers.append(dict(
            logit_bound=logit_bound,
            pad=pad,
            g1=norm1_g[l][None, :],
            w_in=jnp.concatenate(
                [wl[:, :A_IN], wl[:, b_k:b_k + KV_WIDTH], wl[:, c_k:c_k + KV_WIDTH]], axis=1),
            wqt=jnp.concatenate([wl[:, b_q:b_q + Q_WIDTH], wl[:, c_q:c_q + Q_WIDTH]], axis=1).T,
            wvt=jnp.concatenate([wl[:, b_v:b_v + KV_WIDTH], wl[:, c_v:c_v + KV_WIDTH]], axis=1).T,
            lng=ln_v_g[l][None, :],
            lnb=ln_v_b[l][None, :],
            wsp=wsp,
            bsp=jnp.repeat(b_spatial[l].T, HEAD_DIM, axis=1),
            qgt=jnp.broadcast_to(q_norm_g[l][:, None], (HEAD_DIM, TM_IN)),
            kg=jnp.tile(k_norm_g[l], reps)[None, :],
            sink=jnp.repeat(sink[l] * LOG2E, BLOCK).reshape(N_KV, 1, GROUP * BLOCK),
            bg=b_gate[l][None, :],
            g2=norm2_g[l][None, :],
            cw=conv_w[l],
            cb=conv_b[l][None, :],
        ))
    stacked = dict(wg=w_gate.astype(bf16), wbr=w_branch.astype(bf16), wo=w_out.astype(bf16),
                   wup=w_up.astype(bf16), wdn=w_down.astype(bf16))
    gf = final_g[None, :]
    y_prompt = _trunk(x_prompt, layers, stacked, bias, seg, gf)
    y_sample = _trunk(x_sample, layers, stacked, bias, seg, gf)
    return (y_prompt, y_sample)
```

```python
import functools
import math

import jax
import jax.numpy as jnp
from jax import lax
from jax.experimental import pallas as pl
from jax.experimental.pallas import tpu as pltpu

D_MODEL = 1024
DEPTH = 2
HEAD_DIM = 64
BLOCK = 128
A_GROUPS = 8
A_WIDTH = A_GROUPS * HEAD_DIM
N_HEADS = 8
N_KV = 2
GROUP = N_HEADS // N_KV
WINDOW = 128
ROPE_THETA = 10000.0
GRID_W = 64
N_BUCKETS = 32
MAX_DIST = 128
D_FF = 2816
EPS = 1e-6
N_BRANCH = 3
BR_WIDTH = 512
Q_WIDTH = N_HEADS * HEAD_DIM
KV_WIDTH = N_KV * HEAD_DIM
QKV_WIDTH = Q_WIDTH + 2 * KV_WIDTH
A_IN = 2 * A_WIDTH

LANES = 128
SUBLANES = 8
VMEM_LIMIT = 56 * 1024 * 1024

TM_IN = 1024
TM_MERGE = 512
MERGE_ROWS = 256
TM_FFN = 1024
FF_CHUNK = 256
FFN_LOOKAHEAD = 2
FFN_UP_ROWS = 176
BAND_R = 16
BAND_LOOKAHEAD = 2
PV_ROWS = 128
TQ_FLASH = 128
FLASH_SUBTILES = 4
TK_FLASH = 256
FLASH_LOOKAHEAD = 2
LOG2E = math.log2(math.e)
MAX_LOGIT_BOUND = 30.0

f32 = jnp.float32
bf16 = jnp.bfloat16


def _resident(shape, layer=None):
    if layer is None:
        return pl.BlockSpec(shape, lambda *_: (0,) * len(shape), pipeline_mode=pl.Buffered(1))
    return pl.BlockSpec((None,) + tuple(shape), lambda *_: (layer,) + (0,) * len(shape),
                        pipeline_mode=pl.Buffered(1))


def _zero_like(x):
    bits = pltpu.bitcast(x, jnp.uint32)
    return pltpu.bitcast((bits >> 16) >> 16, f32)


def _rms_scale(x):
    return lax.rsqrt(jnp.mean(x * x, axis=-1, keepdims=True) + EPS)


def _gelu_tanh(x):
    c = math.sqrt(2.0 / math.pi)
    return x * (0.5 * (1.0 + jnp.tanh(c * (x + 0.044715 * (x * x * x)))))


def _sigmoid(x):
    return 1.0 / (1.0 + jnp.exp(-x))


def _dot(a, b):
    return jnp.dot(a, b, preferred_element_type=f32)


def _dot_tn(a, b):
    return lax.dot_general(a, b, (((0,), (0,)), ((), ())), preferred_element_type=f32)


def _dot_nt(a, b):
    return lax.dot_general(a, b, (((1,), (1,)), ((), ())), preferred_element_type=f32)


def _head_mean_sq(x, seg_ref):
    sq = x * x
    hi = sq.astype(bf16)
    lo = (sq - hi.astype(f32)).astype(bf16)
    seg = seg_ref[...]
    return _dot(hi, seg) + _dot(lo, seg)


def _swap16(x):
    lane = lax.broadcasted_iota(jnp.int32, x.shape, 1)
    first_half = (lane % 32) < 16
    return jnp.where(first_half, pltpu.roll(x, LANES - 16, 1), pltpu.roll(x, 16, 1))


def _swap16_rows(x):
    h = HEAD_DIM // 4
    return jnp.concatenate([x[h:2 * h], x[0:h], x[3 * h:4 * h], x[2 * h:3 * h]], axis=0)


def _in_proj_kernel(x_ref, g1_ref, w_ref, wqt_ref, wvt_ref, lng_ref, lnb_ref, wsp_ref, bsp_ref,
                    qgt_ref, kg_ref, cos_ref, sin_ref, cost_ref, sint_ref, seg_ref,
                    oa_ref, kb_ref, kc_ref, qbt_ref, qct_ref, vbt_ref, vct_ref):
    tm = x_ref.shape[0]
    x = x_ref[...]
    xn = (x * _rms_scale(x) * g1_ref[...]).astype(bf16)

    zv = _dot(xn, w_ref[:, A_WIDTH:A_IN])
    zu = _dot(xn, w_ref[:, 0:A_WIDTH])
    zk = _dot(xn, w_ref[:, A_IN:A_IN + 2 * KV_WIDTH])
    qbt = _dot_nt(wqt_ref[0:Q_WIDTH, :], xn)
    qt = _dot_nt(wqt_ref[Q_WIDTH:2 * Q_WIDTH, :], xn)
    vt = _dot_nt(wvt_ref[...], xn).astype(bf16)

    v = _gelu_tanh(zv)
    u = _gelu_tanh(zu)
    mu = jnp.mean(v, axis=-1, keepdims=True)
    vc = v - mu
    vn = vc * lax.rsqrt(jnp.mean(vc * vc, axis=-1, keepdims=True) + EPS)
    vn = vn * lng_ref[...] + lnb_ref[...]
    lane = lax.broadcasted_iota(jnp.int32, (BLOCK, LANES), 1)
    low_half = lane < HEAD_DIM
    for c in range(tm // BLOCK):
        rows = slice(c * BLOCK, (c + 1) * BLOCK)
        for j in range(A_WIDTH // LANES):
            cols = slice(j * LANES, (j + 1) * LANES)
            vp = vn[rows, cols]
            stacked = jnp.concatenate(
                [jnp.where(low_half, vp, 0.0), jnp.where(low_half, 0.0, vp)], axis=0)
            sv = _dot(wsp_ref[j], stacked.astype(bf16)) + bsp_ref[:, cols]
            oa_ref[rows, cols] = (u[rows, cols] * sv).astype(bf16)

    kb_ref[...] = zk[:, 0:KV_WIDTH].astype(bf16)

    t = zk[:, KV_WIDTH:2 * KV_WIDTH]
    tn = t * lax.rsqrt(_head_mean_sq(t, seg_ref) + EPS) * kg_ref[...]
    kc_ref[:, 0:KV_WIDTH] = (tn * cos_ref[...] + _swap16(tn) * sin_ref[...]).astype(bf16)
    lane = lax.broadcasted_iota(jnp.int32, (tm, KV_WIDTH), 1)
    kc_ref[:, KV_WIDTH:2 * KV_WIDTH] = jnp.where(lane == 0, 1.0, 0.0).astype(bf16)

    scale = HEAD_DIM ** -0.5 * LOG2E
    qbt_ref[...] = (qbt * scale).astype(bf16)
    cost = cost_ref[...]
    sint = sint_ref[...]
    for h in range(N_HEADS):
        rows = slice(h * HEAD_DIM, (h + 1) * HEAD_DIM)
        th = qt[rows]
        r = lax.rsqrt(jnp.mean(th * th, axis=0, keepdims=True) + EPS)
        tn = th * r * qgt_ref[...]
        qct_ref[rows, :] = ((tn * cost + _swap16_rows(tn) * sint) * scale).astype(bf16)
    for c in range(tm // BLOCK):
        vbt_ref[c] = vt[0:KV_WIDTH, c * BLOCK:(c + 1) * BLOCK]
    for c in range(tm // TK_FLASH):
        vct_ref[c] = vt[KV_WIDTH:2 * KV_WIDTH, c * TK_FLASH:(c + 1) * TK_FLASH]


def _in_proj(x2d, seq, g1, w_in, wqt, wvt, lng, lnb, wsp, bsp, qgt, kg, cos_t, sin_t,
             cos_tt, sin_tt, seg):
    t = x2d.shape[0]
    tm = TM_IN
    tiles_per_seq = seq // tm
    const = lambda i: (0, 0)
    return pl.pallas_call(
        _in_proj_kernel,
        grid=(t // tm,),
        in_specs=[
            pl.BlockSpec((tm, D_MODEL), lambda i: (i, 0)),
            pl.BlockSpec((1, D_MODEL), const),
            _resident((D_MODEL, w_in.shape[1])),
            _resident((2 * Q_WIDTH, D_MODEL)),
            _resident((2 * KV_WIDTH, D_MODEL)),
            pl.BlockSpec((1, A_WIDTH), const),
            pl.BlockSpec((1, A_WIDTH), const),
            pl.BlockSpec((A_WIDTH // LANES, BLOCK, 2 * BLOCK), lambda i: (0, 0, 0)),
            pl.BlockSpec((BLOCK, A_WIDTH), const),
            pl.BlockSpec((HEAD_DIM, tm), const),
            pl.BlockSpec((1, LANES), const),
            pl.BlockSpec((tm, LANES), lambda i: (i % tiles_per_seq, 0)),
            pl.BlockSpec((tm, LANES), lambda i: (i % tiles_per_seq, 0)),
            pl.BlockSpec((HEAD_DIM, tm), lambda i: (0, i % tiles_per_seq)),
            pl.BlockSpec((HEAD_DIM, tm), lambda i: (0, i % tiles_per_seq)),
            pl.BlockSpec((LANES, LANES), const),
        ],
        out_specs=[
            pl.BlockSpec((tm, A_WIDTH), lambda i: (i, 0)),
            pl.BlockSpec((tm, KV_WIDTH), lambda i: (i, 0)),
            pl.BlockSpec((tm, 2 * KV_WIDTH), lambda i: (i, 0)),
            pl.BlockSpec((Q_WIDTH, tm), lambda i: (0, i)),
            pl.BlockSpec((Q_WIDTH, tm), lambda i: (0, i)),
            pl.BlockSpec((tm // BLOCK, KV_WIDTH, BLOCK), lambda i: (i, 0, 0)),
            pl.BlockSpec((tm // TK_FLASH, KV_WIDTH, TK_FLASH), lambda i: (i, 0, 0)),
        ],
        out_shape=[
            jax.ShapeDtypeStruct((t, A_WIDTH), bf16),
            jax.ShapeDtypeStruct((t, KV_WIDTH), bf16),
            jax.ShapeDtypeStruct((t, 2 * KV_WIDTH), bf16),
            jax.ShapeDtypeStruct((Q_WIDTH, t), bf16),
            jax.ShapeDtypeStruct((Q_WIDTH, t), bf16),
            jax.ShapeDtypeStruct((t // BLOCK, KV_WIDTH, BLOCK), bf16),
            jax.ShapeDtypeStruct((t // TK_FLASH, KV_WIDTH, TK_FLASH), bf16),
        ],
        compiler_params=pltpu.CompilerParams(
            dimension_semantics=("parallel",), vmem_limit_bytes=VMEM_LIMIT),
        name="in_proj",
    )(x2d, g1, w_in, wqt, wvt, lng, lnb, wsp, bsp, qgt, kg, cos_t, sin_t, cos_tt, sin_tt, seg)


def _band_bias_kernel(rel_bias_ref, bucket_ref, bias_ref):
    bucket = bucket_ref[...]
    for h in range(N_HEADS):
        val = jnp.full(bucket.shape, -jnp.inf, f32)
        for b in range(N_BUCKETS):
            val = jnp.where(bucket == b, rel_bias_ref[b, h] * LOG2E, val)
        gi = h % GROUP
        bias_ref[h // GROUP, :, gi * BLOCK:(gi + 1) * BLOCK] = val


def _band_bias(rel_bias, bucket_t):
    return pl.pallas_call(
        _band_bias_kernel,
        in_specs=[pl.BlockSpec(memory_space=pltpu.SMEM),
                  pl.BlockSpec((3 * BLOCK, BLOCK), lambda: (0, 0))],
        out_specs=pl.BlockSpec((N_KV, 3 * BLOCK, GROUP * BLOCK), lambda: (0, 0, 0)),
        out_shape=jax.ShapeDtypeStruct((N_KV, 3 * BLOCK, GROUP * BLOCK), f32),
        name="band_bias",
    )(rel_bias, bucket_t)


def _band_attn_kernel(nb, qt_ref, kp_ref, km_ref, kn_ref, vp_ref, vm_ref, vn_ref,
                      bias_ref, sink_ref, ot_ref):
    first = (pl.program_id(0) * BAND_R) % nb == 0
    last = ((pl.program_id(0) + 1) * BAND_R) % nb == 0
    k_win = jnp.concatenate([kp_ref[...], km_ref[...], kn_ref[...]], axis=0)
    v_blocks = [vp_ref[0]] + [vm_ref[r] for r in range(BAND_R)] + [vn_ref[0]]
    ones = jnp.ones((2 * SUBLANES, 3 * BLOCK), bf16)
    zeros = jnp.zeros((HEAD_DIM, GROUP * BLOCK), bf16)
    units = [(r, kv) for r in range(BAND_R) for kv in range(N_KV)]

    def logits(r, kv):
        qg = jnp.concatenate(
            [qt_ref[h * HEAD_DIM:(h + 1) * HEAD_DIM, r * BLOCK:(r + 1) * BLOCK]
             for h in range(kv * GROUP, (kv + 1) * GROUP)], axis=1)
        q_ext = jnp.concatenate([qg, zeros] if kv == 0 else [zeros, qg], axis=0)
        return jnp.concatenate([_dot(k_win[(r + j) * BLOCK:(r + j + 1) * BLOCK], q_ext)
                                for j in range(3)], axis=0)

    s_queue = [logits(*u) for u in units[:BAND_LOOKAHEAD]]
    for i, (r, kv) in enumerate(units):
        if i + BAND_LOOKAHEAD < len(units):
            s_queue.append(logits(*units[i + BAND_LOOKAHEAD]))
        s = s_queue[i] + bias_ref[kv]
        s_queue[i] = None
        if r == 0:
            s = jnp.concatenate([jnp.where(first, -jnp.inf, s[0:BLOCK]), s[BLOCK:]], axis=0)
        if r == BAND_R - 1:
            s = jnp.concatenate([s[:2 * BLOCK], jnp.where(last, -jnp.inf, s[2 * BLOCK:])], axis=0)
        sink = sink_ref[kv]
        if i + 1 < len(units):
            sink = sink + _zero_like(s_queue[i + 1][0:1, :])
        m = jnp.maximum(jnp.max(s, axis=0, keepdims=True), sink)
        p = jnp.exp2(s - m).astype(bf16)
        v_band = jnp.concatenate(
            [blk[kv * HEAD_DIM:(kv + 1) * HEAD_DIM] for blk in v_blocks[r:r + 3]], axis=1)
        pv = _dot(jnp.concatenate([v_band, ones], axis=0), p)
        out = pv[0:HEAD_DIM] / (pv[HEAD_DIM:HEAD_DIM + 1] + jnp.exp2(sink - m))
        for gi in range(GROUP):
            h = kv * GROUP + gi
            ot_ref[h * HEAD_DIM:(h + 1) * HEAD_DIM, r * BLOCK:(r + 1) * BLOCK] = (
                out[:, gi * BLOCK:(gi + 1) * BLOCK].astype(bf16))


def _band_attn(qbt, kb, vbt, seq, sink_rows, bias_t):
    t = kb.shape[0]
    nb = seq // BLOCK
    nblocks = t // BLOCK
    r = BAND_R
    prev_blk = lambda i: jnp.maximum(i * r - 1, 0)
    next_blk = lambda i: jnp.minimum((i + 1) * r, nblocks - 1)
    return pl.pallas_call(
        functools.partial(_band_attn_kernel, nb),
        grid=(nblocks // r,),
        in_specs=[
            pl.BlockSpec((Q_WIDTH, r * BLOCK), lambda i: (0, i)),
            pl.BlockSpec((BLOCK, KV_WIDTH), lambda i: (prev_blk(i), 0)),
            pl.BlockSpec((r * BLOCK, KV_WIDTH), lambda i: (i, 0)),
            pl.BlockSpec((BLOCK, KV_WIDTH), lambda i: (next_blk(i), 0)),
            pl.BlockSpec((1, KV_WIDTH, BLOCK), lambda i: (prev_blk(i), 0, 0)),
            pl.BlockSpec((r, KV_WIDTH, BLOCK), lambda i: (i, 0, 0)),
            pl.BlockSpec((1, KV_WIDTH, BLOCK), lambda i: (next_blk(i), 0, 0)),
            pl.BlockSpec((N_KV, 3 * BLOCK, GROUP * BLOCK), lambda i: (0, 0, 0)),
            pl.BlockSpec((N_KV, 1, GROUP * BLOCK), lambda i: (0, 0, 0)),
        ],
        out_specs=pl.BlockSpec((Q_WIDTH, r * BLOCK), lambda i: (0, i)),
        out_shape=jax.ShapeDtypeStruct((Q_WIDTH, t), bf16),
        compiler_params=pltpu.CompilerParams(
            dimension_semantics=("parallel",), vmem_limit_bytes=VMEM_LIMIT),
        name="band_attn",
    )(qbt, kb, kb, kb, vbt, vbt, vbt, bias_t, sink_rows)


def _flash_queries(qt_ref, pad, cols=slice(None)):
    zeros = jnp.zeros((HEAD_DIM, pad.shape[1]), bf16)
    q_ext = []
    for kv in range(N_KV):
        qg = jnp.concatenate(
            [qt_ref[h * HEAD_DIM:(h + 1) * HEAD_DIM, cols]
             for h in range(kv * GROUP, (kv + 1) * GROUP)], axis=1)
        q_ext.append(jnp.concatenate(([qg, zeros] if kv == 0 else [zeros, qg]) + [pad], axis=0))
    return q_ext


def _flash_store(ot_ref, kv, out, cols=slice(None)):
    tq = out.shape[1] // GROUP
    for gi in range(GROUP):
        h = kv * GROUP + gi
        ot_ref[h * HEAD_DIM:(h + 1) * HEAD_DIM, cols] = out[:, gi * tq:(gi + 1) * tq].astype(bf16)


def _flash_online_kernel(qt_ref, k_ref, vt_ref, ot_ref):
    nq = GROUP * qt_ref.shape[1]
    n_chunks = vt_ref.shape[0]
    tk = vt_ref.shape[2]
    q_ext = _flash_queries(qt_ref, jnp.zeros((KV_WIDTH, nq), bf16))

    def logits(c):
        start = pl.multiple_of(c * tk, tk)
        kc = k_ref[pl.ds(start, tk), :]
        return tuple(_dot(kc, q_ext[kv]) for kv in range(N_KV))

    def body(c, carry):
        s_all, stats = carry
        s_next = logits(jnp.minimum(c + 1, n_chunks - 1))
        vt = vt_ref[c]
        out = []
        for kv in range(N_KV):
            m, l, acc = stats[kv]
            s = s_all[kv]
            m_new = jnp.maximum(m, jnp.max(s, axis=0, keepdims=True))
            alpha = jnp.exp2(m - m_new)
            p = jnp.exp2(s - m_new)
            l_new = alpha * l + jnp.sum(p, axis=0, keepdims=True)
            pv = _dot(vt[kv * HEAD_DIM:(kv + 1) * HEAD_DIM], p.astype(bf16))
            out.append((m_new, l_new, alpha * acc + pv))
        return s_next, tuple(out)

    init = tuple((jnp.full((1, nq), -jnp.inf, f32), jnp.zeros((1, nq), f32),
                  jnp.zeros((HEAD_DIM, nq), f32)) for _ in range(N_KV))
    _, final = lax.fori_loop(0, n_chunks, body, (logits(0), init))
    for kv in range(N_KV):
        _, l, acc = final[kv]
        _flash_store(ot_ref, kv, acc / l)


def _flash_bounded_kernel(qt_ref, pad_ref, k_ref, vt_ref, ot_ref):
    n_chunks = vt_ref.shape[0]
    tk = vt_ref.shape[2]
    ones = jnp.concatenate([jnp.ones((2 * SUBLANES, tk), bf16),
                            jnp.zeros((PV_ROWS - HEAD_DIM - 2 * SUBLANES, tk), bf16)], axis=0)
    units = [(c, kv) for c in range(n_chunks) for kv in range(N_KV)]

    for t in range(qt_ref.shape[1] // TQ_FLASH):
        cols = slice(t * TQ_FLASH, (t + 1) * TQ_FLASH)
        q_ext = _flash_queries(qt_ref, pad_ref[...], cols)

        def logits(c, kv):
            return _dot(k_ref[c * tk:(c + 1) * tk, :], q_ext[kv])

        s_queue = [logits(*u) for u in units[:FLASH_LOOKAHEAD]]
        acc = [None] * N_KV
        for i, (c, kv) in enumerate(units):
            if i + FLASH_LOOKAHEAD < len(units):
                s_queue.append(logits(*units[i + FLASH_LOOKAHEAD]))
            p = jnp.exp2(s_queue[i]).astype(bf16)
            s_queue[i] = None
            v_ext = jnp.concatenate(
                [vt_ref[c, kv * HEAD_DIM:(kv + 1) * HEAD_DIM, :], ones], axis=0)
            pv = _dot(v_ext, p)
            acc[kv] = pv if acc[kv] is None else acc[kv] + pv
        for kv in range(N_KV):
            _flash_store(ot_ref, kv, acc[kv][0:HEAD_DIM] / acc[kv][HEAD_DIM:HEAD_DIM + 1], cols)


def _flash_attn(qct, pad, kc, vct, seq, bounded):
    t = kc.shape[0]
    tq = TQ_FLASH * (FLASH_SUBTILES if bounded else 1)
    q_tiles = seq // tq
    tk = vct.shape[2]
    q_spec = pl.BlockSpec((Q_WIDTH, tq), lambda b, i: (0, b * q_tiles + i))
    kv_specs = [
        pl.BlockSpec((seq, 2 * KV_WIDTH), lambda b, i: (b, 0)),
        pl.BlockSpec((seq // tk, KV_WIDTH, tk), lambda b, i: (b, 0, 0)),
    ]
    pad_spec = pl.BlockSpec((KV_WIDTH, GROUP * TQ_FLASH), lambda b, i: (0, 0))
    return pl.pallas_call(
        _flash_bounded_kernel if bounded else _flash_online_kernel,
        grid=(t // seq, q_tiles),
        in_specs=[q_spec] + ([pad_spec] if bounded else []) + kv_specs,
        out_specs=q_spec,
        out_shape=jax.ShapeDtypeStruct((Q_WIDTH, t), bf16),
        compiler_params=pltpu.CompilerParams(
            dimension_semantics=("parallel", "parallel"), vmem_limit_bytes=VMEM_LIMIT),
        name="flash_bounded" if bounded else "flash_online",
    )(*([qct] + ([pad] if bounded else []) + [kc, vct]))


def _merge_kernel(x_ref, g1_ref, oa_ref, obt_ref, oct_ref, wg_ref, bg_ref, wbr_ref, wo_ref,
                  y_ref):
    tm = x_ref.shape[0]
    blocks = [slice(r, r + MERGE_ROWS) for r in range(0, tm, MERGE_ROWS)]

    def projections(rows):
        x = x_ref[rows, :]
        xn = (x * _rms_scale(x) * g1_ref[...]).astype(bf16)
        projs = (_dot(oa_ref[rows, :], wbr_ref[0]), _dot_tn(obt_ref[:, rows], wbr_ref[1]),
                 _dot_tn(oct_ref[:, rows], wbr_ref[2]))
        logits = [_dot(xn, wg_ref[:, n * D_MODEL:(n + 1) * D_MODEL]) for n in range(N_BRANCH)]
        return x, projs, logits

    staged = [projections(rows) for rows in blocks]
    for rows, (x, projs, logits) in zip(blocks, staged):
        merged = None
        for n, proj in enumerate(projs):
            gate = _sigmoid(logits[n] + bg_ref[:, n * D_MODEL:(n + 1) * D_MODEL])
            term = gate * proj
            merged = term if merged is None else merged + term
        y_ref[rows, :] = x + _dot(merged.astype(bf16), wo_ref[...])


def _merge(x2d, g1, oa, obt, oct, wg, bg, wbr, wo, layer):
    t = x2d.shape[0]
    tm = TM_MERGE
    const = lambda i: (0, 0)
    row = lambda width: pl.BlockSpec((tm, width), lambda i: (i, 0))
    return pl.pallas_call(
        _merge_kernel,
        grid=(t // tm,),
        in_specs=[
            row(D_MODEL),
            pl.BlockSpec((1, D_MODEL), const),
            row(BR_WIDTH),
            pl.BlockSpec((BR_WIDTH, tm), lambda i: (0, i)),
            pl.BlockSpec((BR_WIDTH, tm), lambda i: (0, i)),
            _resident((D_MODEL, N_BRANCH * D_MODEL), layer),
            pl.BlockSpec((1, N_BRANCH * D_MODEL), const),
            _resident((N_BRANCH, BR_WIDTH, D_MODEL), layer),
            _resident((D_MODEL, D_MODEL), layer),
        ],
        out_specs=row(D_MODEL),
        out_shape=jax.ShapeDtypeStruct((t, D_MODEL), f32),
        compiler_params=pltpu.CompilerParams(
            dimension_semantics=("parallel",), vmem_limit_bytes=VMEM_LIMIT),
        name="merge",
    )(x2d, g1, oa, obt, oct, wg, bg, wbr, wo)


def _conv_ffn_kernel(tiles_per_seq, apply_final, xp_ref, x_ref, xq_ref, g2_ref, wup_ref,
                     cw_ref, cb_ref, wdn_ref, gf_ref, y_ref, act_ref):
    tm = x_ref.shape[0]
    i = pl.program_id(0) % tiles_per_seq
    x = x_ref[...]
    halo_p = jnp.where(i > 0, xp_ref[...], 0.0)
    halo_n = jnp.where(i < tiles_per_seq - 1, xq_ref[...], 0.0)
    xe = jnp.concatenate([halo_p, x, halo_n], axis=0)
    xn = (xe * _rms_scale(xe) * g2_ref[...]).astype(bf16)
    rows = tm + 2 * SUBLANES
    n_chunks = D_FF // FF_CHUNK
    chunk_cols = lambda c: (slice(c * FF_CHUNK, (c + 1) * FF_CHUNK),
                            slice(D_FF + c * FF_CHUNK, D_FF + (c + 1) * FF_CHUNK))

    def up(c):
        edges = list(range(0, rows, FFN_UP_ROWS)) + [rows]
        return tuple(jnp.concatenate(
            [_dot(xn[a:b], wup_ref[:, cols]) for a, b in zip(edges[:-1], edges[1:])], axis=0)
            for cols in chunk_cols(c))

    h_queue = [up(c) for c in range(FFN_LOOKAHEAD)]
    for c in range(n_chunks):
        if c + FFN_LOOKAHEAD < n_chunks:
            h_queue.append(up(c + FFN_LOOKAHEAD))
        parts = []
        for h, cols in zip(h_queue[c], chunk_cols(c)):
            hc = (pltpu.roll(h, 1, 0) * cw_ref[0:1, cols]
                  + h * cw_ref[1:2, cols]
                  + pltpu.roll(h, rows - 1, 0) * cw_ref[2:3, cols]
                  + cb_ref[:, cols])
            parts.append(hc[SUBLANES:SUBLANES + tm])
        h_queue[c] = None
        act_ref[:, chunk_cols(c)[0]] = (parts[0] * _sigmoid(parts[0]) * parts[1]).astype(bf16)
    y = x + _dot(act_ref[...], wdn_ref[...])
    if apply_final:
        y = y * _rms_scale(y) * gf_ref[...]
    y_ref[...] = y


def _conv_ffn(x2d, seq, g2, wup, cw, cb, wdn, gf, layer, apply_final):
    t = x2d.shape[0]
    tm = TM_FFN
    tiles_per_seq = seq // tm
    halo_per_tile = tm // SUBLANES
    n_halo = t // SUBLANES
    const = lambda i: (0, 0)
    return pl.pallas_call(
        functools.partial(_conv_ffn_kernel, tiles_per_seq, apply_final),
        grid=(t // tm,),
        in_specs=[
            pl.BlockSpec((SUBLANES, D_MODEL),
                         lambda i: (jnp.maximum(i * halo_per_tile - 1, 0), 0)),
            pl.BlockSpec((tm, D_MODEL), lambda i: (i, 0)),
            pl.BlockSpec((SUBLANES, D_MODEL),
                         lambda i: (jnp.minimum((i + 1) * halo_per_tile, n_halo - 1), 0)),
            pl.BlockSpec((1, D_MODEL), const),
            _resident((D_MODEL, 2 * D_FF), layer),
            pl.BlockSpec((3, 2 * D_FF), const),
            pl.BlockSpec((1, 2 * D_FF), const),
            _resident((D_FF, D_MODEL), layer),
            pl.BlockSpec((1, D_MODEL), const),
        ],
        out_specs=pl.BlockSpec((tm, D_MODEL), lambda i: (i, 0)),
        out_shape=jax.ShapeDtypeStruct((t, D_MODEL), f32),
        scratch_shapes=[pltpu.VMEM((tm, D_FF), bf16)],
        compiler_params=pltpu.CompilerParams(
            dimension_semantics=("parallel",), vmem_limit_bytes=VMEM_LIMIT),
        name="conv_ffn",
    )(x2d, x2d, x2d, g2, wup, cw, cb, wdn, gf)


def _t5_bucket(rel):
    half = N_BUCKETS // 2
    max_exact = half // 2
    ret = jnp.where(rel > 0, half, 0)
    n = jnp.abs(rel)
    nf = jnp.maximum(n, 1).astype(f32)
    large = max_exact + (jnp.log(nf / max_exact) / math.log(MAX_DIST / max_exact)
                         * (half - max_exact)).astype(jnp.int32)
    large = jnp.minimum(large, half - 1)
    return ret + jnp.where(n < max_exact, n, large)


def _band_buckets():
    jpos = jnp.arange(3 * BLOCK)[:, None]
    qpos = jnp.arange(BLOCK)[None, :]
    rel = jpos - BLOCK - qpos
    return jnp.where(jnp.abs(rel) <= WINDOW, _t5_bucket(rel), -1).astype(jnp.int32)


def _rope_tables(seq):
    m = HEAD_DIM // 4
    pos = jnp.arange(seq)
    row = (pos // GRID_W).astype(f32)
    col = (pos % GRID_W).astype(f32)
    inv = ROPE_THETA ** (-jnp.arange(m, dtype=f32) / m)
    ang_r = row[:, None] * inv[None, :]
    ang_c = col[:, None] * inv[None, :]
    cos = jnp.concatenate([jnp.cos(ang_r), jnp.cos(ang_r), jnp.cos(ang_c), jnp.cos(ang_c)], axis=-1)
    sin = jnp.concatenate([-jnp.sin(ang_r), jnp.sin(ang_r), -jnp.sin(ang_c), jnp.sin(ang_c)], axis=-1)
    reps = LANES // HEAD_DIM
    return jnp.tile(cos, (1, reps)), jnp.tile(sin, (1, reps)), cos.T, sin.T


def _trunk(x, layers, stacked, bias, seg, final_g):
    bsz, seq, d = x.shape
    x2d = x.reshape(bsz * seq, d)
    cos_t, sin_t, cos_tt, sin_tt = _rope_tables(seq)
    for l, p in enumerate(layers):
        oa, kb, kc, qbt, qct, vbt, vct = _in_proj(
            x2d, seq, p["g1"], p["w_in"], p["wqt"], p["wvt"], p["lng"], p["lnb"], p["wsp"],
            p["bsp"], p["qgt"], p["kg"], cos_t, sin_t, cos_tt, sin_tt, seg)
        obt = _band_attn(qbt, kb, vbt, seq, p["sink"], bias)
        oct = lax.cond(
            p["logit_bound"] <= MAX_LOGIT_BOUND,
            lambda qct, pad, kc, vct: _flash_attn(qct, pad, kc, vct, seq, bounded=True),
            lambda qct, pad, kc, vct: _flash_attn(qct, pad, kc, vct, seq, bounded=False),
            qct, p["pad"], kc, vct)
        x2d = _merge(x2d, p["g1"], oa, obt, oct, stacked["wg"], p["bg"], stacked["wbr"],
                     stacked["wo"], l)
        x2d = _conv_ffn(x2d, seq, p["g2"], stacked["wup"], p["cw"], p["cb"], stacked["wdn"],
                        final_g, l, apply_final=(l == len(layers) - 1))
    return x2d.reshape(bsz, seq, d)


def kernel(x_prompt, x_sample, rel_bias, norm1_g, w_in, ln_v_g, ln_v_b, w_spatial, b_spatial,
           sink, q_norm_g, k_norm_g, w_gate, b_gate, w_branch, w_out, norm2_g, w_up, conv_w,
           conv_b, w_down, final_g):
    bias = _band_bias(rel_bias, _band_buckets())
    head_of_lane = jnp.arange(LANES) // HEAD_DIM
    seg = ((head_of_lane[:, None] == head_of_lane[None, :]).astype(f32) / HEAD_DIM).astype(bf16)
    reps = LANES // HEAD_DIM
    layers = []
    for l in range(DEPTH):
        ws = w_spatial[l].astype(bf16)
        wsp = jnp.concatenate([ws[0::2], ws[1::2]], axis=-1)
        wl = w_in[l].astype(bf16)
        b_q, b_k, b_v = A_IN, A_IN + Q_WIDTH, A_IN + Q_WIDTH + KV_WIDTH
        c_q, c_k, c_v = (b + QKV_WIDTH for b in (b_q, b_k, b_v))
        logit_bound = (1.02 * HEAD_DIM ** 0.5) * jnp.max(jnp.abs(q_norm_g[l])) * jnp.max(
            jnp.abs(k_norm_g[l]))
        shift = -(logit_bound * (LOG2E * (1.0 + 2.0 ** -7))).astype(bf16)
        pad = jnp.zeros((KV_WIDTH, GROUP * TQ_FLASH), bf16).at[0, :].set(shift)
        layers.append(dict(
            logit_bound=logit_bound,
            pad=pad,
            g1=norm1_g[l][None, :],
            w_in=jnp.concatenate(
                [wl[:, :A_IN], wl[:, b_k:b_k + KV_WIDTH], wl[:, c_k:c_k + KV_WIDTH]], axis=1),
            wqt=jnp.concatenate([wl[:, b_q:b_q + Q_WIDTH], wl[:, c_q:c_q + Q_WIDTH]], axis=1).T,
            wvt=jnp.concatenate([wl[:, b_v:b_v + KV_WIDTH], wl[:, c_v:c_v + KV_WIDTH]], axis=1).T,
            lng=ln_v_g[l][None, :],
            lnb=ln_v_b[l][None, :],
            wsp=wsp,
            bsp=jnp.repeat(b_spatial[l].T, HEAD_DIM, axis=1),
            qgt=jnp.broadcast_to(q_norm_g[l][:, None], (HEAD_DIM, TM_IN)),
            kg=jnp.tile(k_norm_g[l], reps)[None, :],
            sink=jnp.repeat(sink[l] * LOG2E, BLOCK).reshape(N_KV, 1, GROUP * BLOCK),
            bg=b_gate[l][None, :],
            g2=norm2_g[l][None, :],
            cw=conv_w[l],
            cb=conv_b[l][None, :],
        ))
    stacked = dict(wg=w_gate.astype(bf16), wbr=w_branch.astype(bf16), wo=w_out.astype(bf16),
                   wup=w_up.astype(bf16), wdn=w_down.astype(bf16))
    gf = final_g[None, :]
    y_prompt = _trunk(x_prompt, layers, stacked, bias, seg, gf)
    y_sample = _trunk(x_sample, layers, stacked, bias, seg, gf)
    return (y_prompt, y_sample)
```
